```python
import math
import jax, jax.numpy as jnp
from jax import lax
import numpy as np


D_MODEL = 1024
BATCH = 8
SEQ = 2048
DEPTH = 4
DEC_BATCH = 128
DEC_SEQ = 8
PAST_LEN = 16384
PAGE_SIZE = 128

N_MIXERS = 2
N_A = (DEPTH + 1) // 2
N_B = DEPTH // 2
D_RNN = D_MODEL
RG_HEADS = 8
RG_BLK = D_RNN // RG_HEADS
RG_CONV = 4
RG_C = 8.0
HG_DK = 128
HG_HEADS = D_MODEL // HG_DK
HG_DV = D_MODEL // HG_HEADS
HK = HG_HEADS * HG_DK
HV = HG_HEADS * HG_DV
HG_CHUNK = 64
D_FF = 2816
FFN_CONV = 3
ALPHA = (2.0 * DEPTH) ** 0.25
BETA = (8.0 * DEPTH) ** -0.25
LN_EPS = 1e-5
RMS_EPS = 1e-6
F_FLOOR = 1e-30

kernel_name = 'hawk_hgrn2_convffn_deepnorm_step'


def layer_norm(x, g, b):
    xf = x.astype(jnp.float32)
    mu = jnp.mean(xf, axis=-1, keepdims=True)
    var = jnp.mean(jnp.square(xf - mu), axis=-1, keepdims=True)
    y = (xf - mu) * lax.rsqrt(var + LN_EPS) * g.astype(jnp.float32) + b.astype(jnp.float32)
    return y.astype(x.dtype)


def causal_dwconv(x, buf, w, b):
    width = w.shape[0]
    L = x.shape[1]
    xp = jnp.concatenate([buf.astype(x.dtype), x], axis=1)
    y = sum(xp[:, j:j + L] * w[j] for j in range(width)) + b
    return y, xp[:, xp.shape[1] - (width - 1):]


def rglru_mixer(x, h0, conv_buf, w_in, w_conv, b_conv, w_gate, b_gate, lam, w_out):
    B, L, _ = x.shape
    gate, xb = jnp.split(x @ w_in, 2, axis=-1)
    xc, new_buf = causal_dwconv(xb, conv_buf, w_conv, b_conv)
    xh = xc.reshape(B, L, RG_HEADS, RG_BLK)
    gates = jnp.einsum('blhi,ghij->gblhj', xh, w_gate).reshape(2, B, L, D_RNN)
    gates = gates.astype(jnp.float32) + b_gate.astype(jnp.float32)[:, None, None, :]
    r = jax.nn.sigmoid(gates[0])
    ig = jax.nn.sigmoid(gates[1])
    log_a = RG_C * r * jax.nn.log_sigmoid(lam.astype(jnp.float32))
    a = jnp.exp(log_a)
    bvals = jnp.sqrt(jnp.maximum(-jnp.expm1(2.0 * log_a), 0.0)) * ig * xc.astype(jnp.float32)
    bvals = bvals.at[:, 0].add(a[:, 0] * h0.astype(jnp.float32))

    def combine(left, right):
        a1, b1 = left
        a2, b2 = right
        return a1 * a2, a2 * b1 + b2

    _, h = lax.associative_scan(combine, (a, bvals), axis=1)
    y = (h.astype(x.dtype) * jax.nn.gelu(gate)) @ w_out
    return y, h[:, -1], new_buf


def hgrn2_mixer(x, S0, lb, w_in, norm_g, w_out):
    B, L, _ = x.shape
    q, f, iv, g = jnp.split(x @ w_in, [HK, 2 * HK, 2 * HK + HV], axis=-1)
    lbf = lb.astype(jnp.float32)
    fg = lbf + (1.0 - lbf) * jax.nn.sigmoid(f.astype(jnp.float32))
    log_f = jnp.log(jnp.maximum(fg, F_FLOOR))
    k = 1.0 - fg
    q = jax.nn.silu(q.astype(jnp.float32)) * (HG_DK ** -0.5)
    v = iv.astype(jnp.float32)
    c = min(HG_CHUNK, L)
    n = -(-L // c)
    pad = n * c - L

    def to_chunks(t, d):
        t = jnp.pad(t, ((0, 0), (0, pad), (0, 0)))
        return t.reshape(B, n, c, HG_HEADS, d).transpose(1, 0, 3, 2, 4)

    mask = jnp.tril(jnp.ones((c, c), dtype=bool))[None, None, :, :, None]

    def step(S, inp):
        qc, kc, vc, lc = inp
        cum = jnp.cumsum(lc, axis=2)
        diff = cum[:, :, :, None, :] - cum[:, :, None, :, :]
        decay = jnp.where(mask, jnp.exp(jnp.minimum(diff, 0.0)), 0.0)
        A = jnp.einsum('bhtk,bhsk,bhtsk->bhts', qc, kc, decay)
        o = jnp.einsum('bhts,bhsv->bhtv', A, vc) + jnp.einsum('bhtk,bhkv->bhtv', qc * jnp.exp(cum), S)
        last = cum[:, :, -1:, :]
        S_new = jnp.exp(last[:, :, 0, :])[..., None] * S + jnp.einsum('bhsk,bhsv->bhkv', kc * jnp.exp(last - cum), vc)
        return S_new, o

    S, o = lax.scan(step, S0.astype(jnp.float32),
                    (to_chunks(q, HG_DK), to_chunks(k, HG_DK), to_chunks(v, HG_DV), to_chunks(log_f, HG_DK)))
    o = o.transpose(1, 0, 3, 2, 4).reshape(B, n * c, HG_HEADS, HG_DV)[:, :L]
    o = o * lax.rsqrt(jnp.mean(jnp.square(o), axis=-1, keepdims=True) + RMS_EPS) * norm_g.astype(jnp.float32)
    o = o * jax.nn.silu(g.astype(jnp.float32).reshape(B, L, HG_HEADS, HG_DV))
    y = o.reshape(B, L, HV).astype(x.dtype) @ w_out
    return y, S


def conv_ffn(x, buf, w_in, cw, cb, w_out):
    gate, up = jnp.split(x @ w_in, 2, axis=-1)
    gc, new_buf = causal_dwconv(gate, buf, cw, cb)
    return (jax.nn.gelu(gc) * up) @ w_out, new_buf


def trunk(x, rg_h, rg_conv, hg_s, ffn_conv, ln_g, ln_b, rg_w_in, rg_conv_w, rg_conv_b,
          rg_gate_w, rg_gate_b, rg_lambda, rg_w_out, lbs, hg_w_in, hg_norm_g, hg_w_out,
          ffn_w_in, ffn_conv_w, ffn_conv_b, ffn_w_out):
    new_h, new_rc, new_s, new_fc = [], [], [], []
    for i in range(DEPTH):
        j = i // N_MIXERS
        if i % N_MIXERS == 0:
            mix, h, cbuf = rglru_mixer(x, rg_h[j], rg_conv[j], rg_w_in[j], rg_conv_w[j], rg_conv_b[j],
                                       rg_gate_w[j], rg_gate_b[j], rg_lambda[j], rg_w_out[j])
            new_h.append(h.astype(rg_h.dtype))
            new_rc.append(cbuf.astype(rg_conv.dtype))
        else:
            mix, S = hgrn2_mixer(x, hg_s[j], lbs[j], hg_w_in[j], hg_norm_g[j], hg_w_out[j])
            new_s.append(S.astype(hg_s.dtype))
        x = layer_norm(ALPHA * x + mix, ln_g[i, 0], ln_b[i, 0])
        f, fbuf = conv_ffn(x, ffn_conv[i], ffn_w_in[i], ffn_conv_w[i], ffn_conv_b[i], ffn_w_out[i])
        new_fc.append(fbuf.astype(ffn_conv.dtype))
        x = layer_norm(ALPHA * x + f, ln_g[i, 1], ln_b[i, 1])
    return x, jnp.stack(new_h), jnp.stack(new_rc), jnp.stack(new_s), jnp.stack(new_fc)


def setup_inputs(seed: int = 0) -> dict:
    key = jax.random.key(seed)
    ks = jax.random.split(key, 24)
    f32 = jnp.float32

    def nrm(k, shape, s):
        return jax.random.normal(k, shape, f32) * s

    u = jax.random.uniform(ks[13], (N_A, D_RNN), f32, 0.9, 0.999)
    p = u ** (1.0 / RG_C)
    return {
        'x_prompt': nrm(ks[0], (BATCH, SEQ, D_MODEL), 1.0),
        'x_sample': nrm(ks[1], (DEC_BATCH, DEC_SEQ, D_MODEL), 1.0),
        'state_rglru_h': nrm(ks[2], (N_A, DEC_BATCH, D_RNN), 0.5),
        'state_rglru_conv': nrm(ks[3], (N_A, DEC_BATCH, RG_CONV - 1, D_RNN), 1.0),
        'state_hgrn_s': nrm(ks[4], (N_B, DEC_BATCH, HG_HEADS, HG_DK, HG_DV), 0.5),
        'state_ffn_conv': nrm(ks[5], (DEPTH, DEC_BATCH, FFN_CONV - 1, D_FF), 1.0),
        'ln_g': 1.0 + nrm(ks[6], (DEPTH, 2, D_MODEL), 0.02),
        'ln_b': nrm(ks[7], (DEPTH, 2, D_MODEL), 0.02),
        'rg_w_in': nrm(ks[8], (N_A, D_MODEL, 2 * D_RNN), D_MODEL ** -0.5),
        'rg_conv_w': nrm(ks[9], (N_A, RG_CONV, D_RNN), RG_CONV ** -0.5),
        'rg_conv_b': nrm(ks[10], (N_A, D_RNN), 0.02),
        'rg_gate_w': nrm(ks[11], (N_A, 2, RG_HEADS, RG_BLK, RG_BLK), RG_BLK ** -0.5),
        'rg_gate_b': nrm(ks[12], (N_A, 2, D_RNN), 0.02),
        'rg_lambda': jnp.log(p) - jnp.log1p(-p),
        'rg_w_out': nrm(ks[14], (N_A, D_RNN, D_MODEL), D_RNN ** -0.5 * BETA),
        'hg_lower': nrm(ks[15], (N_B, HK), 0.1),
        'hg_w_in': nrm(ks[16], (N_B, D_MODEL, 2 * HK + 2 * HV), D_MODEL ** -0.5),
        'hg_norm_g': 1.0 + nrm(ks[17], (N_B, HG_DV), 0.02),
        'hg_w_out': nrm(ks[18], (N_B, HV, D_MODEL), HV ** -0.5 * BETA),
        'ffn_w_in': nrm(ks[19], (DEPTH, D_MODEL, 2 * D_FF), D_MODEL ** -0.5),
        'ffn_conv_w': nrm(ks[20], (DEPTH, FFN_CONV, D_FF), FFN_CONV ** -0.5),
        'ffn_conv_b': nrm(ks[21], (DEPTH, D_FF), 0.02),
        'ffn_w_out': nrm(ks[22], (DEPTH, D_FF, D_MODEL), D_FF ** -0.5 * BETA),
    }


def reference(x_prompt, x_sample, state_rglru_h, state_rglru_conv, state_hgrn_s, state_ffn_conv,
              ln_g, ln_b, rg_w_in, rg_conv_w, rg_conv_b, rg_gate_w, rg_gate_b, rg_lambda, rg_w_out,
              hg_lower, hg_w_in, hg_norm_g, hg_w_out, ffn_w_in, ffn_conv_w, ffn_conv_b, ffn_w_out):
    sm = jax.nn.softmax(hg_lower.astype(jnp.float32), axis=0)
    lbs = jnp.maximum(jnp.cumsum(sm, axis=0) - sm[0], 0.0)
    weights = (ln_g, ln_b, rg_w_in, rg_conv_w, rg_conv_b, rg_gate_w, rg_gate_b, rg_lambda, rg_w_out,
               lbs, hg_w_in, hg_norm_g, hg_w_out, ffn_w_in, ffn_conv_w, ffn_conv_b, ffn_w_out)
    pb = x_prompt.shape[0]
    dt = x_prompt.dtype
    z_h = jnp.zeros((N_A, pb, D_RNN), dt)
    z_rc = jnp.zeros((N_A, pb, RG_CONV - 1, D_RNN), dt)
    z_s = jnp.zeros((N_B, pb, HG_HEADS, HG_DK, HG_DV), dt)
    z_fc = jnp.zeros((DEPTH, pb, FFN_CONV - 1, D_FF), dt)
    y_prompt, p_h, p_rc, p_s, p_fc = trunk(x_prompt, z_h, z_rc, z_s, z_fc, *weights)
    y_sample, s_h, s_rc, s_s, s_fc = trunk(x_sample, state_rglru_h, state_rglru_conv, state_hgrn_s,
                                           state_ffn_conv, *weights)
    return (y_prompt, y_sample, p_h, p_rc, p_s, p_fc, s_h, s_rc, s_s, s_fc)
```

```python
import functools

import jax
import jax.numpy as jnp
from jax import lax
from jax.experimental import pallas as pl
from jax.experimental.pallas import tpu as pltpu

F32 = jnp.float32
BF16 = jnp.bfloat16

D_MODEL = 1024
D_FF = 2816
HEADS = 8
HD = 128
RG_CONV = 4
FFN_CONV = 3
RG_C = 8.0
LN_EPS = 1e-5
RMS_EPS = 1e-6
F_FLOOR = 1e-30
SUBLANES = 8
LANES = 128
BLK = 128
PROMPT_CHUNK = 64
SAFE_DECAY = 80.0
FF_TILE = 256
TM_PROMPT = 512
VMEM_LIMIT = 56 * 1024 * 1024


def _dot(a, b):
    return jnp.dot(a, b, preferred_element_type=F32)


def _dot_nt(a, b):
    return lax.dot_general(a, b, (((1,), (1,)), ((), ())), preferred_element_type=F32)


def _resident(shape):
    nd = len(shape)
    return pl.BlockSpec(shape, lambda i: (0,) * nd, pipeline_mode=pl.Buffered(1))


def _deepnorm(x, f, g, b, alpha):
    z = alpha * x + f
    mu = jnp.mean(z, axis=-1, keepdims=True)
    zc = z - mu
    var = jnp.mean(zc * zc, axis=-1, keepdims=True)
    return zc * lax.rsqrt(var + LN_EPS) * g + b


def _params(n_axes=1):
    return pltpu.CompilerParams(dimension_semantics=("arbitrary",) * n_axes,
                                vmem_limit_bytes=VMEM_LIMIT)


def _ffn_kernel(*refs, tm, step, tiles_per_seq, has_state, alpha):
    if has_state:
        (x_ref, st_ref, wi_ref, cw_ref, cb_ref, wo_ref, g_ref, b_ref,
         y_ref, so_ref, gbuf, hbuf) = refs
        carry = None
    else:
        (x_ref, wi_ref, cw_ref, cb_ref, wo_ref, g_ref, b_ref,
         y_ref, so_ref, gbuf, hbuf, carry) = refs
    halo = gbuf.shape[0] - tm
    xb = x_ref[...].astype(BF16)
    if not has_state:
        @pl.when(pl.program_id(0) % tiles_per_seq == 0)
        def _():
            carry[...] = jnp.zeros_like(carry)
    for j in range(D_FF // FF_TILE):
        c0 = j * FF_TILE
        g = _dot(xb, wi_ref[:, c0:c0 + FF_TILE])
        u = _dot(xb, wi_ref[:, D_FF + c0:D_FF + c0 + FF_TILE])
        if has_state:
            gbuf[0:halo, :] = st_ref[:, c0:c0 + FF_TILE]
        else:
            gbuf[0:halo, :] = carry[:, c0:c0 + FF_TILE]
        gbuf[halo:halo + tm, :] = g
        g1 = gbuf[halo - step:halo - step + tm, :]
        g2 = gbuf[halo - 2 * step:halo - 2 * step + tm, :]
        cw = cw_ref[:, c0:c0 + FF_TILE]
        gc = g * cw[2:3] + g1 * cw[1:2] + g2 * cw[0:1] + cb_ref[:, c0:c0 + FF_TILE]
        hbuf[:, c0:c0 + FF_TILE] = (jax.nn.gelu(gc) * u).astype(BF16)
        if has_state:
            so_ref[:, c0:c0 + FF_TILE] = gbuf[tm:tm + halo, :]
        else:
            carry[:, c0:c0 + FF_TILE] = gbuf[tm:tm + halo, :]
            so_ref[0, :, c0:c0 + FF_TILE] = gbuf[halo + tm - 2:halo + tm, :]
    f = _dot(hbuf[...], wo_ref[...])
    y_ref[...] = _deepnorm(x_ref[...], f, g_ref[...], b_ref[...], alpha)


def _ffn_call(x, state_tm, wi, cw, cb, wo, g, b, *, seq_len, alpha):
    T = x.shape[0]
    has_state = state_tm is not None
    if has_state:
        tm, step, halo, tiles_per_seq = T, T // seq_len, state_tm.shape[0], 1
    else:
        tm, step, halo, tiles_per_seq = TM_PROMPT, 1, SUBLANES, seq_len // TM_PROMPT
    n_tiles = T // tm
    n_seq = T // seq_len
    row = pl.BlockSpec((tm, D_MODEL), lambda i: (i, 0))
    in_specs = [row]
    args = [x]
    if has_state:
        in_specs.append(_resident(state_tm.shape))
        args.append(state_tm)
    in_specs += [_resident(wi.shape), _resident(cw.shape), _resident(cb.shape),
                 _resident(wo.shape), _resident(g.shape), _resident(b.shape)]
    args += [wi, cw, cb, wo, g, b]
    scratch = [pltpu.VMEM((halo + tm, FF_TILE), F32), pltpu.VMEM((tm, D_FF), BF16)]
    if has_state:
        so_shape = jax.ShapeDtypeStruct(state_tm.shape, F32)
        so_spec = pl.BlockSpec(state_tm.shape, lambda i: (0, 0))
    else:
        so_shape = jax.ShapeDtypeStruct((n_seq, FFN_CONV - 1, D_FF), F32)
        so_spec = pl.BlockSpec((1, FFN_CONV - 1, D_FF), lambda i: (i // tiles_per_seq, 0, 0))
        scratch.append(pltpu.VMEM((halo, D_FF), F32))
    return pl.pallas_call(
        functools.partial(_ffn_kernel, tm=tm, step=step, tiles_per_seq=tiles_per_seq,
                          has_state=has_state, alpha=alpha),
        grid=(n_tiles,),
        in_specs=in_specs,
        out_specs=[row, so_spec],
        out_shape=[jax.ShapeDtypeStruct((T, D_MODEL), F32), so_shape],
        scratch_shapes=scratch,
        compiler_params=_params(),
        name="ffn_state" if has_state else "ffn_prompt",
    )(*args)


def _rglru_kernel(*refs, tm, step, tiles_per_seq, has_state, alpha):
    if has_state:
        (x_ref, h0_ref, cst_ref, wi_ref, cw_ref, cb_ref, wg_ref, bg_ref, lam_ref, wo_ref,
         g_ref, b_ref, y_ref, ho_ref, co_ref, xbuf, bbuf, gatebuf, ybuf) = refs
    else:
        (x_ref, wi_ref, cw_ref, cb_ref, wg_ref, bg_ref, lam_ref, wo_ref,
         g_ref, b_ref, y_ref, ho_ref, co_ref, xbuf, bbuf, gatebuf, ybuf,
         abuf, hin, ccarry, hcarry) = refs
    halo = xbuf.shape[0] - tm
    groups = tm // SUBLANES
    xb = x_ref[...].astype(BF16)
    if has_state:
        xbuf[0:halo, :] = cst_ref[...]
    else:
        @pl.when(pl.program_id(0) % tiles_per_seq == 0)
        def _():
            ccarry[...] = jnp.zeros_like(ccarry)
            hcarry[...] = jnp.zeros_like(hcarry)
        xbuf[0:halo, :] = ccarry[...]
    gatebuf[...] = jax.nn.gelu(_dot(xb, wi_ref[:, 0:D_MODEL]))
    xbuf[halo:halo + tm, :] = _dot(xb, wi_ref[:, D_MODEL:2 * D_MODEL])
    log_sig_lam = jax.nn.log_sigmoid(lam_ref[...])
    for h in range(HEADS):
        c0 = h * HD
        cw = cw_ref[:, c0:c0 + HD]
        xc = cb_ref[:, c0:c0 + HD] + xbuf[halo:halo + tm, c0:c0 + HD] * cw[RG_CONV - 1:RG_CONV]
        for j in range(RG_CONV - 1):
            back = (RG_CONV - 1 - j) * step
            xc = xc + xbuf[halo - back:halo - back + tm, c0:c0 + HD] * cw[j:j + 1]
        gt = _dot(xc.astype(BF16), wg_ref[h])
        bg = bg_ref[:, c0:c0 + HD]
        r = jax.nn.sigmoid(gt[:, 0:HD] + bg[0:1])
        ig = jax.nn.sigmoid(gt[:, HD:2 * HD] + bg[1:2])
        a = jnp.exp(RG_C * r * log_sig_lam[:, c0:c0 + HD])
        bv = jnp.sqrt(jnp.maximum(1.0 - a * a, 0.0)) * ig * xc
        if has_state:
            hh = h0_ref[:, c0:c0 + HD]
            for t in range(tm // step):
                hh = a[t * step:(t + 1) * step] * hh + bv[t * step:(t + 1) * step]
                bbuf[h, t * step:(t + 1) * step, :] = hh
            ho_ref[:, c0:c0 + HD] = hh
        else:
            abuf[h] = a
            bbuf[h] = bv
            pa = abuf[h, pl.ds(0, groups, stride=SUBLANES), :]
            pb = bbuf[h, pl.ds(0, groups, stride=SUBLANES), :]
            for k in range(1, SUBLANES):
                ak = abuf[h, pl.ds(k, groups, stride=SUBLANES), :]
                pb = ak * pb + bbuf[h, pl.ds(k, groups, stride=SUBLANES), :]
                pa = ak * pa
                abuf[h, pl.ds(k, groups, stride=SUBLANES), :] = pa
                bbuf[h, pl.ds(k, groups, stride=SUBLANES), :] = pb
    if has_state:
        co_ref[...] = xbuf[tm:tm + halo, :]
    else:
        ccarry[...] = xbuf[tm:tm + halo, :]
        co_ref[0] = xbuf[halo + tm - (RG_CONV - 1):halo + tm, :]

        def group_step(gi, hc):
            hin[:, pl.ds(gi, 1), :] = hc
            last = gi * SUBLANES + SUBLANES - 1
            return abuf[:, pl.ds(last, 1), :] * hc + bbuf[:, pl.ds(last, 1), :]

        hc = lax.fori_loop(0, groups, group_step, hcarry[:, 0:1, :])
        hcarry[:, 0:1, :] = hc
        for h in range(HEADS):
            ho_ref[0, :, h * HD:(h + 1) * HD] = hc[h]
            hv = hin[h]
            for k in range(SUBLANES):
                bbuf[h, pl.ds(k, groups, stride=SUBLANES), :] = (
                    abuf[h, pl.ds(k, groups, stride=SUBLANES), :] * hv
                    + bbuf[h, pl.ds(k, groups, stride=SUBLANES), :])
    for h in range(HEADS):
        c0 = h * HD
        ybuf[:, c0:c0 + HD] = (bbuf[h] * gatebuf[:, c0:c0 + HD]).astype(BF16)
    f = _dot(ybuf[...], wo_ref[...])
    y_ref[...] = _deepnorm(x_ref[...], f, g_ref[...], b_ref[...], alpha)


def _rglru_call(x, h0, cst_tm, wi, cw, cb, wg, bg, lam, wo, g, b, *, seq_len, alpha):
    T = x.shape[0]
    has_state = h0 is not None
    if has_state:
        tm, step, halo, tiles_per_seq = T, T // seq_len, cst_tm.shape[0], 1
    else:
        tm, step, halo, tiles_per_seq = TM_PROMPT, 1, SUBLANES, seq_len // TM_PROMPT
    n_tiles = T // tm
    n_seq = T // seq_len
    row = pl.BlockSpec((tm, D_MODEL), lambda i: (i, 0))
    in_specs = [row]
    args = [x]
    if has_state:
        in_specs += [_resident(h0.shape), _resident(cst_tm.shape)]
        args += [h0, cst_tm]
    weights = [wi, cw, cb, wg, bg, lam, wo, g, b]
    in_specs += [_resident(w.shape) for w in weights]
    args += weights
    scratch = [pltpu.VMEM((halo + tm, D_MODEL), F32), pltpu.VMEM((HEADS, tm, HD), F32),
               pltpu.VMEM((tm, D_MODEL), F32), pltpu.VMEM((tm, D_MODEL), BF16)]
    if has_state:
        out_shape = [jax.ShapeDtypeStruct((T, D_MODEL), F32),
                     jax.ShapeDtypeStruct(h0.shape, F32),
                     jax.ShapeDtypeStruct(cst_tm.shape, F32)]
        out_specs = [row, pl.BlockSpec(h0.shape, lambda i: (0, 0)),
                     pl.BlockSpec(cst_tm.shape, lambda i: (0, 0))]
    else:
        out_shape = [jax.ShapeDtypeStruct((T, D_MODEL), F32),
                     jax.ShapeDtypeStruct((n_seq, 1, D_MODEL), F32),
                     jax.ShapeDtypeStruct((n_seq, RG_CONV - 1, D_MODEL), F32)]
        out_specs = [row,
                     pl.BlockSpec((1, 1, D_MODEL), lambda i: (i // tiles_per_seq, 0, 0)),
                     pl.BlockSpec((1, RG_CONV - 1, D_MODEL), lambda i: (i // tiles_per_seq, 0, 0))]
        scratch += [pltpu.VMEM((HEADS, tm, HD), F32), pltpu.VMEM((HEADS, tm // SUBLANES, HD), F32),
                    pltpu.VMEM((halo, D_MODEL), F32), pltpu.VMEM((HEADS, SUBLANES, HD), F32)]
    return pl.pallas_call(
        functools.partial(_rglru_kernel, tm=tm, step=step, tiles_per_seq=tiles_per_seq,
                          has_state=has_state, alpha=alpha),
        grid=(n_tiles,),
        in_specs=in_specs,
        out_specs=out_specs,
        out_shape=out_shape,
        scratch_shapes=scratch,
        compiler_params=_params(),
        name="rglru_state" if has_state else "rglru_prompt",
    )(*args)


def _split3(x):
    hi = x.astype(BF16)
    r = x - hi.astype(F32)
    mid = r.astype(BF16)
    lo = (r - mid.astype(F32)).astype(BF16)
    return hi, mid, lo


def _lower_bound(lower_ref, h, layer):
    rows = [lower_ref[n, h] for n in range(lower_ref.shape[0])]
    m = functools.reduce(jnp.maximum, rows)
    es = [jnp.exp(r - m) for r in rows]
    tot = functools.reduce(lambda p, q: p + q, es)
    sm = [e / tot for e in es]
    cs = functools.reduce(lambda p, q: p + q, sm[:layer + 1])
    return jnp.maximum(cs - sm[0], 0.0)


def _chunk_cumsum(cb_ref, tb_ref, eb_ref, tm, chunk):
    groups = tm // SUBLANES
    p = cb_ref[pl.ds(0, groups, stride=SUBLANES), :]
    for k in range(1, SUBLANES):
        p = p + cb_ref[pl.ds(k, groups, stride=SUBLANES), :]
        cb_ref[pl.ds(k, groups, stride=SUBLANES), :] = p
    if chunk > SUBLANES:
        gc = chunk // SUBLANES
        nch = tm // chunk
        tb_ref[...] = p
        e = jnp.zeros((nch, HD), F32)
        eb_ref[pl.ds(0, nch, stride=gc), :] = e
        for j in range(1, gc):
            e = e + tb_ref[pl.ds(j - 1, nch, stride=gc), :]
            eb_ref[pl.ds(j, nch, stride=gc), :] = e
        off = eb_ref[...]
        for k in range(SUBLANES):
            cb_ref[pl.ds(k, groups, stride=SUBLANES), :] = (
                cb_ref[pl.ds(k, groups, stride=SUBLANES), :] + off)
    return p


def _hgrn_front(xb_ref, wi_ref, lower_ref, h, layer, tm, chunk,
                cb_ref, tb_ref, eb_ref, kk_ref, v_ref, sg_ref, qs_ref, qe_ref):
    P = _dot(xb_ref[...], wi_ref[h])
    lb = _lower_bound(lower_ref, h, layer)
    fg = lb + (1.0 - lb) * jax.nn.sigmoid(P[:, HD:2 * HD])
    cb_ref[...] = jnp.log(jnp.maximum(fg, F_FLOOR))
    kk_ref[...] = 1.0 - fg
    q = P[:, 0:HD]
    qs = q * jax.nn.sigmoid(q) * (HD ** -0.5)
    v_ref[...] = P[:, 2 * HD:3 * HD]
    gg = P[:, 3 * HD:4 * HD]
    sg_ref[...] = gg * jax.nn.sigmoid(gg)
    totals = _chunk_cumsum(cb_ref, tb_ref, eb_ref, tm, chunk)
    cum = cb_ref[...]
    qs_ref[...] = qs
    qe_ref[...] = qs * jnp.exp(cum)
    safe = jnp.max(-cum) <= SAFE_DECAY
    return safe, totals


def _same_chunk_scores(a_ref, safe, r0, chunk, cb_ref, kk_ref, qs_ref, qe_ref):
    ri = lax.broadcasted_iota(jnp.int32, (BLK, BLK), 0)
    ci = lax.broadcasted_iota(jnp.int32, (BLK, BLK), 1)
    lc = chunk.bit_length() - 1
    rows = pl.ds(r0, BLK)

    @pl.when(safe)
    def _():
        kinv = kk_ref[rows, :] * jnp.exp(-cb_ref[rows, :])
        s = _dot_nt(qe_ref[rows, :].astype(BF16), kinv.astype(BF16))
        keep = ((ri >> lc) == (ci >> lc)) & (ci <= ri)
        a_ref[...] = jnp.where(keep, s, 0.0)

    @pl.when(jnp.logical_not(safe))
    def _():
        cum = cb_ref[rows, :]
        qs = qs_ref[rows, :]
        kk = kk_ref[rows, :]
        hi, mid, lo = _split3(cum)
        acc = jnp.where(ri == ci, _dot_nt(qs.astype(BF16), kk.astype(BF16)), 0.0)
        for lvl in range(lc):
            m = 1 << lvl
            pivot = ((ri >> (lvl + 1)) << (lvl + 1)) + (m - 1)
            sel = jnp.where(ci == pivot, 1.0, 0.0).astype(BF16)
            ref_cum = _dot(sel, hi) + _dot(sel, mid) + _dot(sel, lo)
            e = jnp.exp(-jnp.abs(cum - ref_cum))
            keep = (((ri >> (lvl + 1)) == (ci >> (lvl + 1)))
                    & ((ri & (2 * m - 1)) >= m) & ((ci & (2 * m - 1)) < m))
            s = _dot_nt((qs * e).astype(BF16), (kk * e).astype(BF16))
            acc = acc + jnp.where(keep, s, 0.0)
        a_ref[...] = acc


def _rms_gate(o, ng, sg):
    return o * lax.rsqrt(jnp.mean(o * o, axis=-1, keepdims=True) + RMS_EPS) * ng * sg


def _hgrn_prompt_kernel(x_ref, lower_ref, wi_ref, ng_ref, wo_ref, g_ref, b_ref, y_ref, so_ref,
                        xb_ref, cb_ref, tb_ref, eb_ref, kk_ref, v_ref, sg_ref, qs_ref, qe_ref,
                        a_ref, oh_ref, ob_ref, s_ref, *, tm, layer, tiles_per_seq, alpha):
    chunk = PROMPT_CHUNK
    assert BLK == 2 * chunk
    i = pl.program_id(0)

    @pl.when(i % tiles_per_seq == 0)
    def _():
        s_ref[...] = jnp.zeros_like(s_ref)

    xb_ref[...] = x_ref[...].astype(BF16)
    second = lax.broadcasted_iota(jnp.int32, (BLK, HD), 0) >= chunk
    ri = lax.broadcasted_iota(jnp.int32, (BLK, BLK), 0)
    ci = lax.broadcasted_iota(jnp.int32, (BLK, BLK), 1)
    cross = (ri >= chunk) & (ci < chunk)

    def head(h, _):
        safe, _ = _hgrn_front(xb_ref, wi_ref, lower_ref, h, layer, tm, chunk,
                              cb_ref, tb_ref, eb_ref, kk_ref, v_ref, sg_ref, qs_ref, qe_ref)
        S = s_ref[h]
        for nb in range(tm // BLK):
            r0 = nb * BLK
            rows = pl.ds(r0, BLK)
            _same_chunk_scores(a_ref, safe, r0, chunk, cb_ref, kk_ref, qs_ref, qe_ref)
            cum = cb_ref[rows, :]
            last0 = cb_ref[pl.ds(r0 + chunk - 1, 1), :]
            last1 = cb_ref[pl.ds(r0 + BLK - 1, 1), :]
            qe = qe_ref[rows, :]
            v16 = v_ref[rows, :].astype(BF16)
            kd = kk_ref[rows, :] * jnp.exp(jnp.where(second, last1, last0) - cum)
            a = a_ref[...] + jnp.where(cross, _dot_nt(qe.astype(BF16), kd.astype(BF16)), 0.0)
            o = _dot(a.astype(BF16), v16)
            q_blk = qe * jnp.where(second, jnp.exp(last0), 1.0)
            o = o + _dot(q_blk.astype(BF16), S.astype(BF16))
            k_blk = kd * jnp.where(second, 1.0, jnp.exp(last1))
            e_col = jnp.broadcast_to(jnp.exp(last0 + last1), (HD, HD)).T
            S = e_col * S + _dot(k_blk.T.astype(BF16), v16)
            oh_ref[rows, :] = o
        s_ref[h] = S
        ob_ref[h] = _rms_gate(oh_ref[...], ng_ref[...], sg_ref[...]).astype(BF16)
        return 0

    lax.fori_loop(0, HEADS, head, 0)
    for h in range(HEADS):
        xb_ref[:, h * HD:(h + 1) * HD] = ob_ref[h]
    f = _dot(xb_ref[...], wo_ref[...])
    y_ref[...] = _deepnorm(x_ref[...], f, g_ref[...], b_ref[...], alpha)

    @pl.when(i % tiles_per_seq == tiles_per_seq - 1)
    def _():
        so_ref[0] = s_ref[...]


def _hgrn_prep(hg_lower, wi, ng, wo):
    n_layers = hg_lower.shape[0]
    lower = hg_lower.reshape(n_layers, HEADS, 1, HD)
    wih = wi.reshape(D_MODEL, 4, HEADS, HD).transpose(2, 0, 1, 3).reshape(HEADS, D_MODEL, 4 * HD)
    return lower, wih.astype(BF16), ng.reshape(1, HD), wo.astype(BF16)


def _hgrn_prompt_call(x, prep, layer, g, b, *, seq_len, alpha):
    lower, wih, ng, wo = prep
    T = x.shape[0]
    tm = TM_PROMPT
    tiles_per_seq = seq_len // tm
    n_seq = T // seq_len
    row = pl.BlockSpec((tm, D_MODEL), lambda i: (i, 0))
    weights = [lower, wih, ng, wo, g, b]
    head_buf = pltpu.VMEM((tm, HD), F32)
    scratch = [pltpu.VMEM((tm, D_MODEL), BF16),
               head_buf,
               pltpu.VMEM((tm // SUBLANES, HD), F32),
               pltpu.VMEM((tm // SUBLANES, HD), F32),
               head_buf, head_buf, head_buf, head_buf, head_buf,
               pltpu.VMEM((BLK, BLK), F32),
               head_buf,
               pltpu.VMEM((HEADS, tm, HD), BF16),
               pltpu.VMEM((HEADS, HD, HD), F32)]
    y, so = pl.pallas_call(
        functools.partial(_hgrn_prompt_kernel, tm=tm, layer=layer,
                          tiles_per_seq=tiles_per_seq, alpha=alpha),
        grid=(T // tm,),
        in_specs=[row] + [_resident(w.shape) for w in weights],
        out_specs=[row, pl.BlockSpec((1, HEADS, HD, HD), lambda i: (i // tiles_per_seq, 0, 0, 0))],
        out_shape=[jax.ShapeDtypeStruct((T, D_MODEL), F32),
                   jax.ShapeDtypeStruct((n_seq, HEADS, HD, HD), F32)],
        scratch_shapes=scratch,
        compiler_params=_params(),
        name="hgrn_prompt",
    )(x, *weights)
    return y, so


def _hgrn_sample_front_kernel(x_ref, lower_ref, wi_ref,
                              oi_ref, qe_ref, v_ref, sg_ref, kdt_ref, et_ref,
                              xb_ref, cb_ref, kk_ref, qs_ref, a_ref, last_ref,
                              *, tm, chunk, layer):
    assert chunk == SUBLANES
    xb_ref[...] = x_ref[...].astype(BF16)
    groups = tm // SUBLANES

    def head(h, _):
        safe, totals = _hgrn_front(xb_ref, wi_ref, lower_ref, h, layer, tm, chunk,
                                   cb_ref, None, None, kk_ref, v_ref.at[h], sg_ref.at[h],
                                   qs_ref, qe_ref.at[h])
        for k in range(SUBLANES):
            last_ref[pl.ds(k, groups, stride=SUBLANES), :] = totals
        for nb in range(tm // BLK):
            r0 = nb * BLK
            rows = pl.ds(r0, BLK)
            _same_chunk_scores(a_ref, safe, r0, chunk, cb_ref, kk_ref, qs_ref, qe_ref.at[h])
            oi_ref[h, rows, :] = _dot(a_ref[...].astype(BF16), v_ref[h, rows, :].astype(BF16))
            last = last_ref[rows, :]
            kd = kk_ref[rows, :] * jnp.exp(last - cb_ref[rows, :])
            kdt_ref[h, :, r0:r0 + BLK] = kd.T
            et_ref[h, :, r0:r0 + BLK] = jnp.exp(last).T
        return 0

    lax.fori_loop(0, HEADS, head, 0)


def _hgrn_sample_state_kernel(qe_ref, oi_ref, v_ref, sg_ref, kdt_ref, et_ref, s0_ref, ng_ref,
                              op_ref, so_ref, *, chunk):
    row_seq = lax.broadcasted_iota(jnp.int32, (BLK, HD), 0) >> (chunk.bit_length() - 1)

    def head(h, _):
        kdt = kdt_ref[h].astype(BF16)
        et = et_ref[h]
        v = v_ref[h]
        for sq in range(BLK // chunk):
            rows = slice(sq * chunk, (sq + 1) * chunk)
            s0 = s0_ref[sq, h]
            o = oi_ref[h, rows, :] + _dot(qe_ref[h, rows, :].astype(BF16), s0.astype(BF16))
            op_ref[h, rows, :] = _rms_gate(o, ng_ref[...], sg_ref[h, rows, :])
            vm = jnp.where(row_seq == sq, v, 0.0).astype(BF16)
            e_col = jnp.broadcast_to(et[:, sq * chunk:sq * chunk + 1], (HD, HD))
            so_ref[sq, h] = e_col * s0 + _dot(kdt, vm)
        return 0

    lax.fori_loop(0, HEADS, head, 0)


def _hgrn_sample_out_kernel(x_ref, op_ref, wo_ref, g_ref, b_ref, y_ref, ob_ref, *, alpha):
    for h in range(HEADS):
        ob_ref[:, h * HD:(h + 1) * HD] = op_ref[h].astype(BF16)
    f = _dot(ob_ref[...], wo_ref[...])
    y_ref[...] = _deepnorm(x_ref[...], f, g_ref[...], b_ref[...], alpha)


def _hgrn_sample_call(x, s0, prep, layer, g, b, *, seq_len, alpha):
    lower, wih, ng, wo = prep
    T = x.shape[0]
    chunk = seq_len
    heads_rows = jax.ShapeDtypeStruct((HEADS, T, HD), F32)
    heads_cols = jax.ShapeDtypeStruct((HEADS, HD, T), F32)
    full = lambda s: pl.BlockSpec(s, lambda i: (0,) * len(s))
    head_buf = pltpu.VMEM((T, HD), F32)
    oi, qe, v, sg, kdt, et = pl.pallas_call(
        functools.partial(_hgrn_sample_front_kernel, tm=T, chunk=chunk, layer=layer),
        grid=(1,),
        in_specs=[full(x.shape), full(lower.shape), full(wih.shape)],
        out_specs=[full(heads_rows.shape)] * 4 + [full(heads_cols.shape)] * 2,
        out_shape=[heads_rows] * 4 + [heads_cols] * 2,
        scratch_shapes=[pltpu.VMEM((T, D_MODEL), BF16), head_buf, head_buf, head_buf,
                        pltpu.VMEM((BLK, BLK), F32), head_buf],
        compiler_params=_params(),
        name="hgrn_sample_front",
    )(x, lower, wih)
    seqs = BLK // chunk
    by_rows = pl.BlockSpec((HEADS, BLK, HD), lambda i: (0, i, 0))
    by_cols = pl.BlockSpec((HEADS, HD, BLK), lambda i: (0, 0, i))
    state = pl.BlockSpec((seqs, HEADS, HD, HD), lambda i: (i, 0, 0, 0))
    op, so = pl.pallas_call(
        functools.partial(_hgrn_sample_state_kernel, chunk=chunk),
        grid=(T // BLK,),
        in_specs=[by_rows, by_rows, by_rows, by_rows, by_cols, by_cols, state, _resident(ng.shape)],
        out_specs=[by_rows, state],
        out_shape=[heads_rows, jax.ShapeDtypeStruct(s0.shape, F32)],
        compiler_params=_params(),
        name="hgrn_sample_state",
    )(qe, oi, v, sg, kdt, et, s0, ng)
    y = pl.pallas_call(
        functools.partial(_hgrn_sample_out_kernel, alpha=alpha),
        grid=(1,),
        in_specs=[full(x.shape), full(op.shape), full(wo.shape), full(g.shape), full(b.shape)],
        out_specs=full(x.shape),
        out_shape=jax.ShapeDtypeStruct(x.shape, F32),
        scratch_shapes=[pltpu.VMEM((T, D_MODEL), BF16)],
        compiler_params=_params(),
        name="hgrn_sample_out",
    )(x, op, wo, g, b)
    return y, so


def _to_time_major(a):
    return jnp.transpose(a, (1, 0, 2)).reshape(-1, a.shape[-1])


def _from_time_major(a, batch):
    return jnp.transpose(a.reshape(-1, batch, a.shape[-1]), (1, 0, 2))


def kernel(x_prompt, x_sample, state_rglru_h, state_rglru_conv, state_hgrn_s, state_ffn_conv,
           ln_g, ln_b, rg_w_in, rg_conv_w, rg_conv_b, rg_gate_w, rg_gate_b, rg_lambda, rg_w_out,
           hg_lower, hg_w_in, hg_norm_g, hg_w_out, ffn_w_in, ffn_conv_w, ffn_conv_b, ffn_w_out):
    depth = ln_g.shape[0]
    alpha = (2.0 * depth) ** 0.25
    pb, pl_len, _ = x_prompt.shape
    sb, sl_len, _ = x_sample.shape

    xp = x_prompt.reshape(pb * pl_len, D_MODEL)
    xs = _to_time_major(x_sample)
    p_h, p_rc, p_s, p_fc = [], [], [], []
    s_h, s_rc, s_s, s_fc = [], [], [], []
    for i in range(depth):
        j = i // 2
        g0, b0 = ln_g[i, 0][None], ln_b[i, 0][None]
        if i % 2 == 0:
            wgc = jnp.concatenate([rg_gate_w[j, 0], rg_gate_w[j, 1]], axis=-1).astype(BF16)
            w = (rg_w_in[j].astype(BF16), rg_conv_w[j], rg_conv_b[j][None], wgc, rg_gate_b[j],
                 rg_lambda[j][None], rg_w_out[j].astype(BF16), g0, b0)
            xp, ho, co = _rglru_call(xp, None, None, *w, seq_len=pl_len, alpha=alpha)
            p_h.append(ho[:, 0])
            p_rc.append(co)
            xs, ho, co = _rglru_call(xs, state_rglru_h[j], _to_time_major(state_rglru_conv[j]), *w,
                                     seq_len=sl_len, alpha=alpha)
            s_h.append(ho)
            s_rc.append(_from_time_major(co, sb))
        else:
            prep = _hgrn_prep(hg_lower, hg_w_in[j], hg_norm_g[j], hg_w_out[j])
            xp, so = _hgrn_prompt_call(xp, prep, j, g0, b0, seq_len=pl_len, alpha=alpha)
            p_s.append(so)
            xs_bm = _from_time_major(xs, sb).reshape(sb * sl_len, D_MODEL)
            ys_bm, so = _hgrn_sample_call(xs_bm, state_hgrn_s[j], prep, j, g0, b0,
                                          seq_len=sl_len, alpha=alpha)
            s_s.append(so)
            xs = _to_time_major(ys_bm.reshape(sb, sl_len, D_MODEL))
        w = (ffn_w_in[i].astype(BF16), ffn_conv_w[i], ffn_conv_b[i][None], ffn_w_out[i].astype(BF16),
             ln_g[i, 1][None], ln_b[i, 1][None])
        xp, fo = _ffn_call(xp, None, *w, seq_len=pl_len, alpha=alpha)
        p_fc.append(fo)
        xs, fo = _ffn_call(xs, _to_time_major(state_ffn_conv[i]), *w, seq_len=sl_len, alpha=alpha)
        s_fc.append(_from_time_major(fo, sb))
    return (xp.reshape(x_prompt.shape), _from_time_major(xs, sb),
            jnp.stack(p_h), jnp.stack(p_rc), jnp.stack(p_s), jnp.stack(p_fc),
            jnp.stack(s_h), jnp.stack(s_rc), jnp.stack(s_s), jnp.stack(s_fc))
```

```python
import functools

import jax
import jax.numpy as jnp
from jax import lax
from jax.experimental import pallas as pl
from jax.experimental.pallas import tpu as pltpu

F32 = jnp.float32
BF16 = jnp.bfloat16

D_MODEL = 1024
D_FF = 2816
HEADS = 8
HD = 128
RG_CONV = 4
FFN_CONV = 3
RG_C = 8.0
LN_EPS = 1e-5
RMS_EPS = 1e-6
F_FLOOR = 1e-30
SUBLANES = 8
LANES = 128
BLK = 128
PROMPT_CHUNK = 64
SAFE_DECAY = 80.0
FF_TILE = 256
TM_PROMPT = 512
VMEM_LIMIT = 56 * 1024 * 1024


def _dot(a, b):
    return jnp.dot(a, b, preferred_element_type=F32)


def _dot_nt(a, b):
    return lax.dot_general(a, b, (((1,), (1,)), ((), ())), preferred_element_type=F32)


def _sigmoid(x):
    return 0.5 * jnp.tanh(0.5 * x) + 0.5


def _resident(shape):
    nd = len(shape)
    return pl.BlockSpec(shape, lambda i: (0,) * nd, pipeline_mode=pl.Buffered(1))


def _deepnorm(x, f, g, b, alpha):
    z = alpha * x + f
    mu = jnp.mean(z, axis=-1, keepdims=True)
    zc = z - mu
    var = jnp.mean(zc * zc, axis=-1, keepdims=True)
    return zc * lax.rsqrt(var + LN_EPS) * g + b


def _params(n_axes=1):
    return pltpu.CompilerParams(dimension_semantics=("arbitrary",) * n_axes,
                                vmem_limit_bytes=VMEM_LIMIT)


def _ffn_kernel(*refs, tm, step, tiles_per_seq, has_state, alpha):
    if has_state:
        (x_ref, st_ref, wi_ref, cw_ref, cb_ref, wo_ref, g_ref, b_ref,
         y_ref, so_ref, gbuf, hbuf) = refs
        carry = None
    else:
        (x_ref, wi_ref, cw_ref, cb_ref, wo_ref, g_ref, b_ref,
         y_ref, so_ref, gbuf, hbuf, carry) = refs
    halo = gbuf.shape[0] - tm
    xb = x_ref[...].astype(BF16)
    if not has_state:
        @pl.when(pl.program_id(0) % tiles_per_seq == 0)
        def _():
            carry[...] = jnp.zeros_like(carry)
    for j in range(D_FF // FF_TILE):
        c0 = j * FF_TILE
        g = _dot(xb, wi_ref[:, c0:c0 + FF_TILE])
        u = _dot(xb, wi_ref[:, D_FF + c0:D_FF + c0 + FF_TILE])
        if has_state:
            gbuf[0:halo, :] = st_ref[:, c0:c0 + FF_TILE]
        else:
            gbuf[0:halo, :] = carry[:, c0:c0 + FF_TILE]
        gbuf[halo:halo + tm, :] = g
        g1 = gbuf[halo - step:halo - step + tm, :]
        g2 = gbuf[halo - 2 * step:halo - 2 * step + tm, :]
        cw = cw_ref[:, c0:c0 + FF_TILE]
        gc = g * cw[2:3] + g1 * cw[1:2] + g2 * cw[0:1] + cb_ref[:, c0:c0 + FF_TILE]
        hbuf[:, c0:c0 + FF_TILE] = (jax.nn.gelu(gc) * u).astype(BF16)
        if has_state:
            so_ref[:, c0:c0 + FF_TILE] = gbuf[tm:tm + halo, :]
        else:
            carry[:, c0:c0 + FF_TILE] = gbuf[tm:tm + halo, :]
            so_ref[0, :, c0:c0 + FF_TILE] = gbuf[halo + tm - 2:halo + tm, :]
    f = _dot(hbuf[...], wo_ref[...])
    y_ref[...] = _deepnorm(x_ref[...], f, g_ref[...], b_ref[...], alpha)


def _ffn_call(x, state_tm, wi, cw, cb, wo, g, b, *, seq_len, alpha):
    T = x.shape[0]
    has_state = state_tm is not None
    if has_state:
        tm, step, halo, tiles_per_seq = T, T // seq_len, state_tm.shape[0], 1
    else:
        tm, step, halo, tiles_per_seq = TM_PROMPT, 1, SUBLANES, seq_len // TM_PROMPT
    n_tiles = T // tm
    n_seq = T // seq_len
    row = pl.BlockSpec((tm, D_MODEL), lambda i: (i, 0))
    in_specs = [row]
    args = [x]
    if has_state:
        in_specs.append(_resident(state_tm.shape))
        args.append(state_tm)
    in_specs += [_resident(wi.shape), _resident(cw.shape), _resident(cb.shape),
                 _resident(wo.shape), _resident(g.shape), _resident(b.shape)]
    args += [wi, cw, cb, wo, g, b]
    scratch = [pltpu.VMEM((halo + tm, FF_TILE), F32), pltpu.VMEM((tm, D_FF), BF16)]
    if has_state:
        so_shape = jax.ShapeDtypeStruct(state_tm.shape, F32)
        so_spec = pl.BlockSpec(state_tm.shape, lambda i: (0, 0))
    else:
        so_shape = jax.ShapeDtypeStruct((n_seq, FFN_CONV - 1, D_FF), F32)
        so_spec = pl.BlockSpec((1, FFN_CONV - 1, D_FF), lambda i: (i // tiles_per_seq, 0, 0))
        scratch.append(pltpu.VMEM((halo, D_FF), F32))
    return pl.pallas_call(
        functools.partial(_ffn_kernel, tm=tm, step=step, tiles_per_seq=tiles_per_seq,
                          has_state=has_state, alpha=alpha),
        grid=(n_tiles,),
        in_specs=in_specs,
        out_specs=[row, so_spec],
        out_shape=[jax.ShapeDtypeStruct((T, D_MODEL), F32), so_shape],
        scratch_shapes=scratch,
        compiler_params=_params(),
        name="ffn_state" if has_state else "ffn_prompt",
    )(*args)


def _rglru_kernel(*refs, tm, step, tiles_per_seq, has_state, alpha):
    if has_state:
        (x_ref, h0_ref, cst_ref, wi_ref, cw_ref, cb_ref, wg_ref, bg_ref, lam_ref, wo_ref,
         g_ref, b_ref, y_ref, ho_ref, co_ref, xbuf, bbuf, gatebuf, ybuf) = refs
    else:
        (x_ref, wi_ref, cw_ref, cb_ref, wg_ref, bg_ref, lam_ref, wo_ref,
         g_ref, b_ref, y_ref, ho_ref, co_ref, xbuf, bbuf, gatebuf, ybuf,
         abuf, hin, ccarry, hcarry) = refs
    halo = xbuf.shape[0] - tm
    groups = tm // SUBLANES
    xb = x_ref[...].astype(BF16)
    if has_state:
        xbuf[0:halo, :] = cst_ref[...]
    else:
        @pl.when(pl.program_id(0) % tiles_per_seq == 0)
        def _():
            ccarry[...] = jnp.zeros_like(ccarry)
            hcarry[...] = jnp.zeros_like(hcarry)
        xbuf[0:halo, :] = ccarry[...]
    gatebuf[...] = jax.nn.gelu(_dot(xb, wi_ref[:, 0:D_MODEL]))
    xbuf[halo:halo + tm, :] = _dot(xb, wi_ref[:, D_MODEL:2 * D_MODEL])
    log_sig_lam = jax.nn.log_sigmoid(lam_ref[...])
    for h in range(HEADS):
        c0 = h * HD
        cw = cw_ref[:, c0:c0 + HD]
        xc = cb_ref[:, c0:c0 + HD] + xbuf[halo:halo + tm, c0:c0 + HD] * cw[RG_CONV - 1:RG_CONV]
        for j in range(RG_CONV - 1):
            back = (RG_CONV - 1 - j) * step
            xc = xc + xbuf[halo - back:halo - back + tm, c0:c0 + HD] * cw[j:j + 1]
        gt = _dot(xc.astype(BF16), wg_ref[h])
        bg = bg_ref[:, c0:c0 + HD]
        r = _sigmoid(gt[:, 0:HD] + bg[0:1])
        ig = _sigmoid(gt[:, HD:2 * HD] + bg[1:2])
        a = jnp.exp(RG_C * r * log_sig_lam[:, c0:c0 + HD])
        bv = jnp.sqrt(jnp.maximum(1.0 - a * a, 0.0)) * ig * xc
        if has_state:
            hh = h0_ref[:, c0:c0 + HD]
            for t in range(tm // step):
                hh = a[t * step:(t + 1) * step] * hh + bv[t * step:(t + 1) * step]
                bbuf[h, t * step:(t + 1) * step, :] = hh
            ho_ref[:, c0:c0 + HD] = hh
        else:
            abuf[h] = a
            bbuf[h] = bv
            pa = abuf[h, pl.ds(0, groups, stride=SUBLANES), :]
            pb = bbuf[h, pl.ds(0, groups, stride=SUBLANES), :]
            for k in range(1, SUBLANES):
                ak = abuf[h, pl.ds(k, groups, stride=SUBLANES), :]
                pb = ak * pb + bbuf[h, pl.ds(k, groups, stride=SUBLANES), :]
                pa = ak * pa
                abuf[h, pl.ds(k, groups, stride=SUBLANES), :] = pa
                bbuf[h, pl.ds(k, groups, stride=SUBLANES), :] = pb
    if has_state:
        co_ref[...] = xbuf[tm:tm + halo, :]
    else:
        ccarry[...] = xbuf[tm:tm + halo, :]
        co_ref[0] = xbuf[halo + tm - (RG_CONV - 1):halo + tm, :]

        def group_step(gi, hc):
            hin[:, pl.ds(gi, 1), :] = hc
            last = gi * SUBLANES + SUBLANES - 1
            return abuf[:, pl.ds(last, 1), :] * hc + bbuf[:, pl.ds(last, 1), :]

        hc = lax.fori_loop(0, groups, group_step, hcarry[:, 0:1, :])
        hcarry[:, 0:1, :] = hc
        for h in range(HEADS):
            ho_ref[0, :, h * HD:(h + 1) * HD] = hc[h]
            hv = hin[h]
            for k in range(SUBLANES):
                bbuf[h, pl.ds(k, groups, stride=SUBLANES), :] = (
                    abuf[h, pl.ds(k, groups, stride=SUBLANES), :] * hv
                    + bbuf[h, pl.ds(k, groups, stride=SUBLANES), :])
    for h in range(HEADS):
        c0 = h * HD
        ybuf[:, c0:c0 + HD] = (bbuf[h] * gatebuf[:, c0:c0 + HD]).astype(BF16)
    f = _dot(ybuf[...], wo_ref[...])
    y_ref[...] = _deepnorm(x_ref[...], f, g_ref[...], b_ref[...], alpha)


def _rglru_call(x, h0, cst_tm, wi, cw, cb, wg, bg, lam, wo, g, b, *, seq_len, alpha):
    T = x.shape[0]
    has_state = h0 is not None
    if has_state:
        tm, step, halo, tiles_per_seq = T, T // seq_len, cst_tm.shape[0], 1
    else:
        tm, step, halo, tiles_per_seq = TM_PROMPT, 1, SUBLANES, seq_len // TM_PROMPT
    n_tiles = T // tm
    n_seq = T // seq_len
    row = pl.BlockSpec((tm, D_MODEL), lambda i: (i, 0))
    in_specs = [row]
    args = [x]
    if has_state:
        in_specs += [_resident(h0.shape), _resident(cst_tm.shape)]
        args += [h0, cst_tm]
    weights = [wi, cw, cb, wg, bg, lam, wo, g, b]
    in_specs += [_resident(w.shape) for w in weights]
    args += weights
    scratch = [pltpu.VMEM((halo + tm, D_MODEL), F32), pltpu.VMEM((HEADS, tm, HD), F32),
               pltpu.VMEM((tm, D_MODEL), F32), pltpu.VMEM((tm, D_MODEL), BF16)]
    if has_state:
        out_shape = [jax.ShapeDtypeStruct((T, D_MODEL), F32),
                     jax.ShapeDtypeStruct(h0.shape, F32),
                     jax.ShapeDtypeStruct(cst_tm.shape, F32)]
        out_specs = [row, pl.BlockSpec(h0.shape, lambda i: (0, 0)),
                     pl.BlockSpec(cst_tm.shape, lambda i: (0, 0))]
    else:
        out_shape = [jax.ShapeDtypeStruct((T, D_MODEL), F32),
                     jax.ShapeDtypeStruct((n_seq, 1, D_MODEL), F32),
                     jax.ShapeDtypeStruct((n_seq, RG_CONV - 1, D_MODEL), F32)]
        out_specs = [row,
                     pl.BlockSpec((1, 1, D_MODEL), lambda i: (i // tiles_per_seq, 0, 0)),
                     pl.BlockSpec((1, RG_CONV - 1, D_MODEL), lambda i: (i // tiles_per_seq, 0, 0))]
        scratch += [pltpu.VMEM((HEADS, tm, HD), F32), pltpu.VMEM((HEADS, tm // SUBLANES, HD), F32),
                    pltpu.VMEM((halo, D_MODEL), F32), pltpu.VMEM((HEADS, SUBLANES, HD), F32)]
    return pl.pallas_call(
        functools.partial(_rglru_kernel, tm=tm, step=step, tiles_per_seq=tiles_per_seq,
                          has_state=has_state, alpha=alpha),
        grid=(n_tiles,),
        in_specs=in_specs,
        out_specs=out_specs,
        out_shape=out_shape,
        scratch_shapes=scratch,
        compiler_params=_params(),
        name="rglru_state" if has_state else "rglru_prompt",
    )(*args)


def _split3(x):
    hi = x.astype(BF16)
    r = x - hi.astype(F32)
    mid = r.astype(BF16)
    lo = (r - mid.astype(F32)).astype(BF16)
    return hi, mid, lo


def _lower_bound(lower_ref, h, layer):
    rows = [lower_ref[n, h] for n in range(lower_ref.shape[0])]
    m = functools.reduce(jnp.maximum, rows)
    es = [jnp.exp(r - m) for r in rows]
    tot = functools.reduce(lambda p, q: p + q, es)
    sm = [e / tot for e in es]
    cs = functools.reduce(lambda p, q: p + q, sm[:layer + 1])
    return jnp.maximum(cs - sm[0], 0.0)


def _chunk_cumsum(cb_ref, tb_ref, eb_ref, tm, chunk):
    groups = tm // SUBLANES
    p = cb_ref[pl.ds(0, groups, stride=SUBLANES), :]
    for k in range(1, SUBLANES):
        p = p + cb_ref[pl.ds(k, groups, stride=SUBLANES), :]
        cb_ref[pl.ds(k, groups, stride=SUBLANES), :] = p
    if chunk > SUBLANES:
        gc = chunk // SUBLANES
        nch = tm // chunk
        tb_ref[...] = p
        e = jnp.zeros((nch, HD), F32)
        eb_ref[pl.ds(0, nch, stride=gc), :] = e
        for j in range(1, gc):
            e = e + tb_ref[pl.ds(j - 1, nch, stride=gc), :]
            eb_ref[pl.ds(j, nch, stride=gc), :] = e
        off = eb_ref[...]
        for k in range(SUBLANES):
            cb_ref[pl.ds(k, groups, stride=SUBLANES), :] = (
                cb_ref[pl.ds(k, groups, stride=SUBLANES), :] + off)
    return p


def _hgrn_front(p_ref, lower_ref, h, layer, tm, chunk,
                cb_ref, tb_ref, eb_ref, kk_ref, v_ref, sg_ref, qs_ref, qe_ref):
    lb = _lower_bound(lower_ref, h, layer)
    fg = lb + (1.0 - lb) * _sigmoid(p_ref[:, HD:2 * HD])
    cb_ref[...] = jnp.log(jnp.maximum(fg, F_FLOOR))
    kk_ref[...] = 1.0 - fg
    q = p_ref[:, 0:HD]
    qs = q * _sigmoid(q) * (HD ** -0.5)
    v_ref[...] = p_ref[:, 2 * HD:3 * HD]
    gg = p_ref[:, 3 * HD:4 * HD]
    sg_ref[...] = gg * _sigmoid(gg)
    totals = _chunk_cumsum(cb_ref, tb_ref, eb_ref, tm, chunk)
    cum = cb_ref[...]
    qs_ref[...] = qs
    qe_ref[...] = qs * jnp.exp(cum)
    safe = jnp.max(-cum) <= SAFE_DECAY
    return safe, totals


def _same_chunk_mask(chunk):
    ri = lax.broadcasted_iota(jnp.int32, (BLK, BLK), 0)
    ci = lax.broadcasted_iota(jnp.int32, (BLK, BLK), 1)
    lc = chunk.bit_length() - 1
    return ((ri >> lc) == (ci >> lc)) & (ci <= ri)


def _inverse_decayed_keys(kk, cum):
    return kk * jnp.exp(jnp.minimum(-cum, SAFE_DECAY))


def _robust_same_chunk_scores(rows, chunk, cb_ref, kk_ref, qs_ref):
    ri = lax.broadcasted_iota(jnp.int32, (BLK, BLK), 0)
    ci = lax.broadcasted_iota(jnp.int32, (BLK, BLK), 1)
    cum = cb_ref[rows, :]
    qs = qs_ref[rows, :]
    kk = kk_ref[rows, :]
    hi, mid, lo = _split3(cum)
    acc = jnp.where(ri == ci, _dot_nt(qs.astype(BF16), kk.astype(BF16)), 0.0)
    for lvl in range(chunk.bit_length() - 1):
        m = 1 << lvl
        pivot = ((ri >> (lvl + 1)) << (lvl + 1)) + (m - 1)
        sel = jnp.where(ci == pivot, 1.0, 0.0).astype(BF16)
        ref_cum = _dot(sel, hi) + _dot(sel, mid) + _dot(sel, lo)
        e = jnp.exp(-jnp.abs(cum - ref_cum))
        keep = (((ri >> (lvl + 1)) == (ci >> (lvl + 1)))
                & ((ri & (2 * m - 1)) >= m) & ((ci & (2 * m - 1)) < m))
        s = _dot_nt((qs * e).astype(BF16), (kk * e).astype(BF16))
        acc = acc + jnp.where(keep, s, 0.0)
    return acc


def _rms_gate(o, ng, sg):
    return o * lax.rsqrt(jnp.mean(o * o, axis=-1, keepdims=True) + RMS_EPS) * ng * sg


def _hgrn_prompt_kernel(x_ref, lower_ref, wi_ref, ng_ref, wo_ref, g_ref, b_ref, y_ref, so_ref,
                        xb_ref, p_ref, cb_ref, tb_ref, eb_ref, kk_ref, v_ref, sg_ref, qs_ref,
                        qe_ref, kd_ref, a_ref, oh_ref, ds_ref, ec_ref, ob_ref, s_ref,
                        *, tm, layer, tiles_per_seq, alpha):
    chunk = PROMPT_CHUNK
    assert BLK == 2 * chunk
    n_blocks = tm // BLK
    i = pl.program_id(0)

    @pl.when(i % tiles_per_seq == 0)
    def _():
        s_ref[...] = jnp.zeros_like(s_ref)

    xb_ref[...] = x_ref[...].astype(BF16)
    second = lax.broadcasted_iota(jnp.int32, (BLK, HD), 0) >= chunk
    ri = lax.broadcasted_iota(jnp.int32, (BLK, BLK), 0)
    ci = lax.broadcasted_iota(jnp.int32, (BLK, BLK), 1)
    cross = (ri >= chunk) & (ci < chunk)
    same = _same_chunk_mask(chunk)

    def project(h, slot):
        p_ref[slot] = _dot(xb_ref[...], wi_ref[h])

    def views(slot):
        return tuple(r.at[slot] for r in (cb_ref, tb_ref, eb_ref, kk_ref, v_ref, sg_ref, qs_ref,
                                          qe_ref, kd_ref, a_ref, oh_ref, ds_ref, ec_ref))

    def scores(h, slot):
        cb, tb, eb, kk_s, v_s, sg_s, qs_s, qe_s, kd_s, a_s, _, ds_s, ec_s = views(slot)
        safe, _ = _hgrn_front(p_ref.at[slot], lower_ref, h, layer, tm, chunk,
                              cb, tb, eb, kk_s, v_s, sg_s, qs_s, qe_s)
        for nb in range(n_blocks):
            rows = pl.ds(nb * BLK, BLK)
            cum = cb[rows, :]
            kk = kk_s[rows, :]
            last0 = cb[pl.ds(nb * BLK + chunk - 1, 1), :]
            last1 = cb[pl.ds(nb * BLK + BLK - 1, 1), :]
            kd = kk * jnp.exp(jnp.where(second, last1, last0) - cum)
            kd_s[rows, :] = kd
            keys = jnp.concatenate([_inverse_decayed_keys(kk, cum), kd], axis=0).astype(BF16)
            s2 = _dot_nt(qe_s[rows, :].astype(BF16), keys)
            a = jnp.where(same, s2[:, 0:BLK], 0.0) + jnp.where(cross, s2[:, BLK:2 * BLK], 0.0)
            a_s[nb] = a.astype(BF16)
            k_blk = kd * jnp.where(second, 1.0, jnp.exp(last1))
            ds_s[nb] = _dot(k_blk.T.astype(BF16), v_s[rows, :].astype(BF16))
            ec_s[nb] = jnp.broadcast_to(jnp.exp(last0 + last1), (HD, HD)).T
        return safe

    def fix_scores(safe, slot):
        cb, _, _, kk_s, _, _, qs_s, qe_s, kd_s, a_s, _, _, _ = views(slot)

        @pl.when(jnp.logical_not(safe))
        def _():
            def fix(nb, _):
                rows = pl.ds(pl.multiple_of(nb * BLK, BLK), BLK)
                s1 = _dot_nt(qe_s[rows, :].astype(BF16), kd_s[rows, :].astype(BF16))
                a = _robust_same_chunk_scores(rows, chunk, cb, kk_s, qs_s)
                a_s[nb] = (a + jnp.where(cross, s1, 0.0)).astype(BF16)
                return 0
            lax.fori_loop(0, n_blocks, fix, 0)

    def outputs(h, slot):
        cb, _, _, _, v_s, sg_s, _, qe_s, _, a_s, oh_s, ds_s, ec_s = views(slot)
        states = [s_ref[h]]
        for nb in range(n_blocks):
            states.append(ec_s[nb] * states[nb] + ds_s[nb])
        s_ref[h] = states[n_blocks]
        for nb in range(n_blocks):
            rows = pl.ds(nb * BLK, BLK)
            last0 = cb[pl.ds(nb * BLK + chunk - 1, 1), :]
            q_blk = qe_s[rows, :] * jnp.where(second, jnp.exp(last0), 1.0)
            lhs = jnp.concatenate([a_s[nb], q_blk.astype(BF16)], axis=1)
            rhs = jnp.concatenate([v_s[rows, :].astype(BF16), states[nb].astype(BF16)], axis=0)
            oh_s[rows, :] = _dot(lhs, rhs)
        ob_ref[h] = _rms_gate(oh_s[...], ng_ref[...], sg_s[...]).astype(BF16)

    project(0, 0)

    def pair(hh, _):
        h0 = 2 * hh
        project(h0 + 1, 1)
        safe0 = scores(h0, 0)
        project(jnp.minimum(h0 + 2, HEADS - 1), 0)
        safe1 = scores(h0 + 1, 1)
        fix_scores(safe0, 0)
        fix_scores(safe1, 1)
        outputs(h0, 0)
        outputs(h0 + 1, 1)
        return 0

    lax.fori_loop(0, HEADS // 2, pair, 0)
    for h in range(HEADS):
        xb_ref[:, h * HD:(h + 1) * HD] = ob_ref[h]
    f = _dot(xb_ref[...], wo_ref[...])
    y_ref[...] = _deepnorm(x_ref[...], f, g_ref[...], b_ref[...], alpha)

    @pl.when(i % tiles_per_seq == tiles_per_seq - 1)
    def _():
        so_ref[0] = s_ref[...]


def _hgrn_prep(hg_lower, wi, ng, wo):
    n_layers = hg_lower.shape[0]
    lower = hg_lower.reshape(n_layers, HEADS, 1, HD)
    wih = wi.reshape(D_MODEL, 4, HEADS, HD).transpose(2, 0, 1, 3).reshape(HEADS, D_MODEL, 4 * HD)
    return lower, wih.astype(BF16), ng.reshape(1, HD), wo.astype(BF16)


def _hgrn_prompt_call(x, prep, layer, g, b, *, seq_len, alpha):
    lower, wih, ng, wo = prep
    T = x.shape[0]
    tm = TM_PROMPT
    tiles_per_seq = seq_len // tm
    n_seq = T // seq_len
    row = pl.BlockSpec((tm, D_MODEL), lambda i: (i, 0))
    weights = [lower, wih, ng, wo, g, b]
    head_buf = pltpu.VMEM((2, tm, HD), F32)
    scratch = [pltpu.VMEM((tm, D_MODEL), BF16),
               pltpu.VMEM((2, tm, 4 * HD), F32),
               head_buf,
               pltpu.VMEM((2, tm // SUBLANES, HD), F32),
               pltpu.VMEM((2, tm // SUBLANES, HD), F32),
               head_buf, head_buf, head_buf, head_buf, head_buf,
               head_buf,
               pltpu.VMEM((2, tm // BLK, BLK, BLK), BF16),
               head_buf,
               pltpu.VMEM((2, tm // BLK, HD, HD), F32),
               pltpu.VMEM((2, tm // BLK, HD, HD), F32),
               pltpu.VMEM((HEADS, tm, HD), BF16),
               pltpu.VMEM((HEADS, HD, HD), F32)]
    y, so = pl.pallas_call(
        functools.partial(_hgrn_prompt_kernel, tm=tm, layer=layer,
                          tiles_per_seq=tiles_per_seq, alpha=alpha),
        grid=(T // tm,),
        in_specs=[row] + [_resident(w.shape) for w in weights],
        out_specs=[row, pl.BlockSpec((1, HEADS, HD, HD), lambda i: (i // tiles_per_seq, 0, 0, 0))],
        out_shape=[jax.ShapeDtypeStruct((T, D_MODEL), F32),
                   jax.ShapeDtypeStruct((n_seq, HEADS, HD, HD), F32)],
        scratch_shapes=scratch,
        compiler_params=_params(),
        name="hgrn_prompt",
    )(x, *weights)
    return y, so


def _hgrn_sample_front_kernel(x_ref, lower_ref, wi_ref,
                              oi_ref, qe_ref, v_ref, sg_ref, kdt_ref, et_ref,
                              xb_ref, p_ref, cb_ref, kk_ref, qs_ref, a_ref, last_ref,
                              *, tm, chunk, layer):
    assert chunk == SUBLANES
    xb_ref[...] = x_ref[...].astype(BF16)
    groups = tm // SUBLANES
    n_blocks = tm // BLK
    same = _same_chunk_mask(chunk)

    def head(h, _):
        p_ref[...] = _dot(xb_ref[...], wi_ref[h])
        safe, totals = _hgrn_front(p_ref, lower_ref, h, layer, tm, chunk,
                                   cb_ref, None, None, kk_ref, v_ref.at[h], sg_ref.at[h],
                                   qs_ref, qe_ref.at[h])
        for k in range(SUBLANES):
            last_ref[pl.ds(k, groups, stride=SUBLANES), :] = totals
        for nb in range(n_blocks):
            r0 = nb * BLK
            rows = pl.ds(r0, BLK)
            cum = cb_ref[rows, :]
            kk = kk_ref[rows, :]
            s = _dot_nt(qe_ref[h, rows, :].astype(BF16), _inverse_decayed_keys(kk, cum).astype(BF16))
            a_ref[nb] = jnp.where(same, s, 0.0).astype(BF16)
            last = last_ref[rows, :]
            kdt_ref[h, :, r0:r0 + BLK] = (kk * jnp.exp(last - cum)).T
            et_ref[h, :, r0:r0 + BLK] = jnp.exp(last).T

        @pl.when(jnp.logical_not(safe))
        def _():
            def fix(nb, _):
                rows = pl.ds(pl.multiple_of(nb * BLK, BLK), BLK)
                a_ref[nb] = _robust_same_chunk_scores(rows, chunk, cb_ref, kk_ref, qs_ref).astype(BF16)
                return 0
            lax.fori_loop(0, n_blocks, fix, 0)

        for nb in range(n_blocks):
            rows = pl.ds(nb * BLK, BLK)
            oi_ref[h, rows, :] = _dot(a_ref[nb], v_ref[h, rows, :].astype(BF16))
        return 0

    lax.fori_loop(0, HEADS, head, 0)


def _hgrn_sample_state_kernel(qe_ref, oi_ref, v_ref, sg_ref, kdt_ref, et_ref, s0_ref, ng_ref,
                              op_ref, so_ref, *, chunk):
    row_seq = lax.broadcasted_iota(jnp.int32, (BLK, HD), 0) >> (chunk.bit_length() - 1)

    def head(h, _):
        kdt = kdt_ref[h].astype(BF16)
        et = et_ref[h]
        v = v_ref[h]
        for sq in range(BLK // chunk):
            rows = slice(sq * chunk, (sq + 1) * chunk)
            s0 = s0_ref[sq, h]
            o = oi_ref[h, rows, :] + _dot(qe_ref[h, rows, :].astype(BF16), s0.astype(BF16))
            op_ref[h, rows, :] = _rms_gate(o, ng_ref[...], sg_ref[h, rows, :])
            vm = jnp.where(row_seq == sq, v, 0.0).astype(BF16)
            e_col = jnp.broadcast_to(et[:, sq * chunk:sq * chunk + 1], (HD, HD))
            so_ref[sq, h] = e_col * s0 + _dot(kdt, vm)
        return 0

    lax.fori_loop(0, HEADS, head, 0)


def _hgrn_sample_out_kernel(x_ref, op_ref, wo_ref, g_ref, b_ref, y_ref, ob_ref, *, alpha):
    for h in range(HEADS):
        ob_ref[:, h * HD:(h + 1) * HD] = op_ref[h].astype(BF16)
    f = _dot(ob_ref[...], wo_ref[...])
    y_ref[...] = _deepnorm(x_ref[...], f, g_ref[...], b_ref[...], alpha)


def _hgrn_sample_call(x, s0, prep, layer, g, b, *, seq_len, alpha):
    lower, wih, ng, wo = prep
    T = x.shape[0]
    chunk = seq_len
    heads_rows = jax.ShapeDtypeStruct((HEADS, T, HD), F32)
    heads_cols = jax.ShapeDtypeStruct((HEADS, HD, T), F32)
    full = lambda s: pl.BlockSpec(s, lambda i: (0,) * len(s))
    head_buf = pltpu.VMEM((T, HD), F32)
    oi, qe, v, sg, kdt, et = pl.pallas_call(
        functools.partial(_hgrn_sample_front_kernel, tm=T, chunk=chunk, layer=layer),
        grid=(1,),
        in_specs=[full(x.shape), full(lower.shape), full(wih.shape)],
        out_specs=[full(heads_rows.shape)] * 4 + [full(heads_cols.shape)] * 2,
        out_shape=[heads_rows] * 4 + [heads_cols] * 2,
        scratch_shapes=[pltpu.VMEM((T, D_MODEL), BF16), pltpu.VMEM((T, 4 * HD), F32),
                        head_buf, head_buf, head_buf,
                        pltpu.VMEM((T // BLK, BLK, BLK), BF16), head_buf],
        compiler_params=_params(),
        name="hgrn_sample_front",
    )(x, lower, wih)
    seqs = BLK // chunk
    by_rows = pl.BlockSpec((HEADS, BLK, HD), lambda i: (0, i, 0))
    by_cols = pl.BlockSpec((HEADS, HD, BLK), lambda i: (0, 0, i))
    state = pl.BlockSpec((seqs, HEADS, HD, HD), lambda i: (i, 0, 0, 0))
    state_in = pl.BlockSpec((None, seqs, HEADS, HD, HD), lambda i: (layer, i, 0, 0, 0))
    op, so = pl.pallas_call(
        functools.partial(_hgrn_sample_state_kernel, chunk=chunk),
        grid=(T // BLK,),
        in_specs=[by_rows, by_rows, by_rows, by_rows, by_cols, by_cols, state_in, _resident(ng.shape)],
        out_specs=[by_rows, state],
        out_shape=[heads_rows, jax.ShapeDtypeStruct(s0.shape[1:], F32)],
        compiler_params=_params(),
        name="hgrn_sample_state",
    )(qe, oi, v, sg, kdt, et, s0, ng)
    y = pl.pallas_call(
        functools.partial(_hgrn_sample_out_kernel, alpha=alpha),
        grid=(1,),
        in_specs=[full(x.shape), full(op.shape), full(wo.shape), full(g.shape), full(b.shape)],
        out_specs=full(x.shape),
        out_shape=jax.ShapeDtypeStruct(x.shape, F32),
        scratch_shapes=[pltpu.VMEM((T, D_MODEL), BF16)],
        compiler_params=_params(),
        name="hgrn_sample_out",
    )(x, op, wo, g, b)
    return y, so


def _to_time_major(a):
    return jnp.transpose(a, (1, 0, 2)).reshape(-1, a.shape[-1])


def _from_time_major(a, batch):
    return jnp.transpose(a.reshape(-1, batch, a.shape[-1]), (1, 0, 2))


def kernel(x_prompt, x_sample, state_rglru_h, state_rglru_conv, state_hgrn_s, state_ffn_conv,
           ln_g, ln_b, rg_w_in, rg_conv_w, rg_conv_b, rg_gate_w, rg_gate_b, rg_lambda, rg_w_out,
           hg_lower, hg_w_in, hg_norm_g, hg_w_out, ffn_w_in, ffn_conv_w, ffn_conv_b, ffn_w_out):
    depth = ln_g.shape[0]
    alpha = (2.0 * depth) ** 0.25
    pb, pl_len, _ = x_prompt.shape
    sb, sl_len, _ = x_sample.shape

    xp = x_prompt.reshape(pb * pl_len, D_MODEL)
    xs = _to_time_major(x_sample)
    p_h, p_rc, p_s, p_fc = [], [], [], []
    s_h, s_rc, s_s, s_fc = [], [], [], []
    for i in range(depth):
        j = i // 2
        g0, b0 = ln_g[i, 0][None], ln_b[i, 0][None]
        if i % 2 == 0:
            wgc = jnp.concatenate([rg_gate_w[j, 0], rg_gate_w[j, 1]], axis=-1).astype(BF16)
            w = (rg_w_in[j].astype(BF16), rg_conv_w[j], rg_conv_b[j][None], wgc, rg_gate_b[j],
                 rg_lambda[j][None], rg_w_out[j].astype(BF16), g0, b0)
            xp, ho, co = _rglru_call(xp, None, None, *w, seq_len=pl_len, alpha=alpha)
            p_h.append(ho[:, 0])
            p_rc.append(co)
            xs, ho, co = _rglru_call(xs, state_rglru_h[j], _to_time_major(state_rglru_conv[j]), *w,
                                     seq_len=sl_len, alpha=alpha)
            s_h.append(ho)
            s_rc.append(_from_time_major(co, sb))
        else:
            prep = _hgrn_prep(hg_lower, hg_w_in[j], hg_norm_g[j], hg_w_out[j])
            xp, so = _hgrn_prompt_call(xp, prep, j, g0, b0, seq_len=pl_len, alpha=alpha)
            p_s.append(so)
            xs_bm = _from_time_major(xs, sb).reshape(sb * sl_len, D_MODEL)
            ys_bm, so = _hgrn_sample_call(xs_bm, state_hgrn_s, prep, j, g0, b0,
                                          seq_len=sl_len, alpha=alpha)
            s_s.append(so)
            xs = _to_time_major(ys_bm.reshape(sb, sl_len, D_MODEL))
        w = (ffn_w_in[i].astype(BF16), ffn_conv_w[i], ffn_conv_b[i][None], ffn_w_out[i].astype(BF16),
             ln_g[i, 1][None], ln_b[i, 1][None])
        xp, fo = _ffn_call(xp, None, *w, seq_len=pl_len, alpha=alpha)
        p_fc.append(fo)
        xs, fo = _ffn_call(xs, _to_time_major(state_ffn_conv[i]), *w, seq_len=sl_len, alpha=alpha)
        s_fc.append(_from_time_major(fo, sb))
    return (xp.reshape(x_prompt.shape), _from_time_major(xs, sb),
            jnp.stack(p_h), jnp.stack(p_rc), jnp.stack(p_s), jnp.stack(p_fc),
            jnp.stack(s_h), jnp.stack(s_rc), jnp.stack(s_s), jnp.stack(s_fc))
```

```python
import functools

import jax
import jax.numpy as jnp
from jax import lax
from jax.experimental import pallas as pl
from jax.experimental.pallas import tpu as pltpu

F32 = jnp.float32
BF16 = jnp.bfloat16

D_MODEL = 1024
D_FF = 2816
HEADS = 8
HD = 128
RG_CONV = 4
FFN_CONV = 3
RG_C = 8.0
LN_EPS = 1e-5
RMS_EPS = 1e-6
F_FLOOR = 1e-30
SUBLANES = 8
LANES = 128
BLK = 128
PROMPT_CHUNK = 64
SAFE_DECAY = 80.0
FF_TILE = 256
TM_PROMPT = 512
VMEM_LIMIT = 56 * 1024 * 1024


def _dot(a, b):
    return jnp.dot(a, b, preferred_element_type=F32)


def _dot_nt(a, b):
    return lax.dot_general(a, b, (((1,), (1,)), ((), ())), preferred_element_type=F32)


def _sigmoid(x):
    return 0.5 * jnp.tanh(0.5 * x) + 0.5


def _resident(shape):
    nd = len(shape)
    return pl.BlockSpec(shape, lambda i: (0,) * nd, pipeline_mode=pl.Buffered(1))


def _layer_shape(w):
    return w[0].shape[1:] if isinstance(w, tuple) else w.shape


def _layer_operand(w):
    if not isinstance(w, tuple):
        return _resident(w.shape), w
    arr, layer = w
    tail = (0,) * (arr.ndim - 1)
    spec = pl.BlockSpec((None,) + arr.shape[1:], lambda i: (layer,) + tail,
                        pipeline_mode=pl.Buffered(1))
    return spec, arr


def _deepnorm(x, f, g, b, alpha):
    z = alpha * x + f
    mu = jnp.mean(z, axis=-1, keepdims=True)
    zc = z - mu
    var = jnp.mean(zc * zc, axis=-1, keepdims=True)
    return zc * lax.rsqrt(var + LN_EPS) * g + b


def _params(n_axes=1):
    return pltpu.CompilerParams(dimension_semantics=("arbitrary",) * n_axes,
                                vmem_limit_bytes=VMEM_LIMIT)


def _ffn_kernel(*refs, tm, step, tiles_per_seq, has_state, alpha):
    if has_state:
        (x_ref, st_ref, wi_ref, cw_ref, cb_ref, wo_ref, g_ref, b_ref,
         y_ref, so_ref, gbuf, hbuf) = refs
        carry = None
    else:
        (x_ref, wi_ref, cw_ref, cb_ref, wo_ref, g_ref, b_ref,
         y_ref, so_ref, gbuf, hbuf, carry) = refs
    halo = gbuf.shape[0] - tm
    xb = x_ref[...].astype(BF16)
    if not has_state:
        @pl.when(pl.program_id(0) % tiles_per_seq == 0)
        def _():
            carry[...] = jnp.zeros_like(carry)
    for j in range(D_FF // FF_TILE):
        c0 = j * FF_TILE
        g = _dot(xb, wi_ref[:, c0:c0 + FF_TILE])
        u = _dot(xb, wi_ref[:, D_FF + c0:D_FF + c0 + FF_TILE])
        if has_state:
            gbuf[0:halo, :] = st_ref[:, c0:c0 + FF_TILE]
        else:
            gbuf[0:halo, :] = carry[:, c0:c0 + FF_TILE]
        gbuf[halo:halo + tm, :] = g
        g1 = gbuf[halo - step:halo - step + tm, :]
        g2 = gbuf[halo - 2 * step:halo - 2 * step + tm, :]
        cw = cw_ref[:, c0:c0 + FF_TILE]
        gc = g * cw[2:3] + g1 * cw[1:2] + g2 * cw[0:1] + cb_ref[:, c0:c0 + FF_TILE]
        hbuf[:, c0:c0 + FF_TILE] = (jax.nn.gelu(gc) * u).astype(BF16)
        if has_state:
            so_ref[:, c0:c0 + FF_TILE] = gbuf[tm:tm + halo, :]
        else:
            carry[:, c0:c0 + FF_TILE] = gbuf[tm:tm + halo, :]
            so_ref[0, :, c0:c0 + FF_TILE] = gbuf[halo + tm - 2:halo + tm, :]
    f = _dot(hbuf[...], wo_ref[...])
    y_ref[...] = _deepnorm(x_ref[...], f, g_ref[...], b_ref[...], alpha)


def _ffn_call(x, state_tm, wi, cw, cb, wo, g, b, *, seq_len, alpha):
    T = x.shape[0]
    has_state = state_tm is not None
    if has_state:
        st_shape = _layer_shape(state_tm)
        tm, step, halo, tiles_per_seq = T, T // seq_len, st_shape[0], 1
    else:
        tm, step, halo, tiles_per_seq = TM_PROMPT, 1, SUBLANES, seq_len // TM_PROMPT
    n_tiles = T // tm
    n_seq = T // seq_len
    row = pl.BlockSpec((tm, D_MODEL), lambda i: (i, 0))
    operands = ([state_tm] if has_state else []) + [wi, cw, cb, wo, g, b]
    specs_args = [_layer_operand(w) for w in operands]
    in_specs = [row] + [s for s, _ in specs_args]
    args = [x] + [a for _, a in specs_args]
    scratch = [pltpu.VMEM((halo + tm, FF_TILE), F32), pltpu.VMEM((tm, D_FF), BF16)]
    if has_state:
        so_shape = jax.ShapeDtypeStruct(st_shape, F32)
        so_spec = pl.BlockSpec(st_shape, lambda i: (0, 0))
    else:
        so_shape = jax.ShapeDtypeStruct((n_seq, FFN_CONV - 1, D_FF), F32)
        so_spec = pl.BlockSpec((1, FFN_CONV - 1, D_FF), lambda i: (i // tiles_per_seq, 0, 0))
        scratch.append(pltpu.VMEM((halo, D_FF), F32))
    return pl.pallas_call(
        functools.partial(_ffn_kernel, tm=tm, step=step, tiles_per_seq=tiles_per_seq,
                          has_state=has_state, alpha=alpha),
        grid=(n_tiles,),
        in_specs=in_specs,
        out_specs=[row, so_spec],
        out_shape=[jax.ShapeDtypeStruct((T, D_MODEL), F32), so_shape],
        scratch_shapes=scratch,
        compiler_params=_params(),
        name="ffn_state" if has_state else "ffn_prompt",
    )(*args)


def _rglru_kernel(*refs, tm, step, tiles_per_seq, has_state, alpha):
    if has_state:
        (x_ref, h0_ref, cst_ref, wi_ref, cw_ref, cb_ref, wg_ref, bg_ref, lam_ref, wo_ref,
         g_ref, b_ref, y_ref, ho_ref, co_ref, xbuf, bbuf, gatebuf, ybuf) = refs
    else:
        (x_ref, wi_ref, cw_ref, cb_ref, wg_ref, bg_ref, lam_ref, wo_ref,
         g_ref, b_ref, y_ref, ho_ref, co_ref, xbuf, bbuf, gatebuf, ybuf,
         abuf, hin, ccarry, hcarry) = refs
    halo = xbuf.shape[0] - tm
    groups = tm // SUBLANES
    xb = x_ref[...].astype(BF16)
    if has_state:
        xbuf[0:halo, :] = cst_ref[...]
    else:
        @pl.when(pl.program_id(0) % tiles_per_seq == 0)
        def _():
            ccarry[...] = jnp.zeros_like(ccarry)
            hcarry[...] = jnp.zeros_like(hcarry)
        xbuf[0:halo, :] = ccarry[...]
    gatebuf[...] = jax.nn.gelu(_dot(xb, wi_ref[:, 0:D_MODEL]))
    xbuf[halo:halo + tm, :] = _dot(xb, wi_ref[:, D_MODEL:2 * D_MODEL])
    log_sig_lam = jax.nn.log_sigmoid(lam_ref[...])
    for h in range(HEADS):
        c0 = h * HD
        cw = cw_ref[:, c0:c0 + HD]
        xc = cb_ref[:, c0:c0 + HD] + xbuf[halo:halo + tm, c0:c0 + HD] * cw[RG_CONV - 1:RG_CONV]
        for j in range(RG_CONV - 1):
            back = (RG_CONV - 1 - j) * step
            xc = xc + xbuf[halo - back:halo - back + tm, c0:c0 + HD] * cw[j:j + 1]
        gt = _dot(xc.astype(BF16), wg_ref[h])
        bg = bg_ref[:, c0:c0 + HD]
        r = _sigmoid(gt[:, 0:HD] + bg[0:1])
        ig = _sigmoid(gt[:, HD:2 * HD] + bg[1:2])
        a = jnp.exp(RG_C * r * log_sig_lam[:, c0:c0 + HD])
        bv = jnp.sqrt(jnp.maximum(1.0 - a * a, 0.0)) * ig * xc
        if has_state:
            hh = h0_ref[:, c0:c0 + HD]
            for t in range(tm // step):
                hh = a[t * step:(t + 1) * step] * hh + bv[t * step:(t + 1) * step]
                bbuf[h, t * step:(t + 1) * step, :] = hh
            ho_ref[:, c0:c0 + HD] = hh
        else:
            abuf[h] = a
            bbuf[h] = bv
            pa = abuf[h, pl.ds(0, groups, stride=SUBLANES), :]
            pb = bbuf[h, pl.ds(0, groups, stride=SUBLANES), :]
            for k in range(1, SUBLANES):
                ak = abuf[h, pl.ds(k, groups, stride=SUBLANES), :]
                pb = ak * pb + bbuf[h, pl.ds(k, groups, stride=SUBLANES), :]
                pa = ak * pa
                abuf[h, pl.ds(k, groups, stride=SUBLANES), :] = pa
                bbuf[h, pl.ds(k, groups, stride=SUBLANES), :] = pb
    if has_state:
        co_ref[...] = xbuf[tm:tm + halo, :]
    else:
        ccarry[...] = xbuf[tm:tm + halo, :]
        co_ref[0] = xbuf[halo + tm - (RG_CONV - 1):halo + tm, :]

        def group_step(gi, hc):
            hin[:, pl.ds(gi, 1), :] = hc
            last = gi * SUBLANES + SUBLANES - 1
            return abuf[:, pl.ds(last, 1), :] * hc + bbuf[:, pl.ds(last, 1), :]

        hc = lax.fori_loop(0, groups, group_step, hcarry[:, 0:1, :])
        hcarry[:, 0:1, :] = hc
        for h in range(HEADS):
            ho_ref[0, :, h * HD:(h + 1) * HD] = hc[h]
            hv = hin[h]
            for k in range(SUBLANES):
                bbuf[h, pl.ds(k, groups, stride=SUBLANES), :] = (
                    abuf[h, pl.ds(k, groups, stride=SUBLANES), :] * hv
                    + bbuf[h, pl.ds(k, groups, stride=SUBLANES), :])
    for h in range(HEADS):
        c0 = h * HD
        ybuf[:, c0:c0 + HD] = (bbuf[h] * gatebuf[:, c0:c0 + HD]).astype(BF16)
    f = _dot(ybuf[...], wo_ref[...])
    y_ref[...] = _deepnorm(x_ref[...], f, g_ref[...], b_ref[...], alpha)


def _rglru_call(x, h0, cst_tm, wi, cw, cb, wg, bg, lam, wo, g, b, *, seq_len, alpha):
    T = x.shape[0]
    has_state = h0 is not None
    if has_state:
        h0_shape, cst_shape = _layer_shape(h0), _layer_shape(cst_tm)
        tm, step, halo, tiles_per_seq = T, T // seq_len, cst_shape[0], 1
    else:
        tm, step, halo, tiles_per_seq = TM_PROMPT, 1, SUBLANES, seq_len // TM_PROMPT
    n_tiles = T // tm
    n_seq = T // seq_len
    row = pl.BlockSpec((tm, D_MODEL), lambda i: (i, 0))
    operands = ([h0, cst_tm] if has_state else []) + [wi, cw, cb, wg, bg, lam, wo, g, b]
    specs_args = [_layer_operand(w) for w in operands]
    in_specs = [row] + [s for s, _ in specs_args]
    args = [x] + [a for _, a in specs_args]
    scratch = [pltpu.VMEM((halo + tm, D_MODEL), F32), pltpu.VMEM((HEADS, tm, HD), F32),
               pltpu.VMEM((tm, D_MODEL), F32), pltpu.VMEM((tm, D_MODEL), BF16)]
    if has_state:
        out_shape = [jax.ShapeDtypeStruct((T, D_MODEL), F32),
                     jax.ShapeDtypeStruct(h0_shape, F32),
                     jax.ShapeDtypeStruct(cst_shape, F32)]
        out_specs = [row, pl.BlockSpec(h0_shape, lambda i: (0, 0)),
                     pl.BlockSpec(cst_shape, lambda i: (0, 0))]
    else:
        out_shape = [jax.ShapeDtypeStruct((T, D_MODEL), F32),
                     jax.ShapeDtypeStruct((n_seq, 1, D_MODEL), F32),
                     jax.ShapeDtypeStruct((n_seq, RG_CONV - 1, D_MODEL), F32)]
        out_specs = [row,
                     pl.BlockSpec((1, 1, D_MODEL), lambda i: (i // tiles_per_seq, 0, 0)),
                     pl.BlockSpec((1, RG_CONV - 1, D_MODEL), lambda i: (i // tiles_per_seq, 0, 0))]
        scratch += [pltpu.VMEM((HEADS, tm, HD), F32), pltpu.VMEM((HEADS, tm // SUBLANES, HD), F32),
                    pltpu.VMEM((halo, D_MODEL), F32), pltpu.VMEM((HEADS, SUBLANES, HD), F32)]
    return pl.pallas_call(
        functools.partial(_rglru_kernel, tm=tm, step=step, tiles_per_seq=tiles_per_seq,
                          has_state=has_state, alpha=alpha),
        grid=(n_tiles,),
        in_specs=in_specs,
        out_specs=out_specs,
        out_shape=out_shape,
        scratch_shapes=scratch,
        compiler_params=_params(),
        name="rglru_state" if has_state else "rglru_prompt",
    )(*args)


def _split3(x):
    hi = x.astype(BF16)
    r = x - hi.astype(F32)
    mid = r.astype(BF16)
    lo = (r - mid.astype(F32)).astype(BF16)
    return hi, mid, lo


def _lower_bound(lower_ref, h, layer):
    rows = [lower_ref[n, h] for n in range(lower_ref.shape[0])]
    m = functools.reduce(jnp.maximum, rows)
    es = [jnp.exp(r - m) for r in rows]
    tot = functools.reduce(lambda p, q: p + q, es)
    sm = [e / tot for e in es]
    cs = functools.reduce(lambda p, q: p + q, sm[:layer + 1])
    return jnp.maximum(cs - sm[0], 0.0)


def _chunk_cumsum(cb_ref, tb_ref, eb_ref, tm, chunk):
    groups = tm // SUBLANES
    p = cb_ref[pl.ds(0, groups, stride=SUBLANES), :]
    for k in range(1, SUBLANES):
        p = p + cb_ref[pl.ds(k, groups, stride=SUBLANES), :]
        cb_ref[pl.ds(k, groups, stride=SUBLANES), :] = p
    if chunk > SUBLANES:
        gc = chunk // SUBLANES
        nch = tm // chunk
        tb_ref[...] = p
        e = jnp.zeros((nch, HD), F32)
        eb_ref[pl.ds(0, nch, stride=gc), :] = e
        for j in range(1, gc):
            e = e + tb_ref[pl.ds(j - 1, nch, stride=gc), :]
            eb_ref[pl.ds(j, nch, stride=gc), :] = e
        off = eb_ref[...]
        for k in range(SUBLANES):
            cb_ref[pl.ds(k, groups, stride=SUBLANES), :] = (
                cb_ref[pl.ds(k, groups, stride=SUBLANES), :] + off)
    return p


def _hgrn_front(p_ref, lower_ref, h, layer, tm, chunk,
                cb_ref, tb_ref, eb_ref, kk_ref, v_ref, sg_ref, qs_ref, qe_ref):
    lb = _lower_bound(lower_ref, h, layer)
    fg = lb + (1.0 - lb) * _sigmoid(p_ref[:, HD:2 * HD])
    cb_ref[...] = jnp.log(jnp.maximum(fg, F_FLOOR))
    kk_ref[...] = 1.0 - fg
    q = p_ref[:, 0:HD]
    qs = q * _sigmoid(q) * (HD ** -0.5)
    v_ref[...] = p_ref[:, 2 * HD:3 * HD]
    gg = p_ref[:, 3 * HD:4 * HD]
    sg_ref[...] = gg * _sigmoid(gg)
    totals = _chunk_cumsum(cb_ref, tb_ref, eb_ref, tm, chunk)
    cum = cb_ref[...]
    qs_ref[...] = qs
    qe_ref[...] = qs * jnp.exp(cum)
    safe = jnp.max(-cum) <= SAFE_DECAY
    return safe, totals


def _same_chunk_mask(chunk):
    ri = lax.broadcasted_iota(jnp.int32, (BLK, BLK), 0)
    ci = lax.broadcasted_iota(jnp.int32, (BLK, BLK), 1)
    lc = chunk.bit_length() - 1
    return ((ri >> lc) == (ci >> lc)) & (ci <= ri)


def _inverse_decayed_keys(kk, cum):
    return kk * jnp.exp(jnp.minimum(-cum, SAFE_DECAY))


def _robust_same_chunk_scores(rows, chunk, cb_ref, kk_ref, qs_ref):
    ri = lax.broadcasted_iota(jnp.int32, (BLK, BLK), 0)
    ci = lax.broadcasted_iota(jnp.int32, (BLK, BLK), 1)
    cum = cb_ref[rows, :]
    qs = qs_ref[rows, :]
    kk = kk_ref[rows, :]
    hi, mid, lo = _split3(cum)
    acc = jnp.where(ri == ci, _dot_nt(qs.astype(BF16), kk.astype(BF16)), 0.0)
    for lvl in range(chunk.bit_length() - 1):
        m = 1 << lvl
        pivot = ((ri >> (lvl + 1)) << (lvl + 1)) + (m - 1)
        sel = jnp.where(ci == pivot, 1.0, 0.0).astype(BF16)
        ref_cum = _dot(sel, hi) + _dot(sel, mid) + _dot(sel, lo)
        e = jnp.exp(-jnp.abs(cum - ref_cum))
        keep = (((ri >> (lvl + 1)) == (ci >> (lvl + 1)))
                & ((ri & (2 * m - 1)) >= m) & ((ci & (2 * m - 1)) < m))
        s = _dot_nt((qs * e).astype(BF16), (kk * e).astype(BF16))
        acc = acc + jnp.where(keep, s, 0.0)
    return acc


def _rms_gate(o, ng, sg):
    return o * lax.rsqrt(jnp.mean(o * o, axis=-1, keepdims=True) + RMS_EPS) * ng * sg


def _hgrn_prompt_kernel(x_ref, lower_ref, wi_ref, ng_ref, wo_ref, g_ref, b_ref, y_ref, so_ref,
                        xb_ref, p_ref, cb_ref, tb_ref, eb_ref, kk_ref, v_ref, sg_ref, qs_ref,
                        qe_ref, kd_ref, a_ref, oh_ref, ds_ref, ec_ref, ob_ref, s_ref,
                        *, tm, layer, tiles_per_seq, alpha):
    chunk = PROMPT_CHUNK
    assert BLK == 2 * chunk
    n_blocks = tm // BLK
    i = pl.program_id(0)

    @pl.when(i % tiles_per_seq == 0)
    def _():
        s_ref[...] = jnp.zeros_like(s_ref)

    xb_ref[...] = x_ref[...].astype(BF16)
    second = lax.broadcasted_iota(jnp.int32, (BLK, HD), 0) >= chunk
    ri = lax.broadcasted_iota(jnp.int32, (BLK, BLK), 0)
    ci = lax.broadcasted_iota(jnp.int32, (BLK, BLK), 1)
    cross = (ri >= chunk) & (ci < chunk)
    same = _same_chunk_mask(chunk)

    def project(h, slot):
        p_ref[slot] = _dot(xb_ref[...], wi_ref[h])

    def views(slot):
        return tuple(r.at[slot] for r in (cb_ref, tb_ref, eb_ref, kk_ref, v_ref, sg_ref, qs_ref,
                                          qe_ref, kd_ref, a_ref, oh_ref, ds_ref, ec_ref))

    def scores(h, slot):
        cb, tb, eb, kk_s, v_s, sg_s, qs_s, qe_s, kd_s, a_s, _, ds_s, ec_s = views(slot)
        safe, _ = _hgrn_front(p_ref.at[slot], lower_ref, h, layer, tm, chunk,
                              cb, tb, eb, kk_s, v_s, sg_s, qs_s, qe_s)
        for nb in range(n_blocks):
            rows = pl.ds(nb * BLK, BLK)
            cum = cb[rows, :]
            kk = kk_s[rows, :]
            last0 = cb[pl.ds(nb * BLK + chunk - 1, 1), :]
            last1 = cb[pl.ds(nb * BLK + BLK - 1, 1), :]
            kd = kk * jnp.exp(jnp.where(second, last1, last0) - cum)
            kd_s[rows, :] = kd
            keys = jnp.concatenate([_inverse_decayed_keys(kk, cum), kd], axis=0).astype(BF16)
            s2 = _dot_nt(qe_s[rows, :].astype(BF16), keys)
            a = jnp.where(same, s2[:, 0:BLK], 0.0) + jnp.where(cross, s2[:, BLK:2 * BLK], 0.0)
            a_s[nb] = a.astype(BF16)
            k_blk = kd * jnp.where(second, 1.0, jnp.exp(last1))
            ds_s[nb] = _dot(k_blk.T.astype(BF16), v_s[rows, :].astype(BF16))
            ec_s[nb] = jnp.broadcast_to(jnp.exp(last0 + last1), (HD, HD)).T
        return safe

    def fix_scores(safe, slot):
        cb, _, _, kk_s, _, _, qs_s, qe_s, kd_s, a_s, _, _, _ = views(slot)

        @pl.when(jnp.logical_not(safe))
        def _():
            def fix(nb, _):
                rows = pl.ds(pl.multiple_of(nb * BLK, BLK), BLK)
                s1 = _dot_nt(qe_s[rows, :].astype(BF16), kd_s[rows, :].astype(BF16))
                a = _robust_same_chunk_scores(rows, chunk, cb, kk_s, qs_s)
                a_s[nb] = (a + jnp.where(cross, s1, 0.0)).astype(BF16)
                return 0
            lax.fori_loop(0, n_blocks, fix, 0)

    def outputs(h, slot):
        cb, _, _, _, v_s, sg_s, _, qe_s, _, a_s, oh_s, ds_s, ec_s = views(slot)
        states = [s_ref[h]]
        for nb in range(n_blocks):
            states.append(ec_s[nb] * states[nb] + ds_s[nb])
        s_ref[h] = states[n_blocks]
        for nb in range(n_blocks):
            rows = pl.ds(nb * BLK, BLK)
            last0 = cb[pl.ds(nb * BLK + chunk - 1, 1), :]
            q_blk = qe_s[rows, :] * jnp.where(second, jnp.exp(last0), 1.0)
            lhs = jnp.concatenate([a_s[nb], q_blk.astype(BF16)], axis=1)
            rhs = jnp.concatenate([v_s[rows, :].astype(BF16), states[nb].astype(BF16)], axis=0)
            oh_s[rows, :] = _dot(lhs, rhs)
        ob_ref[h] = _rms_gate(oh_s[...], ng_ref[...], sg_s[...]).astype(BF16)

    project(0, 0)

    def pair(hh, _):
        h0 = 2 * hh
        project(h0 + 1, 1)
        safe0 = scores(h0, 0)
        project(jnp.minimum(h0 + 2, HEADS - 1), 0)
        safe1 = scores(h0 + 1, 1)
        fix_scores(safe0, 0)
        fix_scores(safe1, 1)
        outputs(h0, 0)
        outputs(h0 + 1, 1)
        return 0

    lax.fori_loop(0, HEADS // 2, pair, 0)
    for h in range(HEADS):
        xb_ref[:, h * HD:(h + 1) * HD] = ob_ref[h]
    f = _dot(xb_ref[...], wo_ref[...])
    y_ref[...] = _deepnorm(x_ref[...], f, g_ref[...], b_ref[...], alpha)

    @pl.when(i % tiles_per_seq == tiles_per_seq - 1)
    def _():
        so_ref[0] = s_ref[...]


def _hgrn_prep(hg_lower, wi, ng, wo):
    n_layers = hg_lower.shape[0]
    lower = hg_lower.reshape(n_layers, HEADS, 1, HD)
    wih = (wi.reshape(n_layers, D_MODEL, 4, HEADS, HD).transpose(0, 3, 1, 2, 4)
           .reshape(n_layers, HEADS, D_MODEL, 4 * HD))
    return lower, wih.astype(BF16), ng.reshape(n_layers, 1, HD), wo.astype(BF16)


def _hgrn_prompt_call(x, prep, layer, g, b, *, seq_len, alpha):
    lower, wih, ng, wo = prep
    T = x.shape[0]
    tm = TM_PROMPT
    tiles_per_seq = seq_len // tm
    n_seq = T // seq_len
    row = pl.BlockSpec((tm, D_MODEL), lambda i: (i, 0))
    specs_args = [_layer_operand(w) for w in (lower, (wih, layer), (ng, layer), (wo, layer), g, b)]
    head_buf = pltpu.VMEM((2, tm, HD), F32)
    scratch = [pltpu.VMEM((tm, D_MODEL), BF16),
               pltpu.VMEM((2, tm, 4 * HD), F32),
               head_buf,
               pltpu.VMEM((2, tm // SUBLANES, HD), F32),
               pltpu.VMEM((2, tm // SUBLANES, HD), F32),
               head_buf, head_buf, head_buf, head_buf, head_buf,
               head_buf,
               pltpu.VMEM((2, tm // BLK, BLK, BLK), BF16),
               head_buf,
               pltpu.VMEM((2, tm // BLK, HD, HD), F32),
               pltpu.VMEM((2, tm // BLK, HD, HD), F32),
               pltpu.VMEM((HEADS, tm, HD), BF16),
               pltpu.VMEM((HEADS, HD, HD), F32)]
    y, so = pl.pallas_call(
        functools.partial(_hgrn_prompt_kernel, tm=tm, layer=layer,
                          tiles_per_seq=tiles_per_seq, alpha=alpha),
        grid=(T // tm,),
        in_specs=[row] + [s for s, _ in specs_args],
        out_specs=[row, pl.BlockSpec((1, HEADS, HD, HD), lambda i: (i // tiles_per_seq, 0, 0, 0))],
        out_shape=[jax.ShapeDtypeStruct((T, D_MODEL), F32),
                   jax.ShapeDtypeStruct((n_seq, HEADS, HD, HD), F32)],
        scratch_shapes=scratch,
        compiler_params=_params(),
        name="hgrn_prompt",
    )(x, *[a for _, a in specs_args])
    return y, so


def _hgrn_sample_front_kernel(x_ref, lower_ref, wi_ref,
                              oi_ref, qe_ref, v_ref, sg_ref, kdt_ref, et_ref,
                              xb_ref, p_ref, cb_ref, kk_ref, qs_ref, a_ref, last_ref,
                              *, tm, chunk, layer):
    assert chunk == SUBLANES
    xb_ref[...] = x_ref[...].astype(BF16)
    groups = tm // SUBLANES
    n_blocks = tm // BLK
    same = _same_chunk_mask(chunk)

    def head(h, _):
        p_ref[...] = _dot(xb_ref[...], wi_ref[h])
        safe, totals = _hgrn_front(p_ref, lower_ref, h, layer, tm, chunk,
                                   cb_ref, None, None, kk_ref, v_ref.at[h], sg_ref.at[h],
                                   qs_ref, qe_ref.at[h])
        for k in range(SUBLANES):
            last_ref[pl.ds(k, groups, stride=SUBLANES), :] = totals
        for nb in range(n_blocks):
            r0 = nb * BLK
            rows = pl.ds(r0, BLK)
            cum = cb_ref[rows, :]
            kk = kk_ref[rows, :]
            s = _dot_nt(qe_ref[h, rows, :].astype(BF16), _inverse_decayed_keys(kk, cum).astype(BF16))
            a_ref[nb] = jnp.where(same, s, 0.0).astype(BF16)
            last = last_ref[rows, :]
            kdt_ref[h, :, r0:r0 + BLK] = (kk * jnp.exp(last - cum)).T
            et_ref[h, :, r0:r0 + BLK] = jnp.exp(last).T

        @pl.when(jnp.logical_not(safe))
        def _():
            def fix(nb, _):
                rows = pl.ds(pl.multiple_of(nb * BLK, BLK), BLK)
                a_ref[nb] = _robust_same_chunk_scores(rows, chunk, cb_ref, kk_ref, qs_ref).astype(BF16)
                return 0
            lax.fori_loop(0, n_blocks, fix, 0)

        for nb in range(n_blocks):
            rows = pl.ds(nb * BLK, BLK)
            oi_ref[h, rows, :] = _dot(a_ref[nb], v_ref[h, rows, :].astype(BF16))
        return 0

    lax.fori_loop(0, HEADS, head, 0)


def _hgrn_sample_state_kernel(qe_ref, oi_ref, v_ref, sg_ref, kdt_ref, et_ref, s0_ref, ng_ref,
                              *rest, chunk):
    op_ref, so_ref = rest[-2:]
    row_seq = lax.broadcasted_iota(jnp.int32, (BLK, HD), 0) >> (chunk.bit_length() - 1)

    def head(h, _):
        kdt = kdt_ref[h].astype(BF16)
        et = et_ref[h]
        v = v_ref[h]
        for sq in range(BLK // chunk):
            rows = slice(sq * chunk, (sq + 1) * chunk)
            s0 = s0_ref[sq, h]
            o = oi_ref[h, rows, :] + _dot(qe_ref[h, rows, :].astype(BF16), s0.astype(BF16))
            op_ref[h, rows, :] = _rms_gate(o, ng_ref[...], sg_ref[h, rows, :])
            vm = jnp.where(row_seq == sq, v, 0.0).astype(BF16)
            e_col = jnp.broadcast_to(et[:, sq * chunk:sq * chunk + 1], (HD, HD))
            so_ref[sq, h] = e_col * s0 + _dot(kdt, vm)
        return 0

    lax.fori_loop(0, HEADS, head, 0)


def _hgrn_sample_out_kernel(x_ref, op_ref, wo_ref, g_ref, b_ref, y_ref, ob_ref, *, alpha):
    for h in range(HEADS):
        ob_ref[:, h * HD:(h + 1) * HD] = op_ref[h].astype(BF16)
    f = _dot(ob_ref[...], wo_ref[...])
    y_ref[...] = _deepnorm(x_ref[...], f, g_ref[...], b_ref[...], alpha)


def _hgrn_sample_call(x, s0, so_prev, prep, layer, g, b, *, seq_len, alpha):
    lower, wih, ng, wo = prep
    T = x.shape[0]
    chunk = seq_len
    heads_rows = jax.ShapeDtypeStruct((HEADS, T, HD), F32)
    heads_cols = jax.ShapeDtypeStruct((HEADS, HD, T), F32)
    full = lambda s: pl.BlockSpec(s, lambda i: (0,) * len(s))
    head_buf = pltpu.VMEM((T, HD), F32)
    specs_args = [_layer_operand(w) for w in (lower, (wih, layer))]
    oi, qe, v, sg, kdt, et = pl.pallas_call(
        functools.partial(_hgrn_sample_front_kernel, tm=T, chunk=chunk, layer=layer),
        grid=(1,),
        in_specs=[full(x.shape)] + [s for s, _ in specs_args],
        out_specs=[full(heads_rows.shape)] * 4 + [full(heads_cols.shape)] * 2,
        out_shape=[heads_rows] * 4 + [heads_cols] * 2,
        scratch_shapes=[pltpu.VMEM((T, D_MODEL), BF16), pltpu.VMEM((T, 4 * HD), F32),
                        head_buf, head_buf, head_buf,
                        pltpu.VMEM((T // BLK, BLK, BLK), BF16), head_buf],
        compiler_params=_params(),
        name="hgrn_sample_front",
    )(x, *[a for _, a in specs_args])
    seqs = BLK // chunk
    by_rows = pl.BlockSpec((HEADS, BLK, HD), lambda i: (0, i, 0))
    by_cols = pl.BlockSpec((HEADS, HD, BLK), lambda i: (0, 0, i))
    state = pl.BlockSpec((None, seqs, HEADS, HD, HD), lambda i: (layer, i, 0, 0, 0))
    ng_spec, ng_arr = _layer_operand((ng, layer))
    in_specs = [by_rows, by_rows, by_rows, by_rows, by_cols, by_cols, state, ng_spec]
    args = [qe, oi, v, sg, kdt, et, s0, ng_arr]
    aliases = {}
    if so_prev is not None:
        in_specs.append(pl.BlockSpec(memory_space=pl.ANY))
        args.append(so_prev)
        aliases = {len(args) - 1: 1}
    op, so = pl.pallas_call(
        functools.partial(_hgrn_sample_state_kernel, chunk=chunk),
        grid=(T // BLK,),
        in_specs=in_specs,
        out_specs=[by_rows, state],
        out_shape=[heads_rows, jax.ShapeDtypeStruct(s0.shape, F32)],
        input_output_aliases=aliases,
        compiler_params=_params(),
        name="hgrn_sample_state",
    )(*args)
    specs_args = [_layer_operand(w) for w in ((wo, layer), g, b)]
    y = pl.pallas_call(
        functools.partial(_hgrn_sample_out_kernel, alpha=alpha),
        grid=(1,),
        in_specs=[full(x.shape), full(op.shape)] + [s for s, _ in specs_args],
        out_specs=full(x.shape),
        out_shape=jax.ShapeDtypeStruct(x.shape, F32),
        scratch_shapes=[pltpu.VMEM((T, D_MODEL), BF16)],
        compiler_params=_params(),
        name="hgrn_sample_out",
    )(x, op, *[a for _, a in specs_args])
    return y, so


def _to_time_major(a):
    return jnp.transpose(a, (1, 0, 2)).reshape(-1, a.shape[-1])


def _from_time_major(a, batch):
    return jnp.transpose(a.reshape(-1, batch, a.shape[-1]), (1, 0, 2))


def kernel(x_prompt, x_sample, state_rglru_h, state_rglru_conv, state_hgrn_s, state_ffn_conv,
           ln_g, ln_b, rg_w_in, rg_conv_w, rg_conv_b, rg_gate_w, rg_gate_b, rg_lambda, rg_w_out,
           hg_lower, hg_w_in, hg_norm_g, hg_w_out, ffn_w_in, ffn_conv_w, ffn_conv_b, ffn_w_out):
    depth = ln_g.shape[0]
    alpha = (2.0 * depth) ** 0.25
    pb, pl_len, _ = x_prompt.shape
    sb, sl_len, _ = x_sample.shape

    n_rg, n_ffn = rg_w_in.shape[0], ffn_w_in.shape[0]
    lng = ln_g.reshape(depth * 2, 1, D_MODEL)
    lnb = ln_b.reshape(depth * 2, 1, D_MODEL)
    rg_wi = rg_w_in.astype(BF16)
    rg_wg = jnp.concatenate([rg_gate_w[:, 0], rg_gate_w[:, 1]], axis=-1).astype(BF16)
    rg_wo = rg_w_out.astype(BF16)
    rg_cb = rg_conv_b.reshape(n_rg, 1, D_MODEL)
    rg_lam = rg_lambda.reshape(n_rg, 1, D_MODEL)
    hg = _hgrn_prep(hg_lower, hg_w_in, hg_norm_g, hg_w_out)
    ffn_wi = ffn_w_in.astype(BF16)
    ffn_wo = ffn_w_out.astype(BF16)
    ffn_cb = ffn_conv_b.reshape(n_ffn, 1, D_FF)
    rg_cst = jnp.transpose(state_rglru_conv, (0, 2, 1, 3)).reshape(n_rg, -1, D_MODEL)
    ffn_cst = jnp.transpose(state_ffn_conv, (0, 2, 1, 3)).reshape(n_ffn, -1, D_FF)

    xp = x_prompt.reshape(pb * pl_len, D_MODEL)
    xs = _to_time_major(x_sample)
    p_h, p_rc, p_s, p_fc = [], [], [], []
    s_h, s_rc, s_fc = [], [], []
    s_s = None
    for i in range(depth):
        j = i // 2
        g0, b0 = (lng, 2 * i), (lnb, 2 * i)
        if i % 2 == 0:
            w = ((rg_wi, j), (rg_conv_w, j), (rg_cb, j), (rg_wg, j), (rg_gate_b, j), (rg_lam, j),
                 (rg_wo, j), g0, b0)
            xp, ho, co = _rglru_call(xp, None, None, *w, seq_len=pl_len, alpha=alpha)
            p_h.append(ho[:, 0])
            p_rc.append(co)
            xs, ho, co = _rglru_call(xs, (state_rglru_h, j), (rg_cst, j), *w,
                                     seq_len=sl_len, alpha=alpha)
            s_h.append(ho)
            s_rc.append(co)
        else:
            xp, so = _hgrn_prompt_call(xp, hg, j, g0, b0, seq_len=pl_len, alpha=alpha)
            p_s.append(so)
            xs_bm = _from_time_major(xs, sb).reshape(sb * sl_len, D_MODEL)
            ys_bm, s_s = _hgrn_sample_call(xs_bm, state_hgrn_s, s_s, hg, j, g0, b0,
                                           seq_len=sl_len, alpha=alpha)
            xs = _to_time_major(ys_bm.reshape(sb, sl_len, D_MODEL))
        w = ((ffn_wi, i), (ffn_conv_w, i), (ffn_cb, i), (ffn_wo, i), (lng, 2 * i + 1), (lnb, 2 * i + 1))
        xp, fo = _ffn_call(xp, None, *w, seq_len=pl_len, alpha=alpha)
        p_fc.append(fo)
        xs, fo = _ffn_call(xs, (ffn_cst, i), *w, seq_len=sl_len, alpha=alpha)
        s_fc.append(fo)

    def stacked_from_time_major(parts):
        a = jnp.stack(parts)
        a = a.reshape(a.shape[0], -1, sb, a.shape[-1])
        return jnp.transpose(a, (0, 2, 1, 3))

    return (xp.reshape(x_prompt.shape), _from_time_major(xs, sb),
            jnp.stack(p_h), jnp.stack(p_rc), jnp.stack(p_s), jnp.stack(p_fc),
            jnp.stack(s_h), stacked_from_time_major(s_rc), s_s, stacked_from_time_major(s_fc))
```

```python
import functools

import jax
import jax.numpy as jnp
from jax import lax
from jax.experimental import pallas as pl
from jax.experimental.pallas import tpu as pltpu

F32 = jnp.float32
BF16 = jnp.bfloat16

D_MODEL = 1024
D_FF = 2816
HEADS = 8
HD = 128
RG_CONV = 4
FFN_CONV = 3
RG_C = 8.0
LN_EPS = 1e-5
RMS_EPS = 1e-6
F_FLOOR = 1e-30
SUBLANES = 8
LANES = 128
BLK = 128
PROMPT_CHUNK = 64
SAFE_DECAY = 80.0
FF_TILE = 256
TM_PROMPT = 512
VMEM_LIMIT = 56 * 1024 * 1024


def _dot(a, b):
    return jnp.dot(a, b, preferred_element_type=F32)


def _dot_nt(a, b):
    return lax.dot_general(a, b, (((1,), (1,)), ((), ())), preferred_element_type=F32)


def _sigmoid(x):
    return 0.5 * jnp.tanh(0.5 * x) + 0.5


def _resident(shape):
    nd = len(shape)
    return pl.BlockSpec(shape, lambda i: (0,) * nd, pipeline_mode=pl.Buffered(1))


def _layer_shape(w):
    return w[0].shape[1:] if isinstance(w, tuple) else w.shape


def _layer_operand(w):
    if not isinstance(w, tuple):
        return _resident(w.shape), w
    arr, layer = w
    tail = (0,) * (arr.ndim - 1)
    spec = pl.BlockSpec((None,) + arr.shape[1:], lambda i: (layer,) + tail,
                        pipeline_mode=pl.Buffered(1))
    return spec, arr


def _deepnorm(x, f, g, b, alpha):
    z = alpha * x + f
    mu = jnp.mean(z, axis=-1, keepdims=True)
    zc = z - mu
    var = jnp.mean(zc * zc, axis=-1, keepdims=True)
    return zc * lax.rsqrt(var + LN_EPS) * g + b


def _params(n_axes=1):
    return pltpu.CompilerParams(dimension_semantics=("arbitrary",) * n_axes,
                                vmem_limit_bytes=VMEM_LIMIT)


def _ffn_kernel(*refs, tm, step, tiles_per_seq, has_state, alpha):
    if has_state:
        (x_ref, st_ref, wi_ref, cw_ref, cb_ref, wo_ref, g_ref, b_ref,
         y_ref, so_ref, gbuf, hbuf) = refs
        carry = None
    else:
        (x_ref, wi_ref, cw_ref, cb_ref, wo_ref, g_ref, b_ref,
         y_ref, so_ref, gbuf, hbuf, carry) = refs
    halo = gbuf.shape[0] - tm
    xb = x_ref[...].astype(BF16)
    if not has_state:
        @pl.when(pl.program_id(0) % tiles_per_seq == 0)
        def _():
            carry[...] = jnp.zeros_like(carry)
    for j in range(D_FF // FF_TILE):
        c0 = j * FF_TILE
        g = _dot(xb, wi_ref[:, c0:c0 + FF_TILE])
        u = _dot(xb, wi_ref[:, D_FF + c0:D_FF + c0 + FF_TILE])
        if has_state:
            gbuf[0:halo, :] = st_ref[:, c0:c0 + FF_TILE]
        else:
            gbuf[0:halo, :] = carry[:, c0:c0 + FF_TILE]
        gbuf[halo:halo + tm, :] = g
        g1 = gbuf[halo - step:halo - step + tm, :]
        g2 = gbuf[halo - 2 * step:halo - 2 * step + tm, :]
        cw = cw_ref[:, c0:c0 + FF_TILE]
        gc = g * cw[2:3] + g1 * cw[1:2] + g2 * cw[0:1] + cb_ref[:, c0:c0 + FF_TILE]
        hbuf[:, c0:c0 + FF_TILE] = (jax.nn.gelu(gc) * u).astype(BF16)
        if has_state:
            so_ref[:, c0:c0 + FF_TILE] = gbuf[tm:tm + halo, :]
        else:
            carry[:, c0:c0 + FF_TILE] = gbuf[tm:tm + halo, :]
            so_ref[0, :, c0:c0 + FF_TILE] = gbuf[halo + tm - 2:halo + tm, :]
    f = _dot(hbuf[...], wo_ref[...])
    y_ref[...] = _deepnorm(x_ref[...], f, g_ref[...], b_ref[...], alpha)


def _ffn_call(x, state_tm, wi, cw, cb, wo, g, b, *, seq_len, alpha):
    T = x.shape[0]
    has_state = state_tm is not None
    if has_state:
        st_shape = _layer_shape(state_tm)
        tm, step, halo, tiles_per_seq = T, T // seq_len, st_shape[0], 1
    else:
        tm, step, halo, tiles_per_seq = TM_PROMPT, 1, SUBLANES, seq_len // TM_PROMPT
    n_tiles = T // tm
    n_seq = T // seq_len
    row = pl.BlockSpec((tm, D_MODEL), lambda i: (i, 0))
    operands = ([state_tm] if has_state else []) + [wi, cw, cb, wo, g, b]
    specs_args = [_layer_operand(w) for w in operands]
    in_specs = [row] + [s for s, _ in specs_args]
    args = [x] + [a for _, a in specs_args]
    scratch = [pltpu.VMEM((halo + tm, FF_TILE), F32), pltpu.VMEM((tm, D_FF), BF16)]
    if has_state:
        so_shape = jax.ShapeDtypeStruct(st_shape, F32)
        so_spec = pl.BlockSpec(st_shape, lambda i: (0, 0))
    else:
        so_shape = jax.ShapeDtypeStruct((n_seq, FFN_CONV - 1, D_FF), F32)
        so_spec = pl.BlockSpec((1, FFN_CONV - 1, D_FF), lambda i: (i // tiles_per_seq, 0, 0))
        scratch.append(pltpu.VMEM((halo, D_FF), F32))
    return pl.pallas_call(
        functools.partial(_ffn_kernel, tm=tm, step=step, tiles_per_seq=tiles_per_seq,
                          has_state=has_state, alpha=alpha),
        grid=(n_tiles,),
        in_specs=in_specs,
        out_specs=[row, so_spec],
        out_shape=[jax.ShapeDtypeStruct((T, D_MODEL), F32), so_shape],
        scratch_shapes=scratch,
        compiler_params=_params(),
        name="ffn_state" if has_state else "ffn_prompt",
    )(*args)


def _rglru_gates(xc, gt, h, bg_ref, log_sig_lam):
    c0 = h * HD
    bg = bg_ref[:, c0:c0 + HD]
    r = _sigmoid(gt[:, 0:HD] + bg[0:1])
    ig = _sigmoid(gt[:, HD:2 * HD] + bg[1:2])
    a = jnp.exp(RG_C * r * log_sig_lam[:, c0:c0 + HD])
    return a, jnp.sqrt(jnp.maximum(1.0 - a * a, 0.0)) * ig * xc


def _rglru_state_kernel(x_ref, h0_ref, cst_ref, wi_ref, cw_ref, cb_ref, wg_ref, bg_ref, lam_ref,
                        wo_ref, g_ref, b_ref, y_ref, ho_ref, co_ref, xbuf, ybuf, *, tm, step, alpha):
    halo = xbuf.shape[0] - tm
    xb = x_ref[...].astype(BF16)
    xbuf[0:halo, :] = cst_ref[...]
    log_sig_lam = jax.nn.log_sigmoid(lam_ref[...])
    for h in range(HEADS):
        c0 = h * HD
        proj = _dot(xb, wi_ref[:, 2 * c0:2 * c0 + 2 * HD])
        gate = jax.nn.gelu(proj[:, 0:HD])
        xbuf[halo:halo + tm, c0:c0 + HD] = proj[:, HD:2 * HD]
        cw = cw_ref[:, c0:c0 + HD]
        xc = cb_ref[:, c0:c0 + HD] + proj[:, HD:2 * HD] * cw[RG_CONV - 1:RG_CONV]
        for j in range(RG_CONV - 1):
            back = (RG_CONV - 1 - j) * step
            xc = xc + xbuf[halo - back:halo - back + tm, c0:c0 + HD] * cw[j:j + 1]
        a, bv = _rglru_gates(xc, _dot(xc.astype(BF16), wg_ref[h]), h, bg_ref, log_sig_lam)
        hh = h0_ref[:, c0:c0 + HD]
        for t in range(tm // step):
            rows = slice(t * step, (t + 1) * step)
            hh = a[rows] * hh + bv[rows]
            ybuf[rows, c0:c0 + HD] = (hh * gate[rows]).astype(BF16)
        ho_ref[:, c0:c0 + HD] = hh
    co_ref[...] = xbuf[tm:tm + halo, :]
    f = _dot(ybuf[...], wo_ref[...])
    y_ref[...] = _deepnorm(x_ref[...], f, g_ref[...], b_ref[...], alpha)


def _rglru_prompt_kernel(x_ref, wi_ref, cw_ref, cb_ref, wg_ref, bg_ref, lam_ref, wo_ref, g_ref, b_ref,
                         y_ref, ho_ref, co_ref,
                         xbuf, gbuf, hbuf, tot, hin, sup, ybuf, ccarry, hcarry,
                         *, tm, tiles_per_seq, alpha):
    halo = xbuf.shape[1] - tm
    groups = tm // SUBLANES
    supers = groups // SUBLANES
    taps = RG_CONV - 1

    @pl.when(pl.program_id(0) % tiles_per_seq == 0)
    def _():
        ccarry[...] = jnp.zeros_like(ccarry)
        hcarry[...] = jnp.zeros_like(hcarry)

    xb = x_ref[...].astype(BF16)
    log_sig_lam = jax.nn.log_sigmoid(lam_ref[...])

    def project(h):
        c0 = h * HD
        proj = _dot(xb, wi_ref[:, 2 * c0:2 * c0 + 2 * HD])
        gbuf[:, c0:c0 + HD] = jax.nn.gelu(proj[:, 0:HD])
        xs = xbuf.at[h]
        xs[0:halo, :] = ccarry[h]
        xs[halo:halo + tm, :] = proj[:, HD:2 * HD]
        ccarry[h] = xs[tm:tm + halo, :]
        co_ref[0, :, c0:c0 + HD] = xs[halo + tm - taps:halo + tm, :]

    def conv(h):
        c0 = h * HD
        xs = xbuf.at[h]
        cw = cw_ref[:, c0:c0 + HD]
        cb = cb_ref[:, c0:c0 + HD]
        lock = [xs[pl.ds(halo - taps + m, groups, stride=SUBLANES), :]
                for m in range(SUBLANES + taps)]
        xc = []
        for k in range(SUBLANES):
            acc = cb + lock[k + taps] * cw[taps:taps + 1]
            for j in range(taps):
                acc = acc + lock[k + j] * cw[j:j + 1]
            xc.append(acc)
        xc = jnp.concatenate(xc, axis=0)
        return xc, _dot(xc.astype(BF16), wg_ref[h])

    def scan(h, xc, gt):
        c0 = h * HD
        a, bv = _rglru_gates(xc, gt, h, bg_ref, log_sig_lam)
        pa, pb = a[0:groups], bv[0:groups]
        a_in, b_in = [pa], [pb]
        for k in range(1, SUBLANES):
            ak = a[k * groups:(k + 1) * groups]
            pb = ak * pb + bv[k * groups:(k + 1) * groups]
            pa = ak * pa
            a_in.append(pa)
            b_in.append(pb)
        tot[h, 0] = pa
        tot[h, 1] = pb
        qa = tot[h, 0, pl.ds(0, supers, stride=SUBLANES), :]
        qb = tot[h, 1, pl.ds(0, supers, stride=SUBLANES), :]
        a_sup, b_sup = [qa], [qb]
        for j in range(1, SUBLANES):
            aj = tot[h, 0, pl.ds(j, supers, stride=SUBLANES), :]
            qb = aj * qb + tot[h, 1, pl.ds(j, supers, stride=SUBLANES), :]
            qa = aj * qa
            a_sup.append(qa)
            b_sup.append(qb)
        hc = hcarry[h]
        for s in range(supers):
            sup[h, pl.ds(s, 1), :] = hc
            hc = qa[s:s + 1] * hc + qb[s:s + 1]
        hcarry[h] = hc
        ho_ref[0, :, c0:c0 + HD] = hc
        h_sup = sup[h]
        hin[h, pl.ds(0, supers, stride=SUBLANES), :] = h_sup
        for j in range(1, SUBLANES):
            hin[h, pl.ds(j, supers, stride=SUBLANES), :] = a_sup[j - 1] * h_sup + b_sup[j - 1]
        h_grp = hin[h]
        for k in range(SUBLANES):
            hbuf[h, pl.ds(k, groups, stride=SUBLANES), :] = a_in[k] * h_grp + b_in[k]

    convs = {}
    for h in range(HEADS + 2):
        if h < HEADS:
            project(h)
        if 1 <= h <= HEADS:
            convs[h - 1] = conv(h - 1)
        if h >= 2:
            scan(h - 2, *convs.pop(h - 2))
    for h in range(HEADS):
        c0 = h * HD
        ybuf[:, c0:c0 + HD] = (hbuf[h] * gbuf[:, c0:c0 + HD]).astype(BF16)
    f = _dot(ybuf[...], wo_ref[...])
    y_ref[...] = _deepnorm(x_ref[...], f, g_ref[...], b_ref[...], alpha)


def _rglru_in_proj(w_in):
    lead = w_in.shape[:-1]
    w = w_in.reshape(*lead, 2, HEADS, HD)
    return jnp.swapaxes(w, -3, -2).reshape(*lead, 2 * HEADS * HD).astype(BF16)


def _rglru_call(x, h0, cst_tm, wi, cw, cb, wg, bg, lam, wo, g, b, *, seq_len, alpha):
    T = x.shape[0]
    has_state = h0 is not None
    n_seq = T // seq_len
    operands = ([h0, cst_tm] if has_state else []) + [wi, cw, cb, wg, bg, lam, wo, g, b]
    specs_args = [_layer_operand(w) for w in operands]
    args = [x] + [a for _, a in specs_args]
    if has_state:
        h0_shape, cst_shape = _layer_shape(h0), _layer_shape(cst_tm)
        tm, halo = T, cst_shape[0]
        body = functools.partial(_rglru_state_kernel, tm=tm, step=T // seq_len, alpha=alpha)
        row = pl.BlockSpec((tm, D_MODEL), lambda i: (i, 0))
        out_shape = [jax.ShapeDtypeStruct((T, D_MODEL), F32),
                     jax.ShapeDtypeStruct(h0_shape, F32),
                     jax.ShapeDtypeStruct(cst_shape, F32)]
        out_specs = [row, pl.BlockSpec(h0_shape, lambda i: (0, 0)),
                     pl.BlockSpec(cst_shape, lambda i: (0, 0))]
        scratch = [pltpu.VMEM((halo + tm, D_MODEL), F32), pltpu.VMEM((tm, D_MODEL), BF16)]
    else:
        tm, halo, tiles_per_seq = TM_PROMPT, SUBLANES, seq_len // TM_PROMPT
        groups = tm // SUBLANES
        body = functools.partial(_rglru_prompt_kernel, tm=tm, tiles_per_seq=tiles_per_seq, alpha=alpha)
        row = pl.BlockSpec((tm, D_MODEL), lambda i: (i, 0))
        out_shape = [jax.ShapeDtypeStruct((T, D_MODEL), F32),
                     jax.ShapeDtypeStruct((n_seq, 1, D_MODEL), F32),
                     jax.ShapeDtypeStruct((n_seq, RG_CONV - 1, D_MODEL), F32)]
        out_specs = [row,
                     pl.BlockSpec((1, 1, D_MODEL), lambda i: (i // tiles_per_seq, 0, 0)),
                     pl.BlockSpec((1, RG_CONV - 1, D_MODEL), lambda i: (i // tiles_per_seq, 0, 0))]
        scratch = [pltpu.VMEM((HEADS, halo + tm, HD), F32),
                   pltpu.VMEM((tm, D_MODEL), F32),
                   pltpu.VMEM((HEADS, tm, HD), F32),
                   pltpu.VMEM((HEADS, 2, groups, HD), F32),
                   pltpu.VMEM((HEADS, groups, HD), F32),
                   pltpu.VMEM((HEADS, groups // SUBLANES, HD), F32),
                   pltpu.VMEM((tm, D_MODEL), BF16),
                   pltpu.VMEM((HEADS, halo, HD), F32),
                   pltpu.VMEM((HEADS, 1, HD), F32)]
    in_specs = [row] + [s for s, _ in specs_args]
    n_tiles = T // tm
    return pl.pallas_call(
        body,
        grid=(n_tiles,),
        in_specs=in_specs,
        out_specs=out_specs,
        out_shape=out_shape,
        scratch_shapes=scratch,
        compiler_params=_params(),
        name="rglru_state" if has_state else "rglru_prompt",
    )(*args)


def _split3(x):
    hi = x.astype(BF16)
    r = x - hi.astype(F32)
    mid = r.astype(BF16)
    lo = (r - mid.astype(F32)).astype(BF16)
    return hi, mid, lo


def _lower_bound(lower_ref, h, layer):
    rows = [lower_ref[n, h] for n in range(lower_ref.shape[0])]
    m = functools.reduce(jnp.maximum, rows)
    es = [jnp.exp(r - m) for r in rows]
    tot = functools.reduce(lambda p, q: p + q, es)
    sm = [e / tot for e in es]
    cs = functools.reduce(lambda p, q: p + q, sm[:layer + 1])
    return jnp.maximum(cs - sm[0], 0.0)


def _chunk_cumsum(cb_ref, tb_ref, eb_ref, tm, chunk):
    groups = tm // SUBLANES
    p = cb_ref[pl.ds(0, groups, stride=SUBLANES), :]
    for k in range(1, SUBLANES):
        p = p + cb_ref[pl.ds(k, groups, stride=SUBLANES), :]
        cb_ref[pl.ds(k, groups, stride=SUBLANES), :] = p
    if chunk > SUBLANES:
        gc = chunk // SUBLANES
        nch = tm // chunk
        tb_ref[...] = p
        e = jnp.zeros((nch, HD), F32)
        eb_ref[pl.ds(0, nch, stride=gc), :] = e
        for j in range(1, gc):
            e = e + tb_ref[pl.ds(j - 1, nch, stride=gc), :]
            eb_ref[pl.ds(j, nch, stride=gc), :] = e
        off = eb_ref[...]
        for k in range(SUBLANES):
            cb_ref[pl.ds(k, groups, stride=SUBLANES), :] = (
                cb_ref[pl.ds(k, groups, stride=SUBLANES), :] + off)
    return p


def _hgrn_front(p_ref, lower_ref, h, layer, tm, chunk,
                cb_ref, tb_ref, eb_ref, kk_ref, v_ref, sg_ref, qs_ref, qe_ref):
    lb = _lower_bound(lower_ref, h, layer)
    fg = lb + (1.0 - lb) * _sigmoid(p_ref[:, HD:2 * HD])
    cb_ref[...] = jnp.log(jnp.maximum(fg, F_FLOOR))
    kk_ref[...] = 1.0 - fg
    q = p_ref[:, 0:HD]
    qs = q * _sigmoid(q) * (HD ** -0.5)
    v_ref[...] = p_ref[:, 2 * HD:3 * HD]
    gg = p_ref[:, 3 * HD:4 * HD]
    sg_ref[...] = gg * _sigmoid(gg)
    totals = _chunk_cumsum(cb_ref, tb_ref, eb_ref, tm, chunk)
    cum = cb_ref[...]
    qs_ref[...] = qs
    qe_ref[...] = qs * jnp.exp(cum)
    safe = jnp.max(-cum) <= SAFE_DECAY
    return safe, totals


def _same_chunk_mask(chunk):
    ri = lax.broadcasted_iota(jnp.int32, (BLK, BLK), 0)
    ci = lax.broadcasted_iota(jnp.int32, (BLK, BLK), 1)
    lc = chunk.bit_length() - 1
    return ((ri >> lc) == (ci >> lc)) & (ci <= ri)


def _inverse_decayed_keys(kk, cum):
    return kk * jnp.exp(jnp.minimum(-cum, SAFE_DECAY))


def _robust_same_chunk_scores(rows, chunk, cb_ref, kk_ref, qs_ref):
    ri = lax.broadcasted_iota(jnp.int32, (BLK, BLK), 0)
    ci = lax.broadcasted_iota(jnp.int32, (BLK, BLK), 1)
    cum = cb_ref[rows, :]
    qs = qs_ref[rows, :]
    kk = kk_ref[rows, :]
    hi, mid, lo = _split3(cum)
    acc = jnp.where(ri == ci, _dot_nt(qs.astype(BF16), kk.astype(BF16)), 0.0)
    for lvl in range(chunk.bit_length() - 1):
        m = 1 << lvl
        pivot = ((ri >> (lvl + 1)) << (lvl + 1)) + (m - 1)
        sel = jnp.where(ci == pivot, 1.0, 0.0).astype(BF16)
        ref_cum = _dot(sel, hi) + _dot(sel, mid) + _dot(sel, lo)
        e = jnp.exp(-jnp.abs(cum - ref_cum))
        keep = (((ri >> (lvl + 1)) == (ci >> (lvl + 1)))
                & ((ri & (2 * m - 1)) >= m) & ((ci & (2 * m - 1)) < m))
        s = _dot_nt((qs * e).astype(BF16), (kk * e).astype(BF16))
        acc = acc + jnp.where(keep, s, 0.0)
    return acc


def _rms_gate(o, ng, sg):
    return o * lax.rsqrt(jnp.mean(o * o, axis=-1, keepdims=True) + RMS_EPS) * ng * sg


def _hgrn_prompt_kernel(x_ref, lower_ref, wi_ref, ng_ref, wo_ref, g_ref, b_ref, y_ref, so_ref,
                        xb_ref, p_ref, cb_ref, tb_ref, eb_ref, kk_ref, v_ref, sg_ref, qs_ref,
                        qe_ref, kd_ref, a_ref, oh_ref, ds_ref, ec_ref, ob_ref, s_ref,
                        *, tm, layer, tiles_per_seq, alpha):
    chunk = PROMPT_CHUNK
    assert BLK == 2 * chunk
    n_blocks = tm // BLK
    i = pl.program_id(0)

    @pl.when(i % tiles_per_seq == 0)
    def _():
        s_ref[...] = jnp.zeros_like(s_ref)

    xb_ref[...] = x_ref[...].astype(BF16)
    second = lax.broadcasted_iota(jnp.int32, (BLK, HD), 0) >= chunk
    ri = lax.broadcasted_iota(jnp.int32, (BLK, BLK), 0)
    ci = lax.broadcasted_iota(jnp.int32, (BLK, BLK), 1)
    cross = (ri >= chunk) & (ci < chunk)
    same = _same_chunk_mask(chunk)

    def project(h, slot):
        p_ref[slot] = _dot(xb_ref[...], wi_ref[h])

    def views(slot):
        return tuple(r.at[slot] for r in (cb_ref, tb_ref, eb_ref, kk_ref, v_ref, sg_ref, qs_ref,
                                          qe_ref, kd_ref, a_ref, oh_ref, ds_ref, ec_ref))

    def scores(h, slot):
        cb, tb, eb, kk_s, v_s, sg_s, qs_s, qe_s, kd_s, a_s, _, ds_s, ec_s = views(slot)
        safe, _ = _hgrn_front(p_ref.at[slot], lower_ref, h, layer, tm, chunk,
                              cb, tb, eb, kk_s, v_s, sg_s, qs_s, qe_s)
        for nb in range(n_blocks):
            rows = pl.ds(nb * BLK, BLK)
            cum = cb[rows, :]
            kk = kk_s[rows, :]
            last0 = cb[pl.ds(nb * BLK + chunk - 1, 1), :]
            last1 = cb[pl.ds(nb * BLK + BLK - 1, 1), :]
            kd = kk * jnp.exp(jnp.where(second, last1, last0) - cum)
            kd_s[rows, :] = kd
            keys = jnp.concatenate([_inverse_decayed_keys(kk, cum), kd], axis=0).astype(BF16)
            s2 = _dot_nt(qe_s[rows, :].astype(BF16), keys)
            a = jnp.where(same, s2[:, 0:BLK], 0.0) + jnp.where(cross, s2[:, BLK:2 * BLK], 0.0)
            a_s[nb] = a.astype(BF16)
            k_blk = kd * jnp.where(second, 1.0, jnp.exp(last1))
            ds_s[nb] = _dot(k_blk.T.astype(BF16), v_s[rows, :].astype(BF16))
            ec_s[nb] = jnp.broadcast_to(jnp.exp(last0 + last1), (HD, HD)).T
        return safe

    def fix_scores(safe, slot):
        cb, _, _, kk_s, _, _, qs_s, qe_s, kd_s, a_s, _, _, _ = views(slot)

        @pl.when(jnp.logical_not(safe))
        def _():
            def fix(nb, _):
                rows = pl.ds(pl.multiple_of(nb * BLK, BLK), BLK)
                s1 = _dot_nt(qe_s[rows, :].astype(BF16), kd_s[rows, :].astype(BF16))
                a = _robust_same_chunk_scores(rows, chunk, cb, kk_s, qs_s)
                a_s[nb] = (a + jnp.where(cross, s1, 0.0)).astype(BF16)
                return 0
            lax.fori_loop(0, n_blocks, fix, 0)

    def outputs(h, slot):
        cb, _, _, _, v_s, sg_s, _, qe_s, _, a_s, oh_s, ds_s, ec_s = views(slot)
        states = [s_ref[h]]
        for nb in range(n_blocks):
            states.append(ec_s[nb] * states[nb] + ds_s[nb])
        s_ref[h] = states[n_blocks]
        for nb in range(n_blocks):
            rows = pl.ds(nb * BLK, BLK)
            last0 = cb[pl.ds(nb * BLK + chunk - 1, 1), :]
            q_blk = qe_s[rows, :] * jnp.where(second, jnp.exp(last0), 1.0)
            lhs = jnp.concatenate([a_s[nb], q_blk.astype(BF16)], axis=1)
            rhs = jnp.concatenate([v_s[rows, :].astype(BF16), states[nb].astype(BF16)], axis=0)
            oh_s[rows, :] = _dot(lhs, rhs)
        ob_ref[h] = _rms_gate(oh_s[...], ng_ref[...], sg_s[...]).astype(BF16)

    project(0, 0)

    def pair(hh, _):
        h0 = 2 * hh
        project(h0 + 1, 1)
        safe0 = scores(h0, 0)
        project(jnp.minimum(h0 + 2, HEADS - 1), 0)
        safe1 = scores(h0 + 1, 1)
        fix_scores(safe0, 0)
        fix_scores(safe1, 1)
        outputs(h0, 0)
        outputs(h0 + 1, 1)
        return 0

    lax.fori_loop(0, HEADS // 2, pair, 0)
    for h in range(HEADS):
        xb_ref[:, h * HD:(h + 1) * HD] = ob_ref[h]
    f = _dot(xb_ref[...], wo_ref[...])
    y_ref[...] = _deepnorm(x_ref[...], f, g_ref[...], b_ref[...], alpha)

    @pl.when(i % tiles_per_seq == tiles_per_seq - 1)
    def _():
        so_ref[0] = s_ref[...]


def _hgrn_prep(hg_lower, wi, ng, wo):
    n_layers = hg_lower.shape[0]
    lower = hg_lower.reshape(n_layers, HEADS, 1, HD)
    wih = (wi.reshape(n_layers, D_MODEL, 4, HEADS, HD).transpose(0, 3, 1, 2, 4)
           .reshape(n_layers, HEADS, D_MODEL, 4 * HD))
    return lower, wih.astype(BF16), ng.reshape(n_layers, 1, HD), wo.astype(BF16)


def _hgrn_prompt_call(x, prep, layer, g, b, *, seq_len, alpha):
    lower, wih, ng, wo = prep
    T = x.shape[0]
    tm = TM_PROMPT
    tiles_per_seq = seq_len // tm
    n_seq = T // seq_len
    row = pl.BlockSpec((tm, D_MODEL), lambda i: (i, 0))
    specs_args = [_layer_operand(w) for w in (lower, (wih, layer), (ng, layer), (wo, layer), g, b)]
    head_buf = pltpu.VMEM((2, tm, HD), F32)
    scratch = [pltpu.VMEM((tm, D_MODEL), BF16),
               pltpu.VMEM((2, tm, 4 * HD), F32),
               head_buf,
               pltpu.VMEM((2, tm // SUBLANES, HD), F32),
               pltpu.VMEM((2, tm // SUBLANES, HD), F32),
               head_buf, head_buf, head_buf, head_buf, head_buf,
               head_buf,
               pltpu.VMEM((2, tm // BLK, BLK, BLK), BF16),
               head_buf,
               pltpu.VMEM((2, tm // BLK, HD, HD), F32),
               pltpu.VMEM((2, tm // BLK, HD, HD), F32),
               pltpu.VMEM((HEADS, tm, HD), BF16),
               pltpu.VMEM((HEADS, HD, HD), F32)]
    y, so = pl.pallas_call(
        functools.partial(_hgrn_prompt_kernel, tm=tm, layer=layer,
                          tiles_per_seq=tiles_per_seq, alpha=alpha),
        grid=(T // tm,),
        in_specs=[row] + [s for s, _ in specs_args],
        out_specs=[row, pl.BlockSpec((1, HEADS, HD, HD), lambda i: (i // tiles_per_seq, 0, 0, 0))],
        out_shape=[jax.ShapeDtypeStruct((T, D_MODEL), F32),
                   jax.ShapeDtypeStruct((n_seq, HEADS, HD, HD), F32)],
        scratch_shapes=scratch,
        compiler_params=_params(),
        name="hgrn_prompt",
    )(x, *[a for _, a in specs_args])
    return y, so


def _hgrn_sample_front_kernel(x_ref, lower_ref, wi_ref,
                              oi_ref, qe_ref, v_ref, sg_ref, kdt_ref, et_ref,
                              xb_ref, p_ref, cb_ref, kk_ref, qs_ref, a_ref, last_ref,
                              *, tm, chunk, layer):
    assert chunk == SUBLANES
    xb_ref[...] = x_ref[...].astype(BF16)
    groups = tm // SUBLANES
    n_blocks = tm // BLK
    same = _same_chunk_mask(chunk)

    def head(h, _):
        p_ref[...] = _dot(xb_ref[...], wi_ref[h])
        safe, totals = _hgrn_front(p_ref, lower_ref, h, layer, tm, chunk,
                                   cb_ref, None, None, kk_ref, v_ref.at[h], sg_ref.at[h],
                                   qs_ref, qe_ref.at[h])
        for k in range(SUBLANES):
            last_ref[pl.ds(k, groups, stride=SUBLANES), :] = totals
        for nb in range(n_blocks):
            r0 = nb * BLK
            rows = pl.ds(r0, BLK)
            cum = cb_ref[rows, :]
            kk = kk_ref[rows, :]
            s = _dot_nt(qe_ref[h, rows, :].astype(BF16), _inverse_decayed_keys(kk, cum).astype(BF16))
            a_ref[nb] = jnp.where(same, s, 0.0).astype(BF16)
            last = last_ref[rows, :]
            kdt_ref[h, :, r0:r0 + BLK] = (kk * jnp.exp(last - cum)).T
            et_ref[h, :, r0:r0 + BLK] = jnp.exp(last).T

        @pl.when(jnp.logical_not(safe))
        def _():
            def fix(nb, _):
                rows = pl.ds(pl.multiple_of(nb * BLK, BLK), BLK)
                a_ref[nb] = _robust_same_chunk_scores(rows, chunk, cb_ref, kk_ref, qs_ref).astype(BF16)
                return 0
            lax.fori_loop(0, n_blocks, fix, 0)

        for nb in range(n_blocks):
            rows = pl.ds(nb * BLK, BLK)
            oi_ref[h, rows, :] = _dot(a_ref[nb], v_ref[h, rows, :].astype(BF16))
        return 0

    lax.fori_loop(0, HEADS, head, 0)


def _hgrn_sample_state_kernel(qe_ref, oi_ref, v_ref, sg_ref, kdt_ref, et_ref, s0_ref, ng_ref,
                              *rest, chunk):
    op_ref, so_ref = rest[-2:]
    row_seq = lax.broadcasted_iota(jnp.int32, (BLK, HD), 0) >> (chunk.bit_length() - 1)

    def head(h, _):
        kdt = kdt_ref[h].astype(BF16)
        et = et_ref[h]
        v = v_ref[h]
        for sq in range(BLK // chunk):
            rows = slice(sq * chunk, (sq + 1) * chunk)
            s0 = s0_ref[sq, h]
            o = oi_ref[h, rows, :] + _dot(qe_ref[h, rows, :].astype(BF16), s0.astype(BF16))
            op_ref[h, rows, :] = _rms_gate(o, ng_ref[...], sg_ref[h, rows, :])
            vm = jnp.where(row_seq == sq, v, 0.0).astype(BF16)
            e_col = jnp.broadcast_to(et[:, sq * chunk:sq * chunk + 1], (HD, HD))
            so_ref[sq, h] = e_col * s0 + _dot(kdt, vm)
        return 0

    lax.fori_loop(0, HEADS, head, 0)


def _hgrn_sample_out_kernel(x_ref, op_ref, wo_ref, g_ref, b_ref, y_ref, ob_ref, *, alpha):
    for h in range(HEADS):
        ob_ref[:, h * HD:(h + 1) * HD] = op_ref[h].astype(BF16)
    f = _dot(ob_ref[...], wo_ref[...])
    y_ref[...] = _deepnorm(x_ref[...], f, g_ref[...], b_ref[...], alpha)


def _hgrn_sample_call(x, s0, so_prev, prep, layer, g, b, *, seq_len, alpha):
    lower, wih, ng, wo = prep
    T = x.shape[0]
    chunk = seq_len
    heads_rows = jax.ShapeDtypeStruct((HEADS, T, HD), F32)
    heads_cols = jax.ShapeDtypeStruct((HEADS, HD, T), F32)
    full = lambda s: pl.BlockSpec(s, lambda i: (0,) * len(s))
    head_buf = pltpu.VMEM((T, HD), F32)
    specs_args = [_layer_operand(w) for w in (lower, (wih, layer))]
    oi, qe, v, sg, kdt, et = pl.pallas_call(
        functools.partial(_hgrn_sample_front_kernel, tm=T, chunk=chunk, layer=layer),
        grid=(1,),
        in_specs=[full(x.shape)] + [s for s, _ in specs_args],
        out_specs=[full(heads_rows.shape)] * 4 + [full(heads_cols.shape)] * 2,
        out_shape=[heads_rows] * 4 + [heads_cols] * 2,
        scratch_shapes=[pltpu.VMEM((T, D_MODEL), BF16), pltpu.VMEM((T, 4 * HD), F32),
                        head_buf, head_buf, head_buf,
                        pltpu.VMEM((T // BLK, BLK, BLK), BF16), head_buf],
        compiler_params=_params(),
        name="hgrn_sample_front",
    )(x, *[a for _, a in specs_args])
    seqs = BLK // chunk
    by_rows = pl.BlockSpec((HEADS, BLK, HD), lambda i: (0, i, 0))
    by_cols = pl.BlockSpec((HEADS, HD, BLK), lambda i: (0, 0, i))
    state = pl.BlockSpec((None, seqs, HEADS, HD, HD), lambda i: (layer, i, 0, 0, 0))
    ng_spec, ng_arr = _layer_operand((ng, layer))
    in_specs = [by_rows, by_rows, by_rows, by_rows, by_cols, by_cols, state, ng_spec]
    args = [qe, oi, v, sg, kdt, et, s0, ng_arr]
    aliases = {}
    if so_prev is not None:
        in_specs.append(pl.BlockSpec(memory_space=pl.ANY))
        args.append(so_prev)
        aliases = {len(args) - 1: 1}
    op, so = pl.pallas_call(
        functools.partial(_hgrn_sample_state_kernel, chunk=chunk),
        grid=(T // BLK,),
        in_specs=in_specs,
        out_specs=[by_rows, state],
        out_shape=[heads_rows, jax.ShapeDtypeStruct(s0.shape, F32)],
        input_output_aliases=aliases,
        compiler_params=_params(),
        name="hgrn_sample_state",
    )(*args)
    specs_args = [_layer_operand(w) for w in ((wo, layer), g, b)]
    y = pl.pallas_call(
        functools.partial(_hgrn_sample_out_kernel, alpha=alpha),
        grid=(1,),
        in_specs=[full(x.shape), full(op.shape)] + [s for s, _ in specs_args],
        out_specs=full(x.shape),
        out_shape=jax.ShapeDtypeStruct(x.shape, F32),
        scratch_shapes=[pltpu.VMEM((T, D_MODEL), BF16)],
        compiler_params=_params(),
        name="hgrn_sample_out",
    )(x, op, *[a for _, a in specs_args])
    return y, so


def _to_time_major(a):
    return jnp.transpose(a, (1, 0, 2)).reshape(-1, a.shape[-1])


def _from_time_major(a, batch):
    return jnp.transpose(a.reshape(-1, batch, a.shape[-1]), (1, 0, 2))


def kernel(x_prompt, x_sample, state_rglru_h, state_rglru_conv, state_hgrn_s, state_ffn_conv,
           ln_g, ln_b, rg_w_in, rg_conv_w, rg_conv_b, rg_gate_w, rg_gate_b, rg_lambda, rg_w_out,
           hg_lower, hg_w_in, hg_norm_g, hg_w_out, ffn_w_in, ffn_conv_w, ffn_conv_b, ffn_w_out):
    depth = ln_g.shape[0]
    alpha = (2.0 * depth) ** 0.25
    pb, pl_len, _ = x_prompt.shape
    sb, sl_len, _ = x_sample.shape

    n_rg, n_ffn = rg_w_in.shape[0], ffn_w_in.shape[0]
    lng = ln_g.reshape(depth * 2, 1, D_MODEL)
    lnb = ln_b.reshape(depth * 2, 1, D_MODEL)
    rg_wi = _rglru_in_proj(rg_w_in)
    rg_wg = jnp.concatenate([rg_gate_w[:, 0], rg_gate_w[:, 1]], axis=-1).astype(BF16)
    rg_wo = rg_w_out.astype(BF16)
    rg_cb = rg_conv_b.reshape(n_rg, 1, D_MODEL)
    rg_lam = rg_lambda.reshape(n_rg, 1, D_MODEL)
    hg = _hgrn_prep(hg_lower, hg_w_in, hg_norm_g, hg_w_out)
    ffn_wi = ffn_w_in.astype(BF16)
    ffn_wo = ffn_w_out.astype(BF16)
    ffn_cb = ffn_conv_b.reshape(n_ffn, 1, D_FF)
    rg_cst = jnp.transpose(state_rglru_conv, (0, 2, 1, 3)).reshape(n_rg, -1, D_MODEL)
    ffn_cst = jnp.transpose(state_ffn_conv, (0, 2, 1, 3)).reshape(n_ffn, -1, D_FF)

    xp = x_prompt.reshape(pb * pl_len, D_MODEL)
    xs = _to_time_major(x_sample)
    p_h, p_rc, p_s, p_fc = [], [], [], []
    s_h, s_rc, s_fc = [], [], []
    s_s = None
    for i in range(depth):
        j = i // 2
        g0, b0 = (lng, 2 * i), (lnb, 2 * i)
        if i % 2 == 0:
            w = ((rg_wi, j), (rg_conv_w, j), (rg_cb, j), (rg_wg, j), (rg_gate_b, j), (rg_lam, j),
                 (rg_wo, j), g0, b0)
            xp, ho, co = _rglru_call(xp, None, None, *w, seq_len=pl_len, alpha=alpha)
            p_h.append(ho[:, 0])
            p_rc.append(co)
            xs, ho, co = _rglru_call(xs, (state_rglru_h, j), (rg_cst, j), *w,
                                     seq_len=sl_len, alpha=alpha)
            s_h.append(ho)
            s_rc.append(co)
        else:
            xp, so = _hgrn_prompt_call(xp, hg, j, g0, b0, seq_len=pl_len, alpha=alpha)
            p_s.append(so)
            xs_bm = _from_time_major(xs, sb).reshape(sb * sl_len, D_MODEL)
            ys_bm, s_s = _hgrn_sample_call(xs_bm, state_hgrn_s, s_s, hg, j, g0, b0,
                                           seq_len=sl_len, alpha=alpha)
            xs = _to_time_major(ys_bm.reshape(sb, sl_len, D_MODEL))
        w = ((ffn_wi, i), (ffn_conv_w, i), (ffn_cb, i), (ffn_wo, i), (lng, 2 * i + 1), (lnb, 2 * i + 1))
        xp, fo = _ffn_call(xp, None, *w, seq_len=pl_len, alpha=alpha)
        p_fc.append(fo)
        xs, fo = _ffn_call(xs, (ffn_cst, i), *w, seq_len=sl_len, alpha=alpha)
        s_fc.append(fo)

    def stacked_from_time_major(parts):
        a = jnp.stack(parts)
        a = a.reshape(a.shape[0], -1, sb, a.shape[-1])
        return jnp.transpose(a, (0, 2, 1, 3))

    return (xp.reshape(x_prompt.shape), _from_time_major(xs, sb),
            jnp.stack(p_h), jnp.stack(p_rc), jnp.stack(p_s), jnp.stack(p_fc),
            jnp.stack(s_h), stacked_from_time_major(s_rc), s_s, stacked_from_time_major(s_fc))
```

```python
import functools

import jax
import jax.numpy as jnp
from jax import lax
from jax.experimental import pallas as pl
from jax.experimental.pallas import tpu as pltpu

F32 = jnp.float32
BF16 = jnp.bfloat16

D_MODEL = 1024
D_FF = 2816
HEADS = 8
HD = 128
RG_CONV = 4
FFN_CONV = 3
RG_C = 8.0
LN_EPS = 1e-5
RMS_EPS = 1e-6
F_FLOOR = 1e-30
SUBLANES = 8
LANES = 128
BLK = 128
PROMPT_CHUNK = 64
SAFE_DECAY = 80.0
FF_TILE = 256
TM_PROMPT = 512
NORM_SLAB = 256
VMEM_LIMIT = 56 * 1024 * 1024


def _dot(a, b):
    return jnp.dot(a, b, preferred_element_type=F32)


def _dot_nt(a, b):
    return lax.dot_general(a, b, (((1,), (1,)), ((), ())), preferred_element_type=F32)


def _sigmoid(x):
    return 0.5 * jnp.tanh(0.5 * x) + 0.5


def _resident(shape):
    nd = len(shape)
    return pl.BlockSpec(shape, lambda i: (0,) * nd, pipeline_mode=pl.Buffered(1))


def _layer_shape(w):
    return w[0].shape[1:] if isinstance(w, tuple) else w.shape


def _layer_operand(w):
    if not isinstance(w, tuple):
        return _resident(w.shape), w
    arr, layer = w
    tail = (0,) * (arr.ndim - 1)
    spec = pl.BlockSpec((None,) + arr.shape[1:], lambda i: (layer,) + tail,
                        pipeline_mode=pl.Buffered(1))
    return spec, arr


def _deepnorm(x, f, g, b, alpha):
    z = alpha * x + f
    mu = jnp.mean(z, axis=-1, keepdims=True)
    zc = z - mu
    var = jnp.mean(zc * zc, axis=-1, keepdims=True)
    return zc * lax.rsqrt(var + LN_EPS) * g + b


def _project_and_norm(x_ref, lhs_ref, wo_ref, g_ref, b_ref, y_ref, alpha, slab):
    rows = x_ref.shape[0]
    slab = min(slab, rows)
    for r0 in range(0, rows, slab):
        sl = pl.ds(r0, slab)
        f = _dot(lhs_ref[sl, :], wo_ref[...])
        y_ref[sl, :] = _deepnorm(x_ref[sl, :], f, g_ref[...], b_ref[...], alpha)


def _params(n_axes=1):
    return pltpu.CompilerParams(dimension_semantics=("arbitrary",) * n_axes,
                                vmem_limit_bytes=VMEM_LIMIT)


def _ffn_kernel(*refs, tm, step, tiles_per_seq, has_state, alpha):
    if has_state:
        (x_ref, st_ref, wi_ref, cw_ref, cb_ref, wo_ref, g_ref, b_ref,
         y_ref, so_ref, gbuf, hbuf) = refs
        carry = None
    else:
        (x_ref, wi_ref, cw_ref, cb_ref, wo_ref, g_ref, b_ref,
         y_ref, so_ref, gbuf, hbuf, carry) = refs
    halo = gbuf.shape[0] - tm
    xb = x_ref[...].astype(BF16)
    if not has_state:
        @pl.when(pl.program_id(0) % tiles_per_seq == 0)
        def _():
            carry[...] = jnp.zeros_like(carry)
    for j in range(D_FF // FF_TILE):
        c0 = j * FF_TILE
        g = _dot(xb, wi_ref[:, c0:c0 + FF_TILE])
        u = _dot(xb, wi_ref[:, D_FF + c0:D_FF + c0 + FF_TILE])
        if has_state:
            gbuf[0:halo, :] = st_ref[:, c0:c0 + FF_TILE]
        else:
            gbuf[0:halo, :] = carry[:, c0:c0 + FF_TILE]
        gbuf[halo:halo + tm, :] = g
        g1 = gbuf[halo - step:halo - step + tm, :]
        g2 = gbuf[halo - 2 * step:halo - 2 * step + tm, :]
        cw = cw_ref[:, c0:c0 + FF_TILE]
        gc = g * cw[2:3] + g1 * cw[1:2] + g2 * cw[0:1] + cb_ref[:, c0:c0 + FF_TILE]
        hbuf[:, c0:c0 + FF_TILE] = (jax.nn.gelu(gc) * u).astype(BF16)
        if has_state:
            so_ref[:, c0:c0 + FF_TILE] = gbuf[tm:tm + halo, :]
        else:
            carry[:, c0:c0 + FF_TILE] = gbuf[tm:tm + halo, :]
            so_ref[0, :, c0:c0 + FF_TILE] = gbuf[halo + tm - 2:halo + tm, :]
    _project_and_norm(x_ref, hbuf, wo_ref, g_ref, b_ref, y_ref, alpha, NORM_SLAB)


def _ffn_call(x, state_tm, wi, cw, cb, wo, g, b, *, seq_len, alpha):
    T = x.shape[0]
    has_state = state_tm is not None
    if has_state:
        st_shape = _layer_shape(state_tm)
        tm, step, halo, tiles_per_seq = T, T // seq_len, st_shape[0], 1
    else:
        tm, step, halo, tiles_per_seq = TM_PROMPT, 1, SUBLANES, seq_len // TM_PROMPT
    n_tiles = T // tm
    n_seq = T // seq_len
    row = pl.BlockSpec((tm, D_MODEL), lambda i: (i, 0))
    operands = ([state_tm] if has_state else []) + [wi, cw, cb, wo, g, b]
    specs_args = [_layer_operand(w) for w in operands]
    in_specs = [row] + [s for s, _ in specs_args]
    args = [x] + [a for _, a in specs_args]
    scratch = [pltpu.VMEM((halo + tm, FF_TILE), F32), pltpu.VMEM((tm, D_FF), BF16)]
    if has_state:
        so_shape = jax.ShapeDtypeStruct(st_shape, F32)
        so_spec = pl.BlockSpec(st_shape, lambda i: (0, 0))
    else:
        so_shape = jax.ShapeDtypeStruct((n_seq, FFN_CONV - 1, D_FF), F32)
        so_spec = pl.BlockSpec((1, FFN_CONV - 1, D_FF), lambda i: (i // tiles_per_seq, 0, 0))
        scratch.append(pltpu.VMEM((halo, D_FF), F32))
    return pl.pallas_call(
        functools.partial(_ffn_kernel, tm=tm, step=step, tiles_per_seq=tiles_per_seq,
                          has_state=has_state, alpha=alpha),
        grid=(n_tiles,),
        in_specs=in_specs,
        out_specs=[row, so_spec],
        out_shape=[jax.ShapeDtypeStruct((T, D_MODEL), F32), so_shape],
        scratch_shapes=scratch,
        compiler_params=_params(),
        name="ffn_state" if has_state else "ffn_prompt",
    )(*args)


def _rglru_gates(xc, gt, h, bg_ref, log_sig_lam):
    c0 = h * HD
    bg = bg_ref[:, c0:c0 + HD]
    r = _sigmoid(gt[:, 0:HD] + bg[0:1])
    ig = _sigmoid(gt[:, HD:2 * HD] + bg[1:2])
    a = jnp.exp(RG_C * r * log_sig_lam[:, c0:c0 + HD])
    return a, jnp.sqrt(jnp.maximum(1.0 - a * a, 0.0)) * ig * xc


def _rglru_state_kernel(x_ref, h0_ref, cst_ref, wi_ref, cw_ref, cb_ref, wg_ref, bg_ref, lam_ref,
                        wo_ref, g_ref, b_ref, y_ref, ho_ref, co_ref, xbuf, ybuf, *, tm, step, alpha):
    halo = xbuf.shape[0] - tm
    xb = x_ref[...].astype(BF16)
    xbuf[0:halo, :] = cst_ref[...]
    log_sig_lam = jax.nn.log_sigmoid(lam_ref[...])
    for h in range(HEADS):
        c0 = h * HD
        proj = _dot(xb, wi_ref[:, 2 * c0:2 * c0 + 2 * HD])
        gate = jax.nn.gelu(proj[:, 0:HD])
        xbuf[halo:halo + tm, c0:c0 + HD] = proj[:, HD:2 * HD]
        cw = cw_ref[:, c0:c0 + HD]
        xc = cb_ref[:, c0:c0 + HD] + proj[:, HD:2 * HD] * cw[RG_CONV - 1:RG_CONV]
        for j in range(RG_CONV - 1):
            back = (RG_CONV - 1 - j) * step
            xc = xc + xbuf[halo - back:halo - back + tm, c0:c0 + HD] * cw[j:j + 1]
        a, bv = _rglru_gates(xc, _dot(xc.astype(BF16), wg_ref[h]), h, bg_ref, log_sig_lam)
        hh = h0_ref[:, c0:c0 + HD]
        for t in range(tm // step):
            rows = slice(t * step, (t + 1) * step)
            hh = a[rows] * hh + bv[rows]
            ybuf[rows, c0:c0 + HD] = (hh * gate[rows]).astype(BF16)
        ho_ref[:, c0:c0 + HD] = hh
    co_ref[...] = xbuf[tm:tm + halo, :]
    _project_and_norm(x_ref, ybuf, wo_ref, g_ref, b_ref, y_ref, alpha, NORM_SLAB)


def _rglru_prompt_kernel(x_ref, wi_ref, cw_ref, cb_ref, wg_ref, bg_ref, lam_ref, wo_ref, g_ref, b_ref,
                         y_ref, ho_ref, co_ref,
                         xbuf, gbuf, hbuf, tot, hin, sup, ybuf, ccarry, hcarry,
                         *, tm, tiles_per_seq, alpha):
    halo = xbuf.shape[1] - tm
    groups = tm // SUBLANES
    supers = groups // SUBLANES
    taps = RG_CONV - 1

    @pl.when(pl.program_id(0) % tiles_per_seq == 0)
    def _():
        ccarry[...] = jnp.zeros_like(ccarry)
        hcarry[...] = jnp.zeros_like(hcarry)

    xb = x_ref[...].astype(BF16)
    log_sig_lam = jax.nn.log_sigmoid(lam_ref[...])

    def project(h):
        c0 = h * HD
        proj = _dot(xb, wi_ref[:, 2 * c0:2 * c0 + 2 * HD])
        gbuf[:, c0:c0 + HD] = jax.nn.gelu(proj[:, 0:HD])
        xs = xbuf.at[h]
        xs[0:halo, :] = ccarry[h]
        xs[halo:halo + tm, :] = proj[:, HD:2 * HD]
        ccarry[h] = xs[tm:tm + halo, :]
        co_ref[0, :, c0:c0 + HD] = xs[halo + tm - taps:halo + tm, :]

    def conv(h):
        c0 = h * HD
        xs = xbuf.at[h]
        cw = cw_ref[:, c0:c0 + HD]
        cb = cb_ref[:, c0:c0 + HD]
        lock = [xs[pl.ds(halo - taps + m, groups, stride=SUBLANES), :]
                for m in range(SUBLANES + taps)]
        xc = []
        for k in range(SUBLANES):
            acc = cb + lock[k + taps] * cw[taps:taps + 1]
            for j in range(taps):
                acc = acc + lock[k + j] * cw[j:j + 1]
            xc.append(acc)
        xc = jnp.concatenate(xc, axis=0)
        return xc, _dot(xc.astype(BF16), wg_ref[h])

    def scan(h, xc, gt):
        c0 = h * HD
        a, bv = _rglru_gates(xc, gt, h, bg_ref, log_sig_lam)
        pa, pb = a[0:groups], bv[0:groups]
        a_in, b_in = [pa], [pb]
        for k in range(1, SUBLANES):
            ak = a[k * groups:(k + 1) * groups]
            pb = ak * pb + bv[k * groups:(k + 1) * groups]
            pa = ak * pa
            a_in.append(pa)
            b_in.append(pb)
        tot[h, 0] = pa
        tot[h, 1] = pb
        qa = tot[h, 0, pl.ds(0, supers, stride=SUBLANES), :]
        qb = tot[h, 1, pl.ds(0, supers, stride=SUBLANES), :]
        a_sup, b_sup = [qa], [qb]
        for j in range(1, SUBLANES):
            aj = tot[h, 0, pl.ds(j, supers, stride=SUBLANES), :]
            qb = aj * qb + tot[h, 1, pl.ds(j, supers, stride=SUBLANES), :]
            qa = aj * qa
            a_sup.append(qa)
            b_sup.append(qb)
        hc = hcarry[h]
        for s in range(supers):
            sup[h, pl.ds(s, 1), :] = hc
            hc = qa[s:s + 1] * hc + qb[s:s + 1]
        hcarry[h] = hc
        ho_ref[0, :, c0:c0 + HD] = hc
        h_sup = sup[h]
        hin[h, pl.ds(0, supers, stride=SUBLANES), :] = h_sup
        for j in range(1, SUBLANES):
            hin[h, pl.ds(j, supers, stride=SUBLANES), :] = a_sup[j - 1] * h_sup + b_sup[j - 1]
        h_grp = hin[h]
        for k in range(SUBLANES):
            hbuf[h, pl.ds(k, groups, stride=SUBLANES), :] = a_in[k] * h_grp + b_in[k]

    convs = {}
    for h in range(HEADS + 2):
        if h < HEADS:
            project(h)
        if 1 <= h <= HEADS:
            convs[h - 1] = conv(h - 1)
        if h >= 2:
            scan(h - 2, *convs.pop(h - 2))
    for h in range(HEADS):
        c0 = h * HD
        ybuf[:, c0:c0 + HD] = (hbuf[h] * gbuf[:, c0:c0 + HD]).astype(BF16)
    _project_and_norm(x_ref, ybuf, wo_ref, g_ref, b_ref, y_ref, alpha, NORM_SLAB)


def _rglru_in_proj(w_in):
    lead = w_in.shape[:-1]
    w = w_in.reshape(*lead, 2, HEADS, HD)
    return jnp.swapaxes(w, -3, -2).reshape(*lead, 2 * HEADS * HD).astype(BF16)


def _rglru_call(x, h0, cst_tm, wi, cw, cb, wg, bg, lam, wo, g, b, *, seq_len, alpha):
    T = x.shape[0]
    has_state = h0 is not None
    n_seq = T // seq_len
    operands = ([h0, cst_tm] if has_state else []) + [wi, cw, cb, wg, bg, lam, wo, g, b]
    specs_args = [_layer_operand(w) for w in operands]
    args = [x] + [a for _, a in specs_args]
    if has_state:
        h0_shape, cst_shape = _layer_shape(h0), _layer_shape(cst_tm)
        tm, halo = T, cst_shape[0]
        body = functools.partial(_rglru_state_kernel, tm=tm, step=T // seq_len, alpha=alpha)
        row = pl.BlockSpec((tm, D_MODEL), lambda i: (i, 0))
        out_shape = [jax.ShapeDtypeStruct((T, D_MODEL), F32),
                     jax.ShapeDtypeStruct(h0_shape, F32),
                     jax.ShapeDtypeStruct(cst_shape, F32)]
        out_specs = [row, pl.BlockSpec(h0_shape, lambda i: (0, 0)),
                     pl.BlockSpec(cst_shape, lambda i: (0, 0))]
        scratch = [pltpu.VMEM((halo + tm, D_MODEL), F32), pltpu.VMEM((tm, D_MODEL), BF16)]
    else:
        tm, halo, tiles_per_seq = TM_PROMPT, SUBLANES, seq_len // TM_PROMPT
        groups = tm // SUBLANES
        body = functools.partial(_rglru_prompt_kernel, tm=tm, tiles_per_seq=tiles_per_seq, alpha=alpha)
        row = pl.BlockSpec((tm, D_MODEL), lambda i: (i, 0))
        out_shape = [jax.ShapeDtypeStruct((T, D_MODEL), F32),
                     jax.ShapeDtypeStruct((n_seq, 1, D_MODEL), F32),
                     jax.ShapeDtypeStruct((n_seq, RG_CONV - 1, D_MODEL), F32)]
        out_specs = [row,
                     pl.BlockSpec((1, 1, D_MODEL), lambda i: (i // tiles_per_seq, 0, 0)),
                     pl.BlockSpec((1, RG_CONV - 1, D_MODEL), lambda i: (i // tiles_per_seq, 0, 0))]
        scratch = [pltpu.VMEM((HEADS, halo + tm, HD), F32),
                   pltpu.VMEM((tm, D_MODEL), F32),
                   pltpu.VMEM((HEADS, tm, HD), F32),
                   pltpu.VMEM((HEADS, 2, groups, HD), F32),
                   pltpu.VMEM((HEADS, groups, HD), F32),
                   pltpu.VMEM((HEADS, groups // SUBLANES, HD), F32),
                   pltpu.VMEM((tm, D_MODEL), BF16),
                   pltpu.VMEM((HEADS, halo, HD), F32),
                   pltpu.VMEM((HEADS, 1, HD), F32)]
    in_specs = [row] + [s for s, _ in specs_args]
    n_tiles = T // tm
    return pl.pallas_call(
        body,
        grid=(n_tiles,),
        in_specs=in_specs,
        out_specs=out_specs,
        out_shape=out_shape,
        scratch_shapes=scratch,
        compiler_params=_params(),
        name="rglru_state" if has_state else "rglru_prompt",
    )(*args)


def _split3(x):
    hi = x.astype(BF16)
    r = x - hi.astype(F32)
    mid = r.astype(BF16)
    lo = (r - mid.astype(F32)).astype(BF16)
    return hi, mid, lo


def _lower_bound(lower_ref, h, layer):
    rows = [lower_ref[n, h] for n in range(lower_ref.shape[0])]
    m = functools.reduce(jnp.maximum, rows)
    es = [jnp.exp(r - m) for r in rows]
    tot = functools.reduce(lambda p, q: p + q, es)
    sm = [e / tot for e in es]
    cs = functools.reduce(lambda p, q: p + q, sm[:layer + 1])
    return jnp.maximum(cs - sm[0], 0.0)


def _chunk_cumsum(cb_ref, tb_ref, eb_ref, tm, chunk):
    groups = tm // SUBLANES
    p = cb_ref[pl.ds(0, groups, stride=SUBLANES), :]
    for k in range(1, SUBLANES):
        p = p + cb_ref[pl.ds(k, groups, stride=SUBLANES), :]
        cb_ref[pl.ds(k, groups, stride=SUBLANES), :] = p
    if chunk > SUBLANES:
        gc = chunk // SUBLANES
        nch = tm // chunk
        tb_ref[...] = p
        e = jnp.zeros((nch, HD), F32)
        eb_ref[pl.ds(0, nch, stride=gc), :] = e
        for j in range(1, gc):
            e = e + tb_ref[pl.ds(j - 1, nch, stride=gc), :]
            eb_ref[pl.ds(j, nch, stride=gc), :] = e
        off = eb_ref[...]
        for k in range(SUBLANES):
            cb_ref[pl.ds(k, groups, stride=SUBLANES), :] = (
                cb_ref[pl.ds(k, groups, stride=SUBLANES), :] + off)
    return p


def _hgrn_front(p_ref, lower_ref, h, layer, tm, chunk,
                cb_ref, tb_ref, eb_ref, kk_ref, v_ref, sg_ref, qs_ref, qe_ref):
    lb = _lower_bound(lower_ref, h, layer)
    fg = lb + (1.0 - lb) * _sigmoid(p_ref[:, HD:2 * HD])
    cb_ref[...] = jnp.log(jnp.maximum(fg, F_FLOOR))
    kk_ref[...] = 1.0 - fg
    q = p_ref[:, 0:HD]
    qs = q * _sigmoid(q) * (HD ** -0.5)
    v_ref[...] = p_ref[:, 2 * HD:3 * HD]
    gg = p_ref[:, 3 * HD:4 * HD]
    sg_ref[...] = gg * _sigmoid(gg)
    totals = _chunk_cumsum(cb_ref, tb_ref, eb_ref, tm, chunk)
    cum = cb_ref[...]
    qs_ref[...] = qs
    qe_ref[...] = qs * jnp.exp(cum)
    safe = jnp.max(-cum) <= SAFE_DECAY
    return safe, totals


def _same_chunk_mask(chunk):
    ri = lax.broadcasted_iota(jnp.int32, (BLK, BLK), 0)
    ci = lax.broadcasted_iota(jnp.int32, (BLK, BLK), 1)
    lc = chunk.bit_length() - 1
    return ((ri >> lc) == (ci >> lc)) & (ci <= ri)


def _inverse_decayed_keys(kk, cum):
    return kk * jnp.exp(jnp.minimum(-cum, SAFE_DECAY))


def _robust_same_chunk_scores(rows, chunk, cb_ref, kk_ref, qs_ref):
    ri = lax.broadcasted_iota(jnp.int32, (BLK, BLK), 0)
    ci = lax.broadcasted_iota(jnp.int32, (BLK, BLK), 1)
    cum = cb_ref[rows, :]
    qs = qs_ref[rows, :]
    kk = kk_ref[rows, :]
    hi, mid, lo = _split3(cum)
    acc = jnp.where(ri == ci, _dot_nt(qs.astype(BF16), kk.astype(BF16)), 0.0)
    for lvl in range(chunk.bit_length() - 1):
        m = 1 << lvl
        pivot = ((ri >> (lvl + 1)) << (lvl + 1)) + (m - 1)
        sel = jnp.where(ci == pivot, 1.0, 0.0).astype(BF16)
        ref_cum = _dot(sel, hi) + _dot(sel, mid) + _dot(sel, lo)
        e = jnp.exp(-jnp.abs(cum - ref_cum))
        keep = (((ri >> (lvl + 1)) == (ci >> (lvl + 1)))
                & ((ri & (2 * m - 1)) >= m) & ((ci & (2 * m - 1)) < m))
        s = _dot_nt((qs * e).astype(BF16), (kk * e).astype(BF16))
        acc = acc + jnp.where(keep, s, 0.0)
    return acc


def _rms_gate(o, ng, sg):
    return o * lax.rsqrt(jnp.mean(o * o, axis=-1, keepdims=True) + RMS_EPS) * ng * sg


def _hgrn_prompt_kernel(x_ref, lower_ref, wi_ref, ng_ref, wo_ref, g_ref, b_ref, y_ref, so_ref,
                        xb_ref, p_ref, cb_ref, tb_ref, eb_ref, kk_ref, v_ref, sg_ref, qs_ref,
                        qe_ref, kd_ref, a_ref, oh_ref, ds_ref, ec_ref, ob_ref, s_ref,
                        *, tm, layer, tiles_per_seq, alpha):
    chunk = PROMPT_CHUNK
    assert BLK == 2 * chunk
    n_blocks = tm // BLK
    i = pl.program_id(0)

    @pl.when(i % tiles_per_seq == 0)
    def _():
        s_ref[...] = jnp.zeros_like(s_ref)

    xb_ref[...] = x_ref[...].astype(BF16)
    second = lax.broadcasted_iota(jnp.int32, (BLK, HD), 0) >= chunk
    ri = lax.broadcasted_iota(jnp.int32, (BLK, BLK), 0)
    ci = lax.broadcasted_iota(jnp.int32, (BLK, BLK), 1)
    cross = (ri >= chunk) & (ci < chunk)
    same = _same_chunk_mask(chunk)

    def project(h, slot):
        p_ref[slot] = _dot(xb_ref[...], wi_ref[h])

    def views(slot):
        return tuple(r.at[slot] for r in (cb_ref, tb_ref, eb_ref, kk_ref, v_ref, sg_ref, qs_ref,
                                          qe_ref, kd_ref, a_ref, oh_ref, ds_ref, ec_ref))

    def scores(h, slot):
        cb, tb, eb, kk_s, v_s, sg_s, qs_s, qe_s, kd_s, a_s, _, ds_s, ec_s = views(slot)
        safe, _ = _hgrn_front(p_ref.at[slot], lower_ref, h, layer, tm, chunk,
                              cb, tb, eb, kk_s, v_s, sg_s, qs_s, qe_s)
        for nb in range(n_blocks):
            rows = pl.ds(nb * BLK, BLK)
            cum = cb[rows, :]
            kk = kk_s[rows, :]
            last0 = cb[pl.ds(nb * BLK + chunk - 1, 1), :]
            last1 = cb[pl.ds(nb * BLK + BLK - 1, 1), :]
            kd = kk * jnp.exp(jnp.where(second, last1, last0) - cum)
            kd_s[rows, :] = kd
            keys = jnp.concatenate([_inverse_decayed_keys(kk, cum), kd], axis=0).astype(BF16)
            s2 = _dot_nt(qe_s[rows, :].astype(BF16), keys)
            a = jnp.where(same, s2[:, 0:BLK], 0.0) + jnp.where(cross, s2[:, BLK:2 * BLK], 0.0)
            a_s[nb] = a.astype(BF16)
            k_blk = kd * jnp.where(second, 1.0, jnp.exp(last1))
            ds_s[nb] = _dot(k_blk.T.astype(BF16), v_s[rows, :].astype(BF16))
            ec_s[nb] = jnp.broadcast_to(jnp.exp(last0 + last1), (HD, HD)).T
        return safe

    def fix_scores(safe, slot):
        cb, _, _, kk_s, _, _, qs_s, qe_s, kd_s, a_s, _, _, _ = views(slot)

        @pl.when(jnp.logical_not(safe))
        def _():
            def fix(nb, _):
                rows = pl.ds(pl.multiple_of(nb * BLK, BLK), BLK)
                s1 = _dot_nt(qe_s[rows, :].astype(BF16), kd_s[rows, :].astype(BF16))
                a = _robust_same_chunk_scores(rows, chunk, cb, kk_s, qs_s)
                a_s[nb] = (a + jnp.where(cross, s1, 0.0)).astype(BF16)
                return 0
            lax.fori_loop(0, n_blocks, fix, 0)

    def outputs(h, slot):
        cb, _, _, _, v_s, sg_s, _, qe_s, _, a_s, oh_s, ds_s, ec_s = views(slot)
        states = [s_ref[h]]
        for nb in range(n_blocks):
            states.append(ec_s[nb] * states[nb] + ds_s[nb])
        s_ref[h] = states[n_blocks]
        for nb in range(n_blocks):
            rows = pl.ds(nb * BLK, BLK)
            last0 = cb[pl.ds(nb * BLK + chunk - 1, 1), :]
            q_blk = qe_s[rows, :] * jnp.where(second, jnp.exp(last0), 1.0)
            lhs = jnp.concatenate([a_s[nb], q_blk.astype(BF16)], axis=1)
            rhs = jnp.concatenate([v_s[rows, :].astype(BF16), states[nb].astype(BF16)], axis=0)
            oh_s[rows, :] = _dot(lhs, rhs)
        ob_ref[h] = _rms_gate(oh_s[...], ng_ref[...], sg_s[...]).astype(BF16)

    project(0, 0)

    def pair(h0, last):
        project(h0 + 1, 1)
        safe0 = scores(h0, 0)
        if not last:
            project(h0 + 2, 0)
        safe1 = scores(h0 + 1, 1)
        fix_scores(safe0, 0)
        fix_scores(safe1, 1)
        outputs(h0, 0)
        outputs(h0 + 1, 1)
        return 0

    lax.fori_loop(0, HEADS // 2 - 1, lambda hh, _: pair(2 * hh, False), 0)
    pair(HEADS - 2, True)
    for h in range(HEADS):
        xb_ref[:, h * HD:(h + 1) * HD] = ob_ref[h]
    _project_and_norm(x_ref, xb_ref, wo_ref, g_ref, b_ref, y_ref, alpha, NORM_SLAB)

    @pl.when(i % tiles_per_seq == tiles_per_seq - 1)
    def _():
        so_ref[0] = s_ref[...]


def _hgrn_prep(hg_lower, wi, ng, wo):
    n_layers = hg_lower.shape[0]
    lower = hg_lower.reshape(n_layers, HEADS, 1, HD)
    wih = (wi.reshape(n_layers, D_MODEL, 4, HEADS, HD).transpose(0, 3, 1, 2, 4)
           .reshape(n_layers, HEADS, D_MODEL, 4 * HD))
    return lower, wih.astype(BF16), ng.reshape(n_layers, 1, HD), wo.astype(BF16)


def _hgrn_prompt_call(x, prep, layer, g, b, *, seq_len, alpha):
    lower, wih, ng, wo = prep
    T = x.shape[0]
    tm = TM_PROMPT
    tiles_per_seq = seq_len // tm
    n_seq = T // seq_len
    row = pl.BlockSpec((tm, D_MODEL), lambda i: (i, 0))
    specs_args = [_layer_operand(w) for w in (lower, (wih, layer), (ng, layer), (wo, layer), g, b)]
    head_buf = pltpu.VMEM((2, tm, HD), F32)
    scratch = [pltpu.VMEM((tm, D_MODEL), BF16),
               pltpu.VMEM((2, tm, 4 * HD), F32),
               head_buf,
               pltpu.VMEM((2, tm // SUBLANES, HD), F32),
               pltpu.VMEM((2, tm // SUBLANES, HD), F32),
               head_buf, head_buf, head_buf, head_buf, head_buf,
               head_buf,
               pltpu.VMEM((2, tm // BLK, BLK, BLK), BF16),
               head_buf,
               pltpu.VMEM((2, tm // BLK, HD, HD), F32),
               pltpu.VMEM((2, tm // BLK, HD, HD), F32),
               pltpu.VMEM((HEADS, tm, HD), BF16),
               pltpu.VMEM((HEADS, HD, HD), F32)]
    y, so = pl.pallas_call(
        functools.partial(_hgrn_prompt_kernel, tm=tm, layer=layer,
                          tiles_per_seq=tiles_per_seq, alpha=alpha),
        grid=(T // tm,),
        in_specs=[row] + [s for s, _ in specs_args],
        out_specs=[row, pl.BlockSpec((1, HEADS, HD, HD), lambda i: (i // tiles_per_seq, 0, 0, 0))],
        out_shape=[jax.ShapeDtypeStruct((T, D_MODEL), F32),
                   jax.ShapeDtypeStruct((n_seq, HEADS, HD, HD), F32)],
        scratch_shapes=scratch,
        compiler_params=_params(),
        name="hgrn_prompt",
    )(x, *[a for _, a in specs_args])
    return y, so


def _hgrn_sample_front_kernel(x_ref, lower_ref, wi_ref,
                              oi_ref, qe_ref, v_ref, sg_ref, kdt_ref, et_ref,
                              xb_ref, p_ref, cb_ref, kk_ref, qs_ref, a_ref, last_ref,
                              *, tm, chunk, layer):
    assert chunk == SUBLANES
    xb_ref[...] = x_ref[...].astype(BF16)
    groups = tm // SUBLANES
    n_blocks = tm // BLK
    same = _same_chunk_mask(chunk)

    def head(h, _):
        p_ref[...] = _dot(xb_ref[...], wi_ref[h])
        safe, totals = _hgrn_front(p_ref, lower_ref, h, layer, tm, chunk,
                                   cb_ref, None, None, kk_ref, v_ref.at[h], sg_ref.at[h],
                                   qs_ref, qe_ref.at[h])
        for k in range(SUBLANES):
            last_ref[pl.ds(k, groups, stride=SUBLANES), :] = totals
        for nb in range(n_blocks):
            r0 = nb * BLK
            rows = pl.ds(r0, BLK)
            cum = cb_ref[rows, :]
            kk = kk_ref[rows, :]
            s = _dot_nt(qe_ref[h, rows, :].astype(BF16), _inverse_decayed_keys(kk, cum).astype(BF16))
            a_ref[nb] = jnp.where(same, s, 0.0).astype(BF16)
            last = last_ref[rows, :]
            kdt_ref[h, :, r0:r0 + BLK] = (kk * jnp.exp(last - cum)).T
            et_ref[h, :, r0:r0 + BLK] = jnp.exp(last).T

        @pl.when(jnp.logical_not(safe))
        def _():
            def fix(nb, _):
                rows = pl.ds(pl.multiple_of(nb * BLK, BLK), BLK)
                a_ref[nb] = _robust_same_chunk_scores(rows, chunk, cb_ref, kk_ref, qs_ref).astype(BF16)
                return 0
            lax.fori_loop(0, n_blocks, fix, 0)

        for nb in range(n_blocks):
            rows = pl.ds(nb * BLK, BLK)
            oi_ref[h, rows, :] = _dot(a_ref[nb], v_ref[h, rows, :].astype(BF16))
        return 0

    lax.fori_loop(0, HEADS, head, 0)


def _hgrn_sample_state_kernel(qe_ref, oi_ref, v_ref, sg_ref, kdt_ref, et_ref, s0_ref, ng_ref,
                              *rest, chunk):
    op_ref, so_ref = rest[-2:]
    row_seq = lax.broadcasted_iota(jnp.int32, (BLK, HD), 0) >> (chunk.bit_length() - 1)

    def head(h, _):
        kdt = kdt_ref[h].astype(BF16)
        et = et_ref[h]
        v = v_ref[h]
        for sq in range(BLK // chunk):
            rows = slice(sq * chunk, (sq + 1) * chunk)
            s0 = s0_ref[sq, h]
            o = oi_ref[h, rows, :] + _dot(qe_ref[h, rows, :].astype(BF16), s0.astype(BF16))
            op_ref[h, rows, :] = _rms_gate(o, ng_ref[...], sg_ref[h, rows, :])
            vm = jnp.where(row_seq == sq, v, 0.0).astype(BF16)
            e_col = jnp.broadcast_to(et[:, sq * chunk:sq * chunk + 1], (HD, HD))
            so_ref[sq, h] = e_col * s0 + _dot(kdt, vm)
        return 0

    lax.fori_loop(0, HEADS, head, 0)


def _hgrn_sample_out_kernel(x_ref, op_ref, wo_ref, g_ref, b_ref, y_ref, ob_ref, *, alpha):
    for h in range(HEADS):
        ob_ref[:, h * HD:(h + 1) * HD] = op_ref[h].astype(BF16)
    f = _dot(ob_ref[...], wo_ref[...])
    y_ref[...] = _deepnorm(x_ref[...], f, g_ref[...], b_ref[...], alpha)


def _hgrn_sample_call(x, s0, so_prev, prep, layer, g, b, *, seq_len, alpha):
    lower, wih, ng, wo = prep
    T = x.shape[0]
    chunk = seq_len
    heads_rows = jax.ShapeDtypeStruct((HEADS, T, HD), F32)
    heads_cols = jax.ShapeDtypeStruct((HEADS, HD, T), F32)
    full = lambda s: pl.BlockSpec(s, lambda i: (0,) * len(s))
    head_buf = pltpu.VMEM((T, HD), F32)
    specs_args = [_layer_operand(w) for w in (lower, (wih, layer))]
    oi, qe, v, sg, kdt, et = pl.pallas_call(
        functools.partial(_hgrn_sample_front_kernel, tm=T, chunk=chunk, layer=layer),
        grid=(1,),
        in_specs=[full(x.shape)] + [s for s, _ in specs_args],
        out_specs=[full(heads_rows.shape)] * 4 + [full(heads_cols.shape)] * 2,
        out_shape=[heads_rows] * 4 + [heads_cols] * 2,
        scratch_shapes=[pltpu.VMEM((T, D_MODEL), BF16), pltpu.VMEM((T, 4 * HD), F32),
                        head_buf, head_buf, head_buf,
                        pltpu.VMEM((T // BLK, BLK, BLK), BF16), head_buf],
        compiler_params=_params(),
        name="hgrn_sample_front",
    )(x, *[a for _, a in specs_args])
    seqs = BLK // chunk
    by_rows = pl.BlockSpec((HEADS, BLK, HD), lambda i: (0, i, 0))
    by_cols = pl.BlockSpec((HEADS, HD, BLK), lambda i: (0, 0, i))
    state = pl.BlockSpec((None, seqs, HEADS, HD, HD), lambda i: (layer, i, 0, 0, 0))
    ng_spec, ng_arr = _layer_operand((ng, layer))
    in_specs = [by_rows, by_rows, by_rows, by_rows, by_cols, by_cols, state, ng_spec]
    args = [qe, oi, v, sg, kdt, et, s0, ng_arr]
    in_specs.append(pl.BlockSpec(memory_space=pl.ANY))
    args.append(so_prev)
    aliases = {len(args) - 1: 1}
    op, so = pl.pallas_call(
        functools.partial(_hgrn_sample_state_kernel, chunk=chunk),
        grid=(T // BLK,),
        in_specs=in_specs,
        out_specs=[by_rows, state],
        out_shape=[heads_rows, jax.ShapeDtypeStruct(s0.shape, F32)],
        input_output_aliases=aliases,
        compiler_params=_params(),
        name="hgrn_sample_state",
    )(*args)
    specs_args = [_layer_operand(w) for w in ((wo, layer), g, b)]
    y = pl.pallas_call(
        functools.partial(_hgrn_sample_out_kernel, alpha=alpha),
        grid=(1,),
        in_specs=[full(x.shape), full(op.shape)] + [s for s, _ in specs_args],
        out_specs=full(x.shape),
        out_shape=jax.ShapeDtypeStruct(x.shape, F32),
        scratch_shapes=[pltpu.VMEM((T, D_MODEL), BF16)],
        compiler_params=_params(),
        name="hgrn_sample_out",
    )(x, op, *[a for _, a in specs_args])
    return y, so


def _to_time_major(a):
    return jnp.transpose(a, (1, 0, 2)).reshape(-1, a.shape[-1])


def _from_time_major(a, batch):
    return jnp.transpose(a.reshape(-1, batch, a.shape[-1]), (1, 0, 2))


def kernel(x_prompt, x_sample, state_rglru_h, state_rglru_conv, state_hgrn_s, state_ffn_conv,
           ln_g, ln_b, rg_w_in, rg_conv_w, rg_conv_b, rg_gate_w, rg_gate_b, rg_lambda, rg_w_out,
           hg_lower, hg_w_in, hg_norm_g, hg_w_out, ffn_w_in, ffn_conv_w, ffn_conv_b, ffn_w_out):
    depth = ln_g.shape[0]
    alpha = (2.0 * depth) ** 0.25
    pb, pl_len, _ = x_prompt.shape
    sb, sl_len, _ = x_sample.shape

    n_rg, n_ffn = rg_w_in.shape[0], ffn_w_in.shape[0]
    lng = ln_g.reshape(depth * 2, 1, D_MODEL)
    lnb = ln_b.reshape(depth * 2, 1, D_MODEL)
    rg_wi = _rglru_in_proj(rg_w_in)
    rg_wg = jnp.concatenate([rg_gate_w[:, 0], rg_gate_w[:, 1]], axis=-1).astype(BF16)
    rg_wo = rg_w_out.astype(BF16)
    rg_cb = rg_conv_b.reshape(n_rg, 1, D_MODEL)
    rg_lam = rg_lambda.reshape(n_rg, 1, D_MODEL)
    hg = _hgrn_prep(hg_lower, hg_w_in, hg_norm_g, hg_w_out)
    ffn_wi = ffn_w_in.astype(BF16)
    ffn_wo = ffn_w_out.astype(BF16)
    ffn_cb = ffn_conv_b.reshape(n_ffn, 1, D_FF)
    rg_cst = jnp.transpose(state_rglru_conv, (0, 2, 1, 3)).reshape(n_rg, -1, D_MODEL)
    ffn_cst = jnp.transpose(state_ffn_conv, (0, 2, 1, 3)).reshape(n_ffn, -1, D_FF)

    xp = x_prompt.reshape(pb * pl_len, D_MODEL)
    xs = _to_time_major(x_sample)
    p_h, p_rc, p_s, p_fc = [], [], [], []
    s_h, s_rc, s_fc = [], [], []
    s_s = jnp.zeros(state_hgrn_s.shape, F32)
    for i in range(depth):
        j = i // 2
        g0, b0 = (lng, 2 * i), (lnb, 2 * i)
        if i % 2 == 0:
            w = ((rg_wi, j), (rg_conv_w, j), (rg_cb, j), (rg_wg, j), (rg_gate_b, j), (rg_lam, j),
                 (rg_wo, j), g0, b0)
            xp, ho, co = _rglru_call(xp, None, None, *w, seq_len=pl_len, alpha=alpha)
            p_h.append(ho[:, 0])
            p_rc.append(co)
            xs, ho, co = _rglru_call(xs, (state_rglru_h, j), (rg_cst, j), *w,
                                     seq_len=sl_len, alpha=alpha)
            s_h.append(ho)
            s_rc.append(co)
        else:
            xp, so = _hgrn_prompt_call(xp, hg, j, g0, b0, seq_len=pl_len, alpha=alpha)
            p_s.append(so)
            xs_bm = _from_time_major(xs, sb).reshape(sb * sl_len, D_MODEL)
            ys_bm, s_s = _hgrn_sample_call(xs_bm, state_hgrn_s, s_s, hg, j, g0, b0,
                                           seq_len=sl_len, alpha=alpha)
            xs = _to_time_major(ys_bm.reshape(sb, sl_len, D_MODEL))
        w = ((ffn_wi, i), (ffn_conv_w, i), (ffn_cb, i), (ffn_wo, i), (lng, 2 * i + 1), (lnb, 2 * i + 1))
        xp, fo = _ffn_call(xp, None, *w, seq_len=pl_len, alpha=alpha)
        p_fc.append(fo)
        xs, fo = _ffn_call(xs, (ffn_cst, i), *w, seq_len=sl_len, alpha=alpha)
        s_fc.append(fo)

    def stacked_from_time_major(parts):
        a = jnp.stack(parts)
        a = a.reshape(a.shape[0], -1, sb, a.shape[-1])
        return jnp.transpose(a, (0, 2, 1, 3))

    return (xp.reshape(x_prompt.shape), _from_time_major(xs, sb),
            jnp.stack(p_h), jnp.stack(p_rc), jnp.stack(p_s), jnp.stack(p_fc),
            jnp.stack(s_h), stacked_from_time_major(s_rc), s_s, stacked_from_time_major(s_fc))
```

```python
import functools

import jax
import jax.numpy as jnp
from jax import lax
from jax.experimental import pallas as pl
from jax.experimental.pallas import tpu as pltpu

F32 = jnp.float32
BF16 = jnp.bfloat16

D_MODEL = 1024
D_FF = 2816
HEADS = 8
HD = 128
RG_CONV = 4
FFN_CONV = 3
RG_C = 8.0
LN_EPS = 1e-5
RMS_EPS = 1e-6
F_FLOOR = 1e-30
SUBLANES = 8
LANES = 128
BLK = 128
PROMPT_CHUNK = 64
SAFE_DECAY = 80.0
FF_TILE = 256
TM_PROMPT = 1024
TM_FFN = 1024
NORM_SLAB = 256
VMEM_LIMIT = 56 * 1024 * 1024


def _dot(a, b):
    return jnp.dot(a, b, preferred_element_type=F32)


def _dot_nt(a, b):
    return lax.dot_general(a, b, (((1,), (1,)), ((), ())), preferred_element_type=F32)


def _sigmoid(x):
    return 0.5 * jnp.tanh(0.5 * x) + 0.5


def _resident(shape):
    nd = len(shape)
    return pl.BlockSpec(shape, lambda i: (0,) * nd, pipeline_mode=pl.Buffered(1))


def _layer_shape(w):
    return w[0].shape[1:] if isinstance(w, tuple) else w.shape


def _layer_operand(w):
    if not isinstance(w, tuple):
        return _resident(w.shape), w
    arr, layer = w
    tail = (0,) * (arr.ndim - 1)
    spec = pl.BlockSpec((None,) + arr.shape[1:], lambda i: (layer,) + tail,
                        pipeline_mode=pl.Buffered(1))
    return spec, arr


def _deepnorm(x, f, g, b, alpha):
    z = alpha * x + f
    mu = jnp.mean(z, axis=-1, keepdims=True)
    zc = z - mu
    var = jnp.mean(zc * zc, axis=-1, keepdims=True)
    return zc * lax.rsqrt(var + LN_EPS) * g + b


def _project_and_norm(x_ref, lhs_ref, wo_ref, g_ref, b_ref, y_ref, alpha, slab):
    rows = x_ref.shape[0]
    slab = min(slab, rows)
    for r0 in range(0, rows, slab):
        sl = pl.ds(r0, slab)
        f = _dot(lhs_ref[sl, :], wo_ref[...])
        y_ref[sl, :] = _deepnorm(x_ref[sl, :], f, g_ref[...], b_ref[...], alpha)


def _params(n_axes=1):
    return pltpu.CompilerParams(dimension_semantics=("arbitrary",) * n_axes,
                                vmem_limit_bytes=VMEM_LIMIT)


def _ffn_kernel(*refs, tm, step, tiles_per_seq, has_state, alpha):
    if has_state:
        (x_ref, st_ref, wi_ref, cw_ref, cb_ref, wo_ref, g_ref, b_ref,
         y_ref, so_ref, gbuf, hbuf) = refs
        carry = None
    else:
        (x_ref, wi_ref, cw_ref, cb_ref, wo_ref, g_ref, b_ref,
         y_ref, so_ref, gbuf, hbuf, carry) = refs
    halo = gbuf.shape[0] - tm
    xb = x_ref[...].astype(BF16)
    if not has_state:
        @pl.when(pl.program_id(0) % tiles_per_seq == 0)
        def _():
            carry[...] = jnp.zeros_like(carry)
    for j in range(D_FF // FF_TILE):
        c0 = j * FF_TILE
        g = _dot(xb, wi_ref[:, c0:c0 + FF_TILE])
        u = _dot(xb, wi_ref[:, D_FF + c0:D_FF + c0 + FF_TILE])
        if has_state:
            gbuf[0:halo, :] = st_ref[:, c0:c0 + FF_TILE]
        else:
            gbuf[0:halo, :] = carry[:, c0:c0 + FF_TILE]
        gbuf[halo:halo + tm, :] = g
        g1 = gbuf[halo - step:halo - step + tm, :]
        g2 = gbuf[halo - 2 * step:halo - 2 * step + tm, :]
        cw = cw_ref[:, c0:c0 + FF_TILE]
        gc = g * cw[2:3] + g1 * cw[1:2] + g2 * cw[0:1] + cb_ref[:, c0:c0 + FF_TILE]
        hbuf[:, c0:c0 + FF_TILE] = (jax.nn.gelu(gc) * u).astype(BF16)
        if has_state:
            so_ref[:, c0:c0 + FF_TILE] = gbuf[tm:tm + halo, :]
        else:
            carry[:, c0:c0 + FF_TILE] = gbuf[tm:tm + halo, :]
            so_ref[0, :, c0:c0 + FF_TILE] = gbuf[halo + tm - 2:halo + tm, :]
    _project_and_norm(x_ref, hbuf, wo_ref, g_ref, b_ref, y_ref, alpha, NORM_SLAB)


def _ffn_call(x, state_tm, wi, cw, cb, wo, g, b, *, seq_len, alpha):
    T = x.shape[0]
    has_state = state_tm is not None
    if has_state:
        st_shape = _layer_shape(state_tm)
        tm, step, halo, tiles_per_seq = T, T // seq_len, st_shape[0], 1
    else:
        tm, step, halo, tiles_per_seq = TM_FFN, 1, SUBLANES, seq_len // TM_FFN
    n_tiles = T // tm
    n_seq = T // seq_len
    row = pl.BlockSpec((tm, D_MODEL), lambda i: (i, 0))
    operands = ([state_tm] if has_state else []) + [wi, cw, cb, wo, g, b]
    specs_args = [_layer_operand(w) for w in operands]
    in_specs = [row] + [s for s, _ in specs_args]
    args = [x] + [a for _, a in specs_args]
    scratch = [pltpu.VMEM((halo + tm, FF_TILE), F32), pltpu.VMEM((tm, D_FF), BF16)]
    if has_state:
        so_shape = jax.ShapeDtypeStruct(st_shape, F32)
        so_spec = pl.BlockSpec(st_shape, lambda i: (0, 0))
    else:
        so_shape = jax.ShapeDtypeStruct((n_seq, FFN_CONV - 1, D_FF), F32)
        so_spec = pl.BlockSpec((1, FFN_CONV - 1, D_FF), lambda i: (i // tiles_per_seq, 0, 0))
        scratch.append(pltpu.VMEM((halo, D_FF), F32))
    return pl.pallas_call(
        functools.partial(_ffn_kernel, tm=tm, step=step, tiles_per_seq=tiles_per_seq,
                          has_state=has_state, alpha=alpha),
        grid=(n_tiles,),
        in_specs=in_specs,
        out_specs=[row, so_spec],
        out_shape=[jax.ShapeDtypeStruct((T, D_MODEL), F32), so_shape],
        scratch_shapes=scratch,
        compiler_params=_params(),
        name="ffn_state" if has_state else "ffn_prompt",
    )(*args)


def _rglru_gates(xc, gt, h, bg_ref, log_sig_lam):
    c0 = h * HD
    bg = bg_ref[:, c0:c0 + HD]
    r = _sigmoid(gt[:, 0:HD] + bg[0:1])
    ig = _sigmoid(gt[:, HD:2 * HD] + bg[1:2])
    a = jnp.exp(RG_C * r * log_sig_lam[:, c0:c0 + HD])
    return a, jnp.sqrt(jnp.maximum(1.0 - a * a, 0.0)) * ig * xc


def _rglru_state_kernel(x_ref, h0_ref, cst_ref, wi_ref, cw_ref, cb_ref, wg_ref, bg_ref, lam_ref,
                        wo_ref, g_ref, b_ref, y_ref, ho_ref, co_ref, xbuf, ybuf, *, tm, step, alpha):
    halo = xbuf.shape[0] - tm
    xb = x_ref[...].astype(BF16)
    xbuf[0:halo, :] = cst_ref[...]
    log_sig_lam = jax.nn.log_sigmoid(lam_ref[...])
    for h in range(HEADS):
        c0 = h * HD
        proj = _dot(xb, wi_ref[:, 2 * c0:2 * c0 + 2 * HD])
        gate = jax.nn.gelu(proj[:, 0:HD])
        xbuf[halo:halo + tm, c0:c0 + HD] = proj[:, HD:2 * HD]
        cw = cw_ref[:, c0:c0 + HD]
        xc = cb_ref[:, c0:c0 + HD] + proj[:, HD:2 * HD] * cw[RG_CONV - 1:RG_CONV]
        for j in range(RG_CONV - 1):
            back = (RG_CONV - 1 - j) * step
            xc = xc + xbuf[halo - back:halo - back + tm, c0:c0 + HD] * cw[j:j + 1]
        a, bv = _rglru_gates(xc, _dot(xc.astype(BF16), wg_ref[h]), h, bg_ref, log_sig_lam)
        hh = h0_ref[:, c0:c0 + HD]
        for t in range(tm // step):
            rows = slice(t * step, (t + 1) * step)
            hh = a[rows] * hh + bv[rows]
            ybuf[rows, c0:c0 + HD] = (hh * gate[rows]).astype(BF16)
        ho_ref[:, c0:c0 + HD] = hh
    co_ref[...] = xbuf[tm:tm + halo, :]
    _project_and_norm(x_ref, ybuf, wo_ref, g_ref, b_ref, y_ref, alpha, NORM_SLAB)


def _rglru_prompt_kernel(x_ref, wi_ref, cw_ref, cb_ref, wg_ref, bg_ref, lam_ref, wo_ref, g_ref, b_ref,
                         y_ref, ho_ref, co_ref,
                         xbuf, gbuf, hbuf, tot, hin, sup, ybuf, ccarry, hcarry,
                         *, tm, tiles_per_seq, alpha):
    halo = xbuf.shape[1] - tm
    groups = tm // SUBLANES
    supers = groups // SUBLANES
    taps = RG_CONV - 1

    @pl.when(pl.program_id(0) % tiles_per_seq == 0)
    def _():
        ccarry[...] = jnp.zeros_like(ccarry)
        hcarry[...] = jnp.zeros_like(hcarry)

    xb = x_ref[...].astype(BF16)
    log_sig_lam = jax.nn.log_sigmoid(lam_ref[...])

    def project(h):
        c0 = h * HD
        proj = _dot(xb, wi_ref[:, 2 * c0:2 * c0 + 2 * HD])
        gbuf[:, c0:c0 + HD] = jax.nn.gelu(proj[:, 0:HD])
        xs = xbuf.at[h]
        xs[0:halo, :] = ccarry[h]
        xs[halo:halo + tm, :] = proj[:, HD:2 * HD]
        ccarry[h] = xs[tm:tm + halo, :]
        co_ref[0, :, c0:c0 + HD] = xs[halo + tm - taps:halo + tm, :]

    def conv(h):
        c0 = h * HD
        xs = xbuf.at[h]
        cw = cw_ref[:, c0:c0 + HD]
        cb = cb_ref[:, c0:c0 + HD]
        lock = [xs[pl.ds(halo - taps + m, groups, stride=SUBLANES), :]
                for m in range(SUBLANES + taps)]
        xc = []
        for k in range(SUBLANES):
            acc = cb + lock[k + taps] * cw[taps:taps + 1]
            for j in range(taps):
                acc = acc + lock[k + j] * cw[j:j + 1]
            xc.append(acc)
        xc = jnp.concatenate(xc, axis=0)
        return xc, _dot(xc.astype(BF16), wg_ref[h])

    def scan(h, xc, gt):
        c0 = h * HD
        a, bv = _rglru_gates(xc, gt, h, bg_ref, log_sig_lam)
        pa, pb = a[0:groups], bv[0:groups]
        a_in, b_in = [pa], [pb]
        for k in range(1, SUBLANES):
            ak = a[k * groups:(k + 1) * groups]
            pb = ak * pb + bv[k * groups:(k + 1) * groups]
            pa = ak * pa
            a_in.append(pa)
            b_in.append(pb)
        tot[h, 0] = pa
        tot[h, 1] = pb
        qa = tot[h, 0, pl.ds(0, supers, stride=SUBLANES), :]
        qb = tot[h, 1, pl.ds(0, supers, stride=SUBLANES), :]
        a_sup, b_sup = [qa], [qb]
        for j in range(1, SUBLANES):
            aj = tot[h, 0, pl.ds(j, supers, stride=SUBLANES), :]
            qb = aj * qb + tot[h, 1, pl.ds(j, supers, stride=SUBLANES), :]
            qa = aj * qa
            a_sup.append(qa)
            b_sup.append(qb)
        hc = hcarry[h]
        for s in range(supers):
            sup[h, pl.ds(s, 1), :] = hc
            hc = qa[s:s + 1] * hc + qb[s:s + 1]
        hcarry[h] = hc
        ho_ref[0, :, c0:c0 + HD] = hc
        h_sup = sup[h]
        hin[h, pl.ds(0, supers, stride=SUBLANES), :] = h_sup
        for j in range(1, SUBLANES):
            hin[h, pl.ds(j, supers, stride=SUBLANES), :] = a_sup[j - 1] * h_sup + b_sup[j - 1]
        h_grp = hin[h]
        for k in range(SUBLANES):
            hbuf[h, pl.ds(k, groups, stride=SUBLANES), :] = a_in[k] * h_grp + b_in[k]

    convs = {}
    for h in range(HEADS + 2):
        if h < HEADS:
            project(h)
        if 1 <= h <= HEADS:
            convs[h - 1] = conv(h - 1)
        if h >= 2:
            scan(h - 2, *convs.pop(h - 2))
    for h in range(HEADS):
        c0 = h * HD
        ybuf[:, c0:c0 + HD] = (hbuf[h] * gbuf[:, c0:c0 + HD]).astype(BF16)
    _project_and_norm(x_ref, ybuf, wo_ref, g_ref, b_ref, y_ref, alpha, NORM_SLAB)


def _rglru_in_proj(w_in):
    lead = w_in.shape[:-1]
    w = w_in.reshape(*lead, 2, HEADS, HD)
    return jnp.swapaxes(w, -3, -2).reshape(*lead, 2 * HEADS * HD).astype(BF16)


def _rglru_call(x, h0, cst_tm, wi, cw, cb, wg, bg, lam, wo, g, b, *, seq_len, alpha):
    T = x.shape[0]
    has_state = h0 is not None
    n_seq = T // seq_len
    operands = ([h0, cst_tm] if has_state else []) + [wi, cw, cb, wg, bg, lam, wo, g, b]
    specs_args = [_layer_operand(w) for w in operands]
    args = [x] + [a for _, a in specs_args]
    if has_state:
        h0_shape, cst_shape = _layer_shape(h0), _layer_shape(cst_tm)
        tm, halo = T, cst_shape[0]
        body = functools.partial(_rglru_state_kernel, tm=tm, step=T // seq_len, alpha=alpha)
        row = pl.BlockSpec((tm, D_MODEL), lambda i: (i, 0))
        out_shape = [jax.ShapeDtypeStruct((T, D_MODEL), F32),
                     jax.ShapeDtypeStruct(h0_shape, F32),
                     jax.ShapeDtypeStruct(cst_shape, F32)]
        out_specs = [row, pl.BlockSpec(h0_shape, lambda i: (0, 0)),
                     pl.BlockSpec(cst_shape, lambda i: (0, 0))]
        scratch = [pltpu.VMEM((halo + tm, D_MODEL), F32), pltpu.VMEM((tm, D_MODEL), BF16)]
    else:
        tm, halo, tiles_per_seq = TM_PROMPT, SUBLANES, seq_len // TM_PROMPT
        groups = tm // SUBLANES
        body = functools.partial(_rglru_prompt_kernel, tm=tm, tiles_per_seq=tiles_per_seq, alpha=alpha)
        row = pl.BlockSpec((tm, D_MODEL), lambda i: (i, 0))
        out_shape = [jax.ShapeDtypeStruct((T, D_MODEL), F32),
                     jax.ShapeDtypeStruct((n_seq, 1, D_MODEL), F32),
                     jax.ShapeDtypeStruct((n_seq, RG_CONV - 1, D_MODEL), F32)]
        out_specs = [row,
                     pl.BlockSpec((1, 1, D_MODEL), lambda i: (i // tiles_per_seq, 0, 0)),
                     pl.BlockSpec((1, RG_CONV - 1, D_MODEL), lambda i: (i // tiles_per_seq, 0, 0))]
        scratch = [pltpu.VMEM((HEADS, halo + tm, HD), F32),
                   pltpu.VMEM((tm, D_MODEL), F32),
                   pltpu.VMEM((HEADS, tm, HD), F32),
                   pltpu.VMEM((HEADS, 2, groups, HD), F32),
                   pltpu.VMEM((HEADS, groups, HD), F32),
                   pltpu.VMEM((HEADS, groups // SUBLANES, HD), F32),
                   pltpu.VMEM((tm, D_MODEL), BF16),
                   pltpu.VMEM((HEADS, halo, HD), F32),
                   pltpu.VMEM((HEADS, 1, HD), F32)]
    in_specs = [row] + [s for s, _ in specs_args]
    n_tiles = T // tm
    return pl.pallas_call(
        body,
        grid=(n_tiles,),
        in_specs=in_specs,
        out_specs=out_specs,
        out_shape=out_shape,
        scratch_shapes=scratch,
        compiler_params=_params(),
        name="rglru_state" if has_state else "rglru_prompt",
    )(*args)


def _split3(x):
    hi = x.astype(BF16)
    r = x - hi.astype(F32)
    mid = r.astype(BF16)
    lo = (r - mid.astype(F32)).astype(BF16)
    return hi, mid, lo


def _lower_bound(lower_ref, h, layer):
    rows = [lower_ref[n, h] for n in range(lower_ref.shape[0])]
    m = functools.reduce(jnp.maximum, rows)
    es = [jnp.exp(r - m) for r in rows]
    tot = functools.reduce(lambda p, q: p + q, es)
    sm = [e / tot for e in es]
    cs = functools.reduce(lambda p, q: p + q, sm[:layer + 1])
    return jnp.maximum(cs - sm[0], 0.0)


def _chunk_cumsum(cb_ref, tb_ref, eb_ref, tm, chunk):
    groups = tm // SUBLANES
    p = cb_ref[pl.ds(0, groups, stride=SUBLANES), :]
    for k in range(1, SUBLANES):
        p = p + cb_ref[pl.ds(k, groups, stride=SUBLANES), :]
        cb_ref[pl.ds(k, groups, stride=SUBLANES), :] = p
    if chunk > SUBLANES:
        gc = chunk // SUBLANES
        nch = tm // chunk
        tb_ref[...] = p
        e = jnp.zeros((nch, HD), F32)
        eb_ref[pl.ds(0, nch, stride=gc), :] = e
        for j in range(1, gc):
            e = e + tb_ref[pl.ds(j - 1, nch, stride=gc), :]
            eb_ref[pl.ds(j, nch, stride=gc), :] = e
        off = eb_ref[...]
        for k in range(SUBLANES):
            cb_ref[pl.ds(k, groups, stride=SUBLANES), :] = (
                cb_ref[pl.ds(k, groups, stride=SUBLANES), :] + off)
    return p


def _hgrn_front(p_ref, lower_ref, h, layer, tm, chunk,
                cb_ref, tb_ref, eb_ref, kk_ref, v_ref, sg_ref, qs_ref, qe_ref):
    lb = _lower_bound(lower_ref, h, layer)
    fg = lb + (1.0 - lb) * _sigmoid(p_ref[:, HD:2 * HD])
    cb_ref[...] = jnp.log(jnp.maximum(fg, F_FLOOR))
    kk_ref[...] = 1.0 - fg
    q = p_ref[:, 0:HD]
    qs = q * _sigmoid(q) * (HD ** -0.5)
    v_ref[...] = p_ref[:, 2 * HD:3 * HD]
    gg = p_ref[:, 3 * HD:4 * HD]
    sg_ref[...] = gg * _sigmoid(gg)
    totals = _chunk_cumsum(cb_ref, tb_ref, eb_ref, tm, chunk)
    cum = cb_ref[...]
    qs_ref[...] = qs
    qe_ref[...] = qs * jnp.exp(cum)
    safe = jnp.max(-cum) <= SAFE_DECAY
    return safe, totals


def _same_chunk_mask(chunk):
    ri = lax.broadcasted_iota(jnp.int32, (BLK, BLK), 0)
    ci = lax.broadcasted_iota(jnp.int32, (BLK, BLK), 1)
    lc = chunk.bit_length() - 1
    return ((ri >> lc) == (ci >> lc)) & (ci <= ri)


def _inverse_decayed_keys(kk, cum):
    return kk * jnp.exp(jnp.minimum(-cum, SAFE_DECAY))


def _robust_same_chunk_scores(rows, chunk, cb_ref, kk_ref, qs_ref):
    ri = lax.broadcasted_iota(jnp.int32, (BLK, BLK), 0)
    ci = lax.broadcasted_iota(jnp.int32, (BLK, BLK), 1)
    cum = cb_ref[rows, :]
    qs = qs_ref[rows, :]
    kk = kk_ref[rows, :]
    hi, mid, lo = _split3(cum)
    acc = jnp.where(ri == ci, _dot_nt(qs.astype(BF16), kk.astype(BF16)), 0.0)
    for lvl in range(chunk.bit_length() - 1):
        m = 1 << lvl
        pivot = ((ri >> (lvl + 1)) << (lvl + 1)) + (m - 1)
        sel = jnp.where(ci == pivot, 1.0, 0.0).astype(BF16)
        ref_cum = _dot(sel, hi) + _dot(sel, mid) + _dot(sel, lo)
        e = jnp.exp(-jnp.abs(cum - ref_cum))
        keep = (((ri >> (lvl + 1)) == (ci >> (lvl + 1)))
                & ((ri & (2 * m - 1)) >= m) & ((ci & (2 * m - 1)) < m))
        s = _dot_nt((qs * e).astype(BF16), (kk * e).astype(BF16))
        acc = acc + jnp.where(keep, s, 0.0)
    return acc


def _rms_gate(o, ng, sg):
    return o * lax.rsqrt(jnp.mean(o * o, axis=-1, keepdims=True) + RMS_EPS) * ng * sg


def _hgrn_prompt_kernel(x_ref, lower_ref, wi_ref, ng_ref, wo_ref, g_ref, b_ref, y_ref, so_ref,
                        xb_ref, p_ref, cb_ref, tb_ref, eb_ref, kk_ref, v_ref, sg_ref, qs_ref,
                        qe_ref, kd_ref, a_ref, oh_ref, ds_ref, ec_ref, st_ref, ob_ref, s_ref,
                        *, tm, layer, tiles_per_seq, alpha):
    chunk = PROMPT_CHUNK
    assert BLK == 2 * chunk
    n_blocks = tm // BLK
    i = pl.program_id(0)

    @pl.when(i % tiles_per_seq == 0)
    def _():
        s_ref[...] = jnp.zeros_like(s_ref)

    xb_ref[...] = x_ref[...].astype(BF16)
    second = lax.broadcasted_iota(jnp.int32, (BLK, HD), 0) >= chunk
    ri = lax.broadcasted_iota(jnp.int32, (BLK, BLK), 0)
    ci = lax.broadcasted_iota(jnp.int32, (BLK, BLK), 1)
    cross = (ri >= chunk) & (ci < chunk)
    same = _same_chunk_mask(chunk)

    def project(h, slot):
        p_ref[slot] = _dot(xb_ref[...], wi_ref[h])

    def views(slot):
        return tuple(r.at[slot] for r in (cb_ref, tb_ref, eb_ref, kk_ref, v_ref, sg_ref, qs_ref,
                                          qe_ref, kd_ref, a_ref, oh_ref, ds_ref, ec_ref))

    def scores(h, slot):
        cb, tb, eb, kk_s, v_s, sg_s, qs_s, qe_s, kd_s, a_s, _, ds_s, ec_s = views(slot)
        safe, _ = _hgrn_front(p_ref.at[slot], lower_ref, h, layer, tm, chunk,
                              cb, tb, eb, kk_s, v_s, sg_s, qs_s, qe_s)
        for nb in range(n_blocks):
            rows = pl.ds(nb * BLK, BLK)
            cum = cb[rows, :]
            kk = kk_s[rows, :]
            last0 = cb[pl.ds(nb * BLK + chunk - 1, 1), :]
            last1 = cb[pl.ds(nb * BLK + BLK - 1, 1), :]
            kd = kk * jnp.exp(jnp.where(second, last1, last0) - cum)
            kd_s[rows, :] = kd
            keys = jnp.concatenate([_inverse_decayed_keys(kk, cum), kd], axis=0).astype(BF16)
            s2 = _dot_nt(qe_s[rows, :].astype(BF16), keys)
            a = jnp.where(same, s2[:, 0:BLK], 0.0) + jnp.where(cross, s2[:, BLK:2 * BLK], 0.0)
            a_s[nb] = a.astype(BF16)
            k_blk = kd * jnp.where(second, 1.0, jnp.exp(last1))
            ds_s[nb] = _dot(k_blk.T.astype(BF16), v_s[rows, :].astype(BF16))
            ec_s[nb] = jnp.broadcast_to(jnp.exp(last0 + last1), (HD, HD)).T
        return safe

    def robust_scores(slot):
        cb, _, _, kk_s, _, _, qs_s, qe_s, kd_s, a_s, _, _, _ = views(slot)

        def fix(nb, _):
            rows = pl.ds(pl.multiple_of(nb * BLK, BLK), BLK)
            s1 = _dot_nt(qe_s[rows, :].astype(BF16), kd_s[rows, :].astype(BF16))
            a = _robust_same_chunk_scores(rows, chunk, cb, kk_s, qs_s)
            a_s[nb] = (a + jnp.where(cross, s1, 0.0)).astype(BF16)
            return 0
        lax.fori_loop(0, n_blocks, fix, 0)

    def advance_state(h, slot):
        ds_s, ec_s = views(slot)[11:13]
        state = s_ref[h]
        for nb in range(n_blocks):
            st_ref[slot, nb] = state
            state = ec_s[nb] * state + ds_s[nb]
        s_ref[h] = state

    def outputs(h, slot):
        cb, _, _, _, v_s, sg_s, _, qe_s, _, a_s, oh_s, _, _ = views(slot)
        for nb in range(n_blocks):
            rows = pl.ds(nb * BLK, BLK)
            last0 = cb[pl.ds(nb * BLK + chunk - 1, 1), :]
            q_blk = qe_s[rows, :] * jnp.where(second, jnp.exp(last0), 1.0)
            lhs = jnp.concatenate([a_s[nb], q_blk.astype(BF16)], axis=1)
            rhs = jnp.concatenate([v_s[rows, :].astype(BF16), st_ref[slot, nb].astype(BF16)], axis=0)
            oh_s[rows, :] = _dot(lhs, rhs)
        ob_ref[h] = _rms_gate(oh_s[...], ng_ref[...], sg_s[...]).astype(BF16)

    project(0, 0)

    def pair(h0, last):
        project(h0 + 1, 1)
        safe0 = scores(h0, 0)
        advance_state(h0, 0)
        outputs(h0, 0)
        if not last:
            project(h0 + 2, 0)
        safe1 = scores(h0 + 1, 1)
        advance_state(h0 + 1, 1)
        outputs(h0 + 1, 1)

        @pl.when(jnp.logical_not(jnp.logical_and(safe0, safe1)))
        def _():
            for slot in range(2):
                robust_scores(slot)
                outputs(h0 + slot, slot)
        return 0

    lax.fori_loop(0, HEADS // 2 - 1, lambda hh, _: pair(2 * hh, False), 0)
    pair(HEADS - 2, True)
    for h in range(HEADS):
        xb_ref[:, h * HD:(h + 1) * HD] = ob_ref[h]
    _project_and_norm(x_ref, xb_ref, wo_ref, g_ref, b_ref, y_ref, alpha, NORM_SLAB)

    @pl.when(i % tiles_per_seq == tiles_per_seq - 1)
    def _():
        so_ref[0] = s_ref[...]


def _hgrn_prep(hg_lower, wi, ng, wo):
    n_layers = hg_lower.shape[0]
    lower = hg_lower.reshape(n_layers, HEADS, 1, HD)
    wih = (wi.reshape(n_layers, D_MODEL, 4, HEADS, HD).transpose(0, 3, 1, 2, 4)
           .reshape(n_layers, HEADS, D_MODEL, 4 * HD))
    return lower, wih.astype(BF16), ng.reshape(n_layers, 1, HD), wo.astype(BF16)


def _hgrn_prompt_call(x, prep, layer, g, b, *, seq_len, alpha):
    lower, wih, ng, wo = prep
    T = x.shape[0]
    tm = TM_PROMPT
    tiles_per_seq = seq_len // tm
    n_seq = T // seq_len
    row = pl.BlockSpec((tm, D_MODEL), lambda i: (i, 0))
    specs_args = [_layer_operand(w) for w in (lower, (wih, layer), (ng, layer), (wo, layer), g, b)]
    head_buf = pltpu.VMEM((2, tm, HD), F32)
    scratch = [pltpu.VMEM((tm, D_MODEL), BF16),
               pltpu.VMEM((2, tm, 4 * HD), F32),
               head_buf,
               pltpu.VMEM((2, tm // SUBLANES, HD), F32),
               pltpu.VMEM((2, tm // SUBLANES, HD), F32),
               head_buf, head_buf, head_buf, head_buf, head_buf,
               head_buf,
               pltpu.VMEM((2, tm // BLK, BLK, BLK), BF16),
               head_buf,
               pltpu.VMEM((2, tm // BLK, HD, HD), F32),
               pltpu.VMEM((2, tm // BLK, HD, HD), F32),
               pltpu.VMEM((2, tm // BLK, HD, HD), F32),
               pltpu.VMEM((HEADS, tm, HD), BF16),
               pltpu.VMEM((HEADS, HD, HD), F32)]
    y, so = pl.pallas_call(
        functools.partial(_hgrn_prompt_kernel, tm=tm, layer=layer,
                          tiles_per_seq=tiles_per_seq, alpha=alpha),
        grid=(T // tm,),
        in_specs=[row] + [s for s, _ in specs_args],
        out_specs=[row, pl.BlockSpec((1, HEADS, HD, HD), lambda i: (i // tiles_per_seq, 0, 0, 0))],
        out_shape=[jax.ShapeDtypeStruct((T, D_MODEL), F32),
                   jax.ShapeDtypeStruct((n_seq, HEADS, HD, HD), F32)],
        scratch_shapes=scratch,
        compiler_params=_params(),
        name="hgrn_prompt",
    )(x, *[a for _, a in specs_args])
    return y, so


def _hgrn_sample_front_kernel(x_ref, lower_ref, wi_ref,
                              oi_ref, qe_ref, v_ref, sg_ref, kdt_ref, et_ref,
                              xb_ref, p_ref, cb_ref, kk_ref, qs_ref, a_ref, last_ref,
                              *, tm, chunk, layer):
    assert chunk == SUBLANES
    xb_ref[...] = x_ref[...].astype(BF16)
    groups = tm // SUBLANES
    n_blocks = tm // BLK
    same = _same_chunk_mask(chunk)

    def head(h, _):
        p_ref[...] = _dot(xb_ref[...], wi_ref[h])
        safe, totals = _hgrn_front(p_ref, lower_ref, h, layer, tm, chunk,
                                   cb_ref, None, None, kk_ref, v_ref.at[h], sg_ref.at[h],
                                   qs_ref, qe_ref.at[h])
        for k in range(SUBLANES):
            last_ref[pl.ds(k, groups, stride=SUBLANES), :] = totals
        for nb in range(n_blocks):
            r0 = nb * BLK
            rows = pl.ds(r0, BLK)
            cum = cb_ref[rows, :]
            kk = kk_ref[rows, :]
            s = _dot_nt(qe_ref[h, rows, :].astype(BF16), _inverse_decayed_keys(kk, cum).astype(BF16))
            a_ref[nb] = jnp.where(same, s, 0.0).astype(BF16)
            last = last_ref[rows, :]
            kdt_ref[h, :, r0:r0 + BLK] = (kk * jnp.exp(last - cum)).T
            et_ref[h, :, r0:r0 + BLK] = jnp.exp(last).T

        @pl.when(jnp.logical_not(safe))
        def _():
            def fix(nb, _):
                rows = pl.ds(pl.multiple_of(nb * BLK, BLK), BLK)
                a_ref[nb] = _robust_same_chunk_scores(rows, chunk, cb_ref, kk_ref, qs_ref).astype(BF16)
                return 0
            lax.fori_loop(0, n_blocks, fix, 0)

        for nb in range(n_blocks):
            rows = pl.ds(nb * BLK, BLK)
            oi_ref[h, rows, :] = _dot(a_ref[nb], v_ref[h, rows, :].astype(BF16))
        return 0

    lax.fori_loop(0, HEADS, head, 0)


def _hgrn_sample_state_kernel(qe_ref, oi_ref, v_ref, sg_ref, kdt_ref, et_ref, s0_ref, ng_ref,
                              *rest, chunk):
    op_ref, so_ref = rest[-2:]
    row_seq = lax.broadcasted_iota(jnp.int32, (BLK, HD), 0) >> (chunk.bit_length() - 1)

    def head(h, _):
        kdt = kdt_ref[h].astype(BF16)
        et = et_ref[h]
        v = v_ref[h]
        for sq in range(BLK // chunk):
            rows = slice(sq * chunk, (sq + 1) * chunk)
            s0 = s0_ref[sq, h]
            o = oi_ref[h, rows, :] + _dot(qe_ref[h, rows, :].astype(BF16), s0.astype(BF16))
            op_ref[h, rows, :] = _rms_gate(o, ng_ref[...], sg_ref[h, rows, :])
            vm = jnp.where(row_seq == sq, v, 0.0).astype(BF16)
            e_col = jnp.broadcast_to(et[:, sq * chunk:sq * chunk + 1], (HD, HD))
            so_ref[sq, h] = e_col * s0 + _dot(kdt, vm)
        return 0

    lax.fori_loop(0, HEADS, head, 0)


def _hgrn_sample_out_kernel(x_ref, op_ref, wo_ref, g_ref, b_ref, y_ref, ob_ref, *, alpha):
    for h in range(HEADS):
        ob_ref[:, h * HD:(h + 1) * HD] = op_ref[h].astype(BF16)
    f = _dot(ob_ref[...], wo_ref[...])
    y_ref[...] = _deepnorm(x_ref[...], f, g_ref[...], b_ref[...], alpha)


def _hgrn_sample_call(x, s0, so_prev, prep, layer, g, b, *, seq_len, alpha):
    lower, wih, ng, wo = prep
    T = x.shape[0]
    chunk = seq_len
    heads_rows = jax.ShapeDtypeStruct((HEADS, T, HD), F32)
    heads_cols = jax.ShapeDtypeStruct((HEADS, HD, T), F32)
    full = lambda s: pl.BlockSpec(s, lambda i: (0,) * len(s))
    head_buf = pltpu.VMEM((T, HD), F32)
    specs_args = [_layer_operand(w) for w in (lower, (wih, layer))]
    oi, qe, v, sg, kdt, et = pl.pallas_call(
        functools.partial(_hgrn_sample_front_kernel, tm=T, chunk=chunk, layer=layer),
        grid=(1,),
        in_specs=[full(x.shape)] + [s for s, _ in specs_args],
        out_specs=[full(heads_rows.shape)] * 4 + [full(heads_cols.shape)] * 2,
        out_shape=[heads_rows] * 4 + [heads_cols] * 2,
        scratch_shapes=[pltpu.VMEM((T, D_MODEL), BF16), pltpu.VMEM((T, 4 * HD), F32),
                        head_buf, head_buf, head_buf,
                        pltpu.VMEM((T // BLK, BLK, BLK), BF16), head_buf],
        compiler_params=_params(),
        name="hgrn_sample_front",
    )(x, *[a for _, a in specs_args])
    seqs = BLK // chunk
    by_rows = pl.BlockSpec((HEADS, BLK, HD), lambda i: (0, i, 0))
    by_cols = pl.BlockSpec((HEADS, HD, BLK), lambda i: (0, 0, i))
    state = pl.BlockSpec((None, seqs, HEADS, HD, HD), lambda i: (layer, i, 0, 0, 0))
    ng_spec, ng_arr = _layer_operand((ng, layer))
    in_specs = [by_rows, by_rows, by_rows, by_rows, by_cols, by_cols, state, ng_spec]
    args = [qe, oi, v, sg, kdt, et, s0, ng_arr]
    in_specs.append(pl.BlockSpec(memory_space=pl.ANY))
    args.append(so_prev)
    aliases = {len(args) - 1: 1}
    op, so = pl.pallas_call(
        functools.partial(_hgrn_sample_state_kernel, chunk=chunk),
        grid=(T // BLK,),
        in_specs=in_specs,
        out_specs=[by_rows, state],
        out_shape=[heads_rows, jax.ShapeDtypeStruct(s0.shape, F32)],
        input_output_aliases=aliases,
        compiler_params=_params(),
        name="hgrn_sample_state",
    )(*args)
    specs_args = [_layer_operand(w) for w in ((wo, layer), g, b)]
    y = pl.pallas_call(
        functools.partial(_hgrn_sample_out_kernel, alpha=alpha),
        grid=(1,),
        in_specs=[full(x.shape), full(op.shape)] + [s for s, _ in specs_args],
        out_specs=full(x.shape),
        out_shape=jax.ShapeDtypeStruct(x.shape, F32),
        scratch_shapes=[pltpu.VMEM((T, D_MODEL), BF16)],
        compiler_params=_params(),
        name="hgrn_sample_out",
    )(x, op, *[a for _, a in specs_args])
    return y, so


def _to_time_major(a):
    return jnp.transpose(a, (1, 0, 2)).reshape(-1, a.shape[-1])


def _from_time_major(a, batch):
    return jnp.transpose(a.reshape(-1, batch, a.shape[-1]), (1, 0, 2))


def kernel(x_prompt, x_sample, state_rglru_h, state_rglru_conv, state_hgrn_s, state_ffn_conv,
           ln_g, ln_b, rg_w_in, rg_conv_w, rg_conv_b, rg_gate_w, rg_gate_b, rg_lambda, rg_w_out,
           hg_lower, hg_w_in, hg_norm_g, hg_w_out, ffn_w_in, ffn_conv_w, ffn_conv_b, ffn_w_out):
    depth = ln_g.shape[0]
    alpha = (2.0 * depth) ** 0.25
    pb, pl_len, _ = x_prompt.shape
    sb, sl_len, _ = x_sample.shape

    n_rg, n_ffn = rg_w_in.shape[0], ffn_w_in.shape[0]
    lng = ln_g.reshape(depth * 2, 1, D_MODEL)
    lnb = ln_b.reshape(depth * 2, 1, D_MODEL)
    rg_wi = _rglru_in_proj(rg_w_in)
    rg_wg = jnp.concatenate([rg_gate_w[:, 0], rg_gate_w[:, 1]], axis=-1).astype(BF16)
    rg_wo = rg_w_out.astype(BF16)
    rg_cb = rg_conv_b.reshape(n_rg, 1, D_MODEL)
    rg_lam = rg_lambda.reshape(n_rg, 1, D_MODEL)
    hg = _hgrn_prep(hg_lower, hg_w_in, hg_norm_g, hg_w_out)
    ffn_wi = ffn_w_in.astype(BF16)
    ffn_wo = ffn_w_out.astype(BF16)
    ffn_cb = ffn_conv_b.reshape(n_ffn, 1, D_FF)
    rg_cst = jnp.transpose(state_rglru_conv, (0, 2, 1, 3)).reshape(n_rg, -1, D_MODEL)
    ffn_cst = jnp.transpose(state_ffn_conv, (0, 2, 1, 3)).reshape(n_ffn, -1, D_FF)

    xp = x_prompt.reshape(pb * pl_len, D_MODEL)
    xs = _to_time_major(x_sample)
    p_h, p_rc, p_s, p_fc = [], [], [], []
    s_h, s_rc, s_fc = [], [], []
    s_s = jnp.zeros(state_hgrn_s.shape, F32)
    for i in range(depth):
        j = i // 2
        g0, b0 = (lng, 2 * i), (lnb, 2 * i)
        if i % 2 == 0:
            w = ((rg_wi, j), (rg_conv_w, j), (rg_cb, j), (rg_wg, j), (rg_gate_b, j), (rg_lam, j),
                 (rg_wo, j), g0, b0)
            xp, ho, co = _rglru_call(xp, None, None, *w, seq_len=pl_len, alpha=alpha)
            p_h.append(ho[:, 0])
            p_rc.append(co)
            xs, ho, co = _rglru_call(xs, (state_rglru_h, j), (rg_cst, j), *w,
                                     seq_len=sl_len, alpha=alpha)
            s_h.append(ho)
            s_rc.append(co)
        else:
            xp, so = _hgrn_prompt_call(xp, hg, j, g0, b0, seq_len=pl_len, alpha=alpha)
            p_s.append(so)
            xs_bm = _from_time_major(xs, sb).reshape(sb * sl_len, D_MODEL)
            ys_bm, s_s = _hgrn_sample_call(xs_bm, state_hgrn_s, s_s, hg, j, g0, b0,
                                           seq_len=sl_len, alpha=alpha)
            xs = _to_time_major(ys_bm.reshape(sb, sl_len, D_MODEL))
        w = ((ffn_wi, i), (ffn_conv_w, i), (ffn_cb, i), (ffn_wo, i), (lng, 2 * i + 1), (lnb, 2 * i + 1))
        xp, fo = _ffn_call(xp, None, *w, seq_len=pl_len, alpha=alpha)
        p_fc.append(fo)
        xs, fo = _ffn_call(xs, (ffn_cst, i), *w, seq_len=sl_len, alpha=alpha)
        s_fc.append(fo)

    def stacked_from_time_major(parts):
        a = jnp.stack(parts)
        a = a.reshape(a.shape[0], -1, sb, a.shape[-1])
        return jnp.transpose(a, (0, 2, 1, 3))

    return (xp.reshape(x_prompt.shape), _from_time_major(xs, sb),
            jnp.stack(p_h), jnp.stack(p_rc), jnp.stack(p_s), jnp.stack(p_fc),
            jnp.stack(s_h), stacked_from_time_major(s_rc), s_s, stacked_from_time_major(s_fc))
```

```python
import functools

import jax
import jax.numpy as jnp
from jax import lax
from jax.experimental import pallas as pl
from jax.experimental.pallas import tpu as pltpu

F32 = jnp.float32
BF16 = jnp.bfloat16

D_MODEL = 1024
D_FF = 2816
HEADS = 8
HD = 128
RG_CONV = 4
FFN_CONV = 3
RG_C = 8.0
LN_EPS = 1e-5
RMS_EPS = 1e-6
F_FLOOR = 1e-30
SUBLANES = 8
LANES = 128
BLK = 128
PROMPT_CHUNK = 64
SAFE_DECAY = 80.0
FF_TILE = 256
TM_PROMPT = 1024
TM_FFN = 1024
NORM_SLAB = 256
VMEM_LIMIT = 56 * 1024 * 1024


def _dot(a, b):
    return jnp.dot(a, b, preferred_element_type=F32)


def _dot_nt(a, b):
    return lax.dot_general(a, b, (((1,), (1,)), ((), ())), preferred_element_type=F32)


def _sigmoid(x):
    return 0.5 * jnp.tanh(0.5 * x) + 0.5


def _resident(shape):
    nd = len(shape)
    return pl.BlockSpec(shape, lambda i: (0,) * nd, pipeline_mode=pl.Buffered(1))


def _layer_shape(w):
    return w[0].shape[1:] if isinstance(w, tuple) else w.shape


def _layer_operand(w):
    if not isinstance(w, tuple):
        return _resident(w.shape), w
    arr, layer = w
    tail = (0,) * (arr.ndim - 1)
    spec = pl.BlockSpec((None,) + arr.shape[1:], lambda i: (layer,) + tail,
                        pipeline_mode=pl.Buffered(1))
    return spec, arr


def _deepnorm(x, f, g, b, alpha):
    z = alpha * x + f
    mu = jnp.mean(z, axis=-1, keepdims=True)
    zc = z - mu
    var = jnp.mean(zc * zc, axis=-1, keepdims=True)
    return zc * lax.rsqrt(var + LN_EPS) * g + b


def _project_and_norm(x_ref, lhs_ref, wo_ref, g_ref, b_ref, y_ref, alpha, slab):
    rows = x_ref.shape[0]
    slab = min(slab, rows)
    for r0 in range(0, rows, slab):
        sl = pl.ds(r0, slab)
        f = _dot(lhs_ref[sl, :], wo_ref[...])
        y_ref[sl, :] = _deepnorm(x_ref[sl, :], f, g_ref[...], b_ref[...], alpha)


def _params(n_axes=1):
    return pltpu.CompilerParams(dimension_semantics=("arbitrary",) * n_axes,
                                vmem_limit_bytes=VMEM_LIMIT)


def _ffn_kernel(*refs, tm, step, tiles_per_seq, has_state, alpha):
    if has_state:
        (x_ref, st_ref, wi_ref, cw_ref, cb_ref, wo_ref, g_ref, b_ref,
         y_ref, so_ref, gbuf, hbuf) = refs
        carry = None
    else:
        (x_ref, wi_ref, cw_ref, cb_ref, wo_ref, g_ref, b_ref,
         y_ref, so_ref, gbuf, hbuf, carry) = refs
    halo = gbuf.shape[0] - tm
    xb = x_ref[...].astype(BF16)
    if not has_state:
        @pl.when(pl.program_id(0) % tiles_per_seq == 0)
        def _():
            carry[...] = jnp.zeros_like(carry)
    for j in range(D_FF // FF_TILE):
        c0 = j * FF_TILE
        g = _dot(xb, wi_ref[:, c0:c0 + FF_TILE])
        u = _dot(xb, wi_ref[:, D_FF + c0:D_FF + c0 + FF_TILE])
        if has_state:
            for k in range(FFN_CONV - 1):
                gbuf[k * step:(k + 1) * step, :] = st_ref[:, k * D_FF + c0:k * D_FF + c0 + FF_TILE]
        else:
            gbuf[0:halo, :] = carry[:, c0:c0 + FF_TILE]
        gbuf[halo:halo + tm, :] = g
        g1 = gbuf[halo - step:halo - step + tm, :]
        g2 = gbuf[halo - 2 * step:halo - 2 * step + tm, :]
        cw = cw_ref[:, c0:c0 + FF_TILE]
        gc = g * cw[2:3] + g1 * cw[1:2] + g2 * cw[0:1] + cb_ref[:, c0:c0 + FF_TILE]
        hbuf[:, c0:c0 + FF_TILE] = (jax.nn.gelu(gc) * u).astype(BF16)
        if has_state:
            for k in range(FFN_CONV - 1):
                so_ref[:, k * D_FF + c0:k * D_FF + c0 + FF_TILE] = gbuf[tm + k * step:tm + (k + 1) * step, :]
        else:
            carry[:, c0:c0 + FF_TILE] = gbuf[tm:tm + halo, :]
            so_ref[0, :, c0:c0 + FF_TILE] = gbuf[halo + tm - 2:halo + tm, :]
    _project_and_norm(x_ref, hbuf, wo_ref, g_ref, b_ref, y_ref, alpha, NORM_SLAB)


def _ffn_call(x, state_tm, wi, cw, cb, wo, g, b, *, seq_len, alpha):
    T = x.shape[0]
    has_state = state_tm is not None
    if has_state:
        st_shape = _layer_shape(state_tm)
        step = T // seq_len
        tm, halo, tiles_per_seq = T, (FFN_CONV - 1) * step, 1
    else:
        tm, step, halo, tiles_per_seq = TM_FFN, 1, SUBLANES, seq_len // TM_FFN
    n_tiles = T // tm
    n_seq = T // seq_len
    row = pl.BlockSpec((tm, D_MODEL), lambda i: (i, 0))
    operands = ([state_tm] if has_state else []) + [wi, cw, cb, wo, g, b]
    specs_args = [_layer_operand(w) for w in operands]
    in_specs = [row] + [s for s, _ in specs_args]
    args = [x] + [a for _, a in specs_args]
    scratch = [pltpu.VMEM((halo + tm, FF_TILE), F32), pltpu.VMEM((tm, D_FF), BF16)]
    if has_state:
        so_shape = jax.ShapeDtypeStruct(st_shape, F32)
        so_spec = pl.BlockSpec(st_shape, lambda i: (0, 0))
    else:
        so_shape = jax.ShapeDtypeStruct((n_seq, FFN_CONV - 1, D_FF), F32)
        so_spec = pl.BlockSpec((1, FFN_CONV - 1, D_FF), lambda i: (i // tiles_per_seq, 0, 0))
        scratch.append(pltpu.VMEM((halo, D_FF), F32))
    return pl.pallas_call(
        functools.partial(_ffn_kernel, tm=tm, step=step, tiles_per_seq=tiles_per_seq,
                          has_state=has_state, alpha=alpha),
        grid=(n_tiles,),
        in_specs=in_specs,
        out_specs=[row, so_spec],
        out_shape=[jax.ShapeDtypeStruct((T, D_MODEL), F32), so_shape],
        scratch_shapes=scratch,
        compiler_params=_params(),
        name="ffn_state" if has_state else "ffn_prompt",
    )(*args)


def _rglru_gates(xc, gt, h, bg_ref, log_sig_lam):
    c0 = h * HD
    bg = bg_ref[:, c0:c0 + HD]
    r = _sigmoid(gt[:, 0:HD] + bg[0:1])
    ig = _sigmoid(gt[:, HD:2 * HD] + bg[1:2])
    a = jnp.exp(RG_C * r * log_sig_lam[:, c0:c0 + HD])
    return a, jnp.sqrt(jnp.maximum(1.0 - a * a, 0.0)) * ig * xc


def _rglru_state_kernel(x_ref, h0_ref, cst_ref, wi_ref, cw_ref, cb_ref, wg_ref, bg_ref, lam_ref,
                        wo_ref, g_ref, b_ref, y_ref, ho_ref, co_ref, xbuf, ybuf, *, tm, step, alpha):
    halo = xbuf.shape[0] - tm
    xb = x_ref[...].astype(BF16)
    for k in range(RG_CONV - 1):
        xbuf[k * step:(k + 1) * step, :] = cst_ref[:, k * D_MODEL:(k + 1) * D_MODEL]
    log_sig_lam = jax.nn.log_sigmoid(lam_ref[...])
    for h in range(HEADS):
        c0 = h * HD
        proj = _dot(xb, wi_ref[:, 2 * c0:2 * c0 + 2 * HD])
        gate = jax.nn.gelu(proj[:, 0:HD])
        xbuf[halo:halo + tm, c0:c0 + HD] = proj[:, HD:2 * HD]
        cw = cw_ref[:, c0:c0 + HD]
        xc = cb_ref[:, c0:c0 + HD] + proj[:, HD:2 * HD] * cw[RG_CONV - 1:RG_CONV]
        for j in range(RG_CONV - 1):
            back = (RG_CONV - 1 - j) * step
            xc = xc + xbuf[halo - back:halo - back + tm, c0:c0 + HD] * cw[j:j + 1]
        a, bv = _rglru_gates(xc, _dot(xc.astype(BF16), wg_ref[h]), h, bg_ref, log_sig_lam)
        hh = h0_ref[:, c0:c0 + HD]
        for t in range(tm // step):
            rows = slice(t * step, (t + 1) * step)
            hh = a[rows] * hh + bv[rows]
            ybuf[rows, c0:c0 + HD] = (hh * gate[rows]).astype(BF16)
        ho_ref[:, c0:c0 + HD] = hh
    for k in range(RG_CONV - 1):
        co_ref[:, k * D_MODEL:(k + 1) * D_MODEL] = xbuf[tm + k * step:tm + (k + 1) * step, :]
    _project_and_norm(x_ref, ybuf, wo_ref, g_ref, b_ref, y_ref, alpha, NORM_SLAB)


def _rglru_prompt_kernel(x_ref, wi_ref, cw_ref, cb_ref, wg_ref, bg_ref, lam_ref, wo_ref, g_ref, b_ref,
                         y_ref, ho_ref, co_ref,
                         xbuf, gbuf, hbuf, tot, hin, sup, ybuf, ccarry, hcarry,
                         *, tm, tiles_per_seq, alpha):
    halo = xbuf.shape[1] - tm
    groups = tm // SUBLANES
    supers = groups // SUBLANES
    taps = RG_CONV - 1

    @pl.when(pl.program_id(0) % tiles_per_seq == 0)
    def _():
        ccarry[...] = jnp.zeros_like(ccarry)
        hcarry[...] = jnp.zeros_like(hcarry)

    xb = x_ref[...].astype(BF16)
    log_sig_lam = jax.nn.log_sigmoid(lam_ref[...])

    def project(h):
        c0 = h * HD
        proj = _dot(xb, wi_ref[:, 2 * c0:2 * c0 + 2 * HD])
        gbuf[:, c0:c0 + HD] = jax.nn.gelu(proj[:, 0:HD])
        xs = xbuf.at[h]
        xs[0:halo, :] = ccarry[h]
        xs[halo:halo + tm, :] = proj[:, HD:2 * HD]
        ccarry[h] = xs[tm:tm + halo, :]
        co_ref[0, :, c0:c0 + HD] = xs[halo + tm - taps:halo + tm, :]

    def conv(h):
        c0 = h * HD
        xs = xbuf.at[h]
        cw = cw_ref[:, c0:c0 + HD]
        cb = cb_ref[:, c0:c0 + HD]
        lock = [xs[pl.ds(halo - taps + m, groups, stride=SUBLANES), :]
                for m in range(SUBLANES + taps)]
        xc = []
        for k in range(SUBLANES):
            acc = cb + lock[k + taps] * cw[taps:taps + 1]
            for j in range(taps):
                acc = acc + lock[k + j] * cw[j:j + 1]
            xc.append(acc)
        xc = jnp.concatenate(xc, axis=0)
        return xc, _dot(xc.astype(BF16), wg_ref[h])

    def scan(h, xc, gt):
        c0 = h * HD
        a, bv = _rglru_gates(xc, gt, h, bg_ref, log_sig_lam)
        pa, pb = a[0:groups], bv[0:groups]
        a_in, b_in = [pa], [pb]
        for k in range(1, SUBLANES):
            ak = a[k * groups:(k + 1) * groups]
            pb = ak * pb + bv[k * groups:(k + 1) * groups]
            pa = ak * pa
            a_in.append(pa)
            b_in.append(pb)
        tot[h, 0] = pa
        tot[h, 1] = pb
        qa = tot[h, 0, pl.ds(0, supers, stride=SUBLANES), :]
        qb = tot[h, 1, pl.ds(0, supers, stride=SUBLANES), :]
        a_sup, b_sup = [qa], [qb]
        for j in range(1, SUBLANES):
            aj = tot[h, 0, pl.ds(j, supers, stride=SUBLANES), :]
            qb = aj * qb + tot[h, 1, pl.ds(j, supers, stride=SUBLANES), :]
            qa = aj * qa
            a_sup.append(qa)
            b_sup.append(qb)
        hc = hcarry[h]
        for s in range(supers):
            sup[h, pl.ds(s, 1), :] = hc
            hc = qa[s:s + 1] * hc + qb[s:s + 1]
        hcarry[h] = hc
        ho_ref[0, :, c0:c0 + HD] = hc
        h_sup = sup[h]
        hin[h, pl.ds(0, supers, stride=SUBLANES), :] = h_sup
        for j in range(1, SUBLANES):
            hin[h, pl.ds(j, supers, stride=SUBLANES), :] = a_sup[j - 1] * h_sup + b_sup[j - 1]
        h_grp = hin[h]
        for k in range(SUBLANES):
            hbuf[h, pl.ds(k, groups, stride=SUBLANES), :] = a_in[k] * h_grp + b_in[k]

    convs = {}
    for h in range(HEADS + 2):
        if h < HEADS:
            project(h)
        if 1 <= h <= HEADS:
            convs[h - 1] = conv(h - 1)
        if h >= 2:
            scan(h - 2, *convs.pop(h - 2))
    for h in range(HEADS):
        c0 = h * HD
        ybuf[:, c0:c0 + HD] = (hbuf[h] * gbuf[:, c0:c0 + HD]).astype(BF16)
    _project_and_norm(x_ref, ybuf, wo_ref, g_ref, b_ref, y_ref, alpha, NORM_SLAB)


def _rglru_in_proj(w_in):
    lead = w_in.shape[:-1]
    w = w_in.astype(BF16).reshape(*lead, 2, HEADS, HD)
    return jnp.swapaxes(w, -3, -2).reshape(*lead, 2 * HEADS * HD)


def _rglru_call(x, h0, cst_tm, wi, cw, cb, wg, bg, lam, wo, g, b, *, seq_len, alpha):
    T = x.shape[0]
    has_state = h0 is not None
    n_seq = T // seq_len
    operands = ([h0, cst_tm] if has_state else []) + [wi, cw, cb, wg, bg, lam, wo, g, b]
    specs_args = [_layer_operand(w) for w in operands]
    args = [x] + [a for _, a in specs_args]
    if has_state:
        h0_shape, cst_shape = _layer_shape(h0), _layer_shape(cst_tm)
        tm, halo = T, (RG_CONV - 1) * (T // seq_len)
        body = functools.partial(_rglru_state_kernel, tm=tm, step=T // seq_len, alpha=alpha)
        row = pl.BlockSpec((tm, D_MODEL), lambda i: (i, 0))
        out_shape = [jax.ShapeDtypeStruct((T, D_MODEL), F32),
                     jax.ShapeDtypeStruct(h0_shape, F32),
                     jax.ShapeDtypeStruct(cst_shape, F32)]
        out_specs = [row, pl.BlockSpec(h0_shape, lambda i: (0, 0)),
                     pl.BlockSpec(cst_shape, lambda i: (0, 0))]
        scratch = [pltpu.VMEM((halo + tm, D_MODEL), F32), pltpu.VMEM((tm, D_MODEL), BF16)]
    else:
        tm, halo, tiles_per_seq = TM_PROMPT, SUBLANES, seq_len // TM_PROMPT
        groups = tm // SUBLANES
        body = functools.partial(_rglru_prompt_kernel, tm=tm, tiles_per_seq=tiles_per_seq, alpha=alpha)
        row = pl.BlockSpec((tm, D_MODEL), lambda i: (i, 0))
        out_shape = [jax.ShapeDtypeStruct((T, D_MODEL), F32),
                     jax.ShapeDtypeStruct((n_seq, 1, D_MODEL), F32),
                     jax.ShapeDtypeStruct((n_seq, RG_CONV - 1, D_MODEL), F32)]
        out_specs = [row,
                     pl.BlockSpec((1, 1, D_MODEL), lambda i: (i // tiles_per_seq, 0, 0)),
                     pl.BlockSpec((1, RG_CONV - 1, D_MODEL), lambda i: (i // tiles_per_seq, 0, 0))]
        scratch = [pltpu.VMEM((HEADS, halo + tm, HD), F32),
                   pltpu.VMEM((tm, D_MODEL), F32),
                   pltpu.VMEM((HEADS, tm, HD), F32),
                   pltpu.VMEM((HEADS, 2, groups, HD), F32),
                   pltpu.VMEM((HEADS, groups, HD), F32),
                   pltpu.VMEM((HEADS, groups // SUBLANES, HD), F32),
                   pltpu.VMEM((tm, D_MODEL), BF16),
                   pltpu.VMEM((HEADS, halo, HD), F32),
                   pltpu.VMEM((HEADS, 1, HD), F32)]
    in_specs = [row] + [s for s, _ in specs_args]
    n_tiles = T // tm
    return pl.pallas_call(
        body,
        grid=(n_tiles,),
        in_specs=in_specs,
        out_specs=out_specs,
        out_shape=out_shape,
        scratch_shapes=scratch,
        compiler_params=_params(),
        name="rglru_state" if has_state else "rglru_prompt",
    )(*args)


def _split3(x):
    hi = x.astype(BF16)
    r = x - hi.astype(F32)
    mid = r.astype(BF16)
    lo = (r - mid.astype(F32)).astype(BF16)
    return hi, mid, lo


def _lower_bound(lower_ref, h, layer):
    rows = [lower_ref[n, h] for n in range(lower_ref.shape[0])]
    m = functools.reduce(jnp.maximum, rows)
    es = [jnp.exp(r - m) for r in rows]
    tot = functools.reduce(lambda p, q: p + q, es)
    sm = [e / tot for e in es]
    cs = functools.reduce(lambda p, q: p + q, sm[:layer + 1])
    return jnp.maximum(cs - sm[0], 0.0)


def _chunk_cumsum(cb_ref, tb_ref, eb_ref, tm, chunk):
    groups = tm // SUBLANES
    p = cb_ref[pl.ds(0, groups, stride=SUBLANES), :]
    for k in range(1, SUBLANES):
        p = p + cb_ref[pl.ds(k, groups, stride=SUBLANES), :]
        cb_ref[pl.ds(k, groups, stride=SUBLANES), :] = p
    if chunk > SUBLANES:
        gc = chunk // SUBLANES
        nch = tm // chunk
        tb_ref[...] = p
        e = jnp.zeros((nch, HD), F32)
        eb_ref[pl.ds(0, nch, stride=gc), :] = e
        for j in range(1, gc):
            e = e + tb_ref[pl.ds(j - 1, nch, stride=gc), :]
            eb_ref[pl.ds(j, nch, stride=gc), :] = e
        off = eb_ref[...]
        for k in range(SUBLANES):
            cb_ref[pl.ds(k, groups, stride=SUBLANES), :] = (
                cb_ref[pl.ds(k, groups, stride=SUBLANES), :] + off)
    return p


def _hgrn_front(part, lower_ref, h, layer, tm, chunk,
                cb_ref, tb_ref, eb_ref, kk_ref, v_ref, sg_ref, qs_ref, qe_ref):
    lb = _lower_bound(lower_ref, h, layer)
    fg = lb + (1.0 - lb) * _sigmoid(part(1))
    cb_ref[...] = jnp.log(jnp.maximum(fg, F_FLOOR))
    kk_ref[...] = 1.0 - fg
    q = part(0)
    qs = q * _sigmoid(q) * (HD ** -0.5)
    v_ref[...] = part(2)
    gg = part(3)
    sg_ref[...] = gg * _sigmoid(gg)
    totals = _chunk_cumsum(cb_ref, tb_ref, eb_ref, tm, chunk)
    cum = cb_ref[...]
    qs_ref[...] = qs
    qe_ref[...] = qs * jnp.exp(cum)
    safe = jnp.max(-cum) <= SAFE_DECAY
    return safe, totals


def _same_chunk_mask(chunk):
    ri = lax.broadcasted_iota(jnp.int32, (BLK, BLK), 0)
    ci = lax.broadcasted_iota(jnp.int32, (BLK, BLK), 1)
    lc = chunk.bit_length() - 1
    return ((ri >> lc) == (ci >> lc)) & (ci <= ri)


def _inverse_decayed_keys(kk, cum):
    return kk * jnp.exp(jnp.minimum(-cum, SAFE_DECAY))


def _robust_same_chunk_scores(rows, chunk, cb_ref, kk_ref, qs_ref):
    ri = lax.broadcasted_iota(jnp.int32, (BLK, BLK), 0)
    ci = lax.broadcasted_iota(jnp.int32, (BLK, BLK), 1)
    cum = cb_ref[rows, :]
    qs = qs_ref[rows, :]
    kk = kk_ref[rows, :]
    hi, mid, lo = _split3(cum)
    acc = jnp.where(ri == ci, _dot_nt(qs.astype(BF16), kk.astype(BF16)), 0.0)
    for lvl in range(chunk.bit_length() - 1):
        m = 1 << lvl
        pivot = ((ri >> (lvl + 1)) << (lvl + 1)) + (m - 1)
        sel = jnp.where(ci == pivot, 1.0, 0.0).astype(BF16)
        ref_cum = _dot(sel, hi) + _dot(sel, mid) + _dot(sel, lo)
        e = jnp.exp(-jnp.abs(cum - ref_cum))
        keep = (((ri >> (lvl + 1)) == (ci >> (lvl + 1)))
                & ((ri & (2 * m - 1)) >= m) & ((ci & (2 * m - 1)) < m))
        s = _dot_nt((qs * e).astype(BF16), (kk * e).astype(BF16))
        acc = acc + jnp.where(keep, s, 0.0)
    return acc


def _rms_gate(o, ng, sg):
    return o * lax.rsqrt(jnp.mean(o * o, axis=-1, keepdims=True) + RMS_EPS) * ng * sg


def _hgrn_prompt_kernel(x_ref, lower_ref, wi_ref, ng_ref, wo_ref, g_ref, b_ref, y_ref, so_ref,
                        xb_ref, p_ref, cb_ref, tb_ref, eb_ref, kk_ref, v_ref, sg_ref, qs_ref,
                        qe_ref, kd_ref, a_ref, oh_ref, ds_ref, ec_ref, st_ref, ob_ref, s_ref,
                        *, tm, layer, tiles_per_seq, alpha):
    chunk = PROMPT_CHUNK
    assert BLK == 2 * chunk
    n_blocks = tm // BLK
    i = pl.program_id(0)

    @pl.when(i % tiles_per_seq == 0)
    def _():
        s_ref[...] = jnp.zeros_like(s_ref)

    xb_ref[...] = x_ref[...].astype(BF16)
    second = lax.broadcasted_iota(jnp.int32, (BLK, HD), 0) >= chunk
    ri = lax.broadcasted_iota(jnp.int32, (BLK, BLK), 0)
    ci = lax.broadcasted_iota(jnp.int32, (BLK, BLK), 1)
    cross = (ri >= chunk) & (ci < chunk)
    same = _same_chunk_mask(chunk)

    def project(h, slot):
        p_ref[slot] = _dot(xb_ref[...], wi_ref[h])

    def views(slot):
        return tuple(r.at[slot] for r in (cb_ref, tb_ref, eb_ref, kk_ref, v_ref, sg_ref, qs_ref,
                                          qe_ref, kd_ref, a_ref, oh_ref, ds_ref, ec_ref))

    def scores(h, slot):
        cb, tb, eb, kk_s, v_s, sg_s, qs_s, qe_s, kd_s, a_s, _, ds_s, ec_s = views(slot)
        safe, _ = _hgrn_front(lambda k: p_ref[slot, :, k * HD:(k + 1) * HD], lower_ref, h, layer,
                              tm, chunk, cb, tb, eb, kk_s, v_s, sg_s, qs_s, qe_s)
        for nb in range(n_blocks):
            rows = pl.ds(nb * BLK, BLK)
            cum = cb[rows, :]
            kk = kk_s[rows, :]
            last0 = cb[pl.ds(nb * BLK + chunk - 1, 1), :]
            last1 = cb[pl.ds(nb * BLK + BLK - 1, 1), :]
            kd = kk * jnp.exp(jnp.where(second, last1, last0) - cum)
            kd_s[rows, :] = kd
            keys = jnp.concatenate([_inverse_decayed_keys(kk, cum), kd], axis=0).astype(BF16)
            s2 = _dot_nt(qe_s[rows, :].astype(BF16), keys)
            a = jnp.where(same, s2[:, 0:BLK], 0.0) + jnp.where(cross, s2[:, BLK:2 * BLK], 0.0)
            a_s[nb] = a.astype(BF16)
            k_blk = kd * jnp.where(second, 1.0, jnp.exp(last1))
            ds_s[nb] = _dot(k_blk.T.astype(BF16), v_s[rows, :].astype(BF16))
            ec_s[nb] = jnp.broadcast_to(jnp.exp(last0 + last1), (HD, HD)).T
        return safe

    def robust_scores(slot):
        cb, _, _, kk_s, _, _, qs_s, qe_s, kd_s, a_s, _, _, _ = views(slot)

        def fix(nb, _):
            rows = pl.ds(pl.multiple_of(nb * BLK, BLK), BLK)
            s1 = _dot_nt(qe_s[rows, :].astype(BF16), kd_s[rows, :].astype(BF16))
            a = _robust_same_chunk_scores(rows, chunk, cb, kk_s, qs_s)
            a_s[nb] = (a + jnp.where(cross, s1, 0.0)).astype(BF16)
            return 0
        lax.fori_loop(0, n_blocks, fix, 0)

    def advance_state(h, slot):
        ds_s, ec_s = views(slot)[11:13]
        state = s_ref[h]
        for nb in range(n_blocks):
            st_ref[slot, nb] = state
            state = ec_s[nb] * state + ds_s[nb]
        s_ref[h] = state

    def outputs(h, slot):
        cb, _, _, _, v_s, sg_s, _, qe_s, _, a_s, oh_s, _, _ = views(slot)
        for nb in range(n_blocks):
            rows = pl.ds(nb * BLK, BLK)
            last0 = cb[pl.ds(nb * BLK + chunk - 1, 1), :]
            q_blk = qe_s[rows, :] * jnp.where(second, jnp.exp(last0), 1.0)
            lhs = jnp.concatenate([a_s[nb], q_blk.astype(BF16)], axis=1)
            rhs = jnp.concatenate([v_s[rows, :].astype(BF16), st_ref[slot, nb].astype(BF16)], axis=0)
            oh_s[rows, :] = _dot(lhs, rhs)
        ob_ref[h] = _rms_gate(oh_s[...], ng_ref[...], sg_s[...]).astype(BF16)

    project(0, 0)

    def pair(h0, last):
        project(h0 + 1, 1)
        safe0 = scores(h0, 0)
        advance_state(h0, 0)
        outputs(h0, 0)
        if not last:
            project(h0 + 2, 0)
        safe1 = scores(h0 + 1, 1)
        advance_state(h0 + 1, 1)
        outputs(h0 + 1, 1)

        @pl.when(jnp.logical_not(jnp.logical_and(safe0, safe1)))
        def _():
            for slot in range(2):
                robust_scores(slot)
                outputs(h0 + slot, slot)
        return 0

    lax.fori_loop(0, HEADS // 2 - 1, lambda hh, _: pair(2 * hh, False), 0)
    pair(HEADS - 2, True)
    for h in range(HEADS):
        xb_ref[:, h * HD:(h + 1) * HD] = ob_ref[h]
    _project_and_norm(x_ref, xb_ref, wo_ref, g_ref, b_ref, y_ref, alpha, NORM_SLAB)

    @pl.when(i % tiles_per_seq == tiles_per_seq - 1)
    def _():
        so_ref[0] = s_ref[...]


def _hgrn_prep(hg_lower, wi, ng, wo):
    n_layers = hg_lower.shape[0]
    lower = hg_lower.reshape(n_layers, HEADS, 1, HD)
    wih = (wi.astype(BF16).reshape(n_layers, D_MODEL, 4, HEADS, HD).transpose(0, 3, 1, 2, 4)
           .reshape(n_layers, HEADS, D_MODEL, 4 * HD))
    return lower, wih, ng.reshape(n_layers, 1, HD), wo.astype(BF16)


def _hgrn_prompt_call(x, prep, layer, g, b, *, seq_len, alpha):
    lower, wih, ng, wo = prep
    T = x.shape[0]
    tm = TM_PROMPT
    tiles_per_seq = seq_len // tm
    n_seq = T // seq_len
    row = pl.BlockSpec((tm, D_MODEL), lambda i: (i, 0))
    specs_args = [_layer_operand(w) for w in (lower, (wih, layer), (ng, layer), (wo, layer), g, b)]
    head_buf = pltpu.VMEM((2, tm, HD), F32)
    scratch = [pltpu.VMEM((tm, D_MODEL), BF16),
               pltpu.VMEM((2, tm, 4 * HD), F32),
               head_buf,
               pltpu.VMEM((2, tm // SUBLANES, HD), F32),
               pltpu.VMEM((2, tm // SUBLANES, HD), F32),
               head_buf, head_buf, head_buf, head_buf, head_buf,
               head_buf,
               pltpu.VMEM((2, tm // BLK, BLK, BLK), BF16),
               head_buf,
               pltpu.VMEM((2, tm // BLK, HD, HD), F32),
               pltpu.VMEM((2, tm // BLK, HD, HD), F32),
               pltpu.VMEM((2, tm // BLK, HD, HD), F32),
               pltpu.VMEM((HEADS, tm, HD), BF16),
               pltpu.VMEM((HEADS, HD, HD), F32)]
    y, so = pl.pallas_call(
        functools.partial(_hgrn_prompt_kernel, tm=tm, layer=layer,
                          tiles_per_seq=tiles_per_seq, alpha=alpha),
        grid=(T // tm,),
        in_specs=[row] + [s for s, _ in specs_args],
        out_specs=[row, pl.BlockSpec((1, HEADS, HD, HD), lambda i: (i // tiles_per_seq, 0, 0, 0))],
        out_shape=[jax.ShapeDtypeStruct((T, D_MODEL), F32),
                   jax.ShapeDtypeStruct((n_seq, HEADS, HD, HD), F32)],
        scratch_shapes=scratch,
        compiler_params=_params(),
        name="hgrn_prompt",
    )(x, *[a for _, a in specs_args])
    return y, so


def _hgrn_sample_front_kernel(x_ref, lower_ref, wi_ref,
                              oi_ref, qe_ref, v_ref, sg_ref, kdt_ref, et_ref,
                              xb_ref, p_ref, cb_ref, kk_ref, qs_ref, a_ref, last_ref,
                              *, tm, chunk, layer):
    assert chunk == SUBLANES
    xb_ref[...] = x_ref[...].astype(BF16)
    groups = tm // SUBLANES
    n_blocks = tm // BLK
    same = _same_chunk_mask(chunk)

    def head(h, _):
        proj = _dot(xb_ref[...], wi_ref[h])
        for t in range(chunk):
            for k in range(4):
                p_ref[k, pl.ds(t, groups, stride=chunk), :] = (
                    proj[t * groups:(t + 1) * groups, k * HD:(k + 1) * HD])
        safe, totals = _hgrn_front(lambda k: p_ref[k], lower_ref, h, layer, tm, chunk,
                                   cb_ref, None, None, kk_ref, v_ref.at[h], sg_ref.at[h],
                                   qs_ref, qe_ref.at[h])
        for k in range(SUBLANES):
            last_ref[pl.ds(k, groups, stride=SUBLANES), :] = totals
        for nb in range(n_blocks):
            r0 = nb * BLK
            rows = pl.ds(r0, BLK)
            cum = cb_ref[rows, :]
            kk = kk_ref[rows, :]
            s = _dot_nt(qe_ref[h, rows, :].astype(BF16), _inverse_decayed_keys(kk, cum).astype(BF16))
            a_ref[nb] = jnp.where(same, s, 0.0).astype(BF16)
            last = last_ref[rows, :]
            kdt_ref[h, :, r0:r0 + BLK] = (kk * jnp.exp(last - cum)).T
            et_ref[h, :, r0:r0 + BLK] = jnp.exp(last).T

        @pl.when(jnp.logical_not(safe))
        def _():
            def fix(nb, _):
                rows = pl.ds(pl.multiple_of(nb * BLK, BLK), BLK)
                a_ref[nb] = _robust_same_chunk_scores(rows, chunk, cb_ref, kk_ref, qs_ref).astype(BF16)
                return 0
            lax.fori_loop(0, n_blocks, fix, 0)

        for nb in range(n_blocks):
            rows = pl.ds(nb * BLK, BLK)
            oi_ref[h, rows, :] = _dot(a_ref[nb], v_ref[h, rows, :].astype(BF16))
        return 0

    lax.fori_loop(0, HEADS, head, 0)


def _hgrn_sample_state_kernel(qe_ref, oi_ref, v_ref, sg_ref, kdt_ref, et_ref, s0_ref, ng_ref,
                              *rest, chunk):
    op_ref, so_ref = rest[-2:]
    row_seq = lax.broadcasted_iota(jnp.int32, (BLK, HD), 0) >> (chunk.bit_length() - 1)

    def head(h, _):
        kdt = kdt_ref[h].astype(BF16)
        et = et_ref[h]
        v = v_ref[h]
        for sq in range(BLK // chunk):
            rows = slice(sq * chunk, (sq + 1) * chunk)
            s0 = s0_ref[sq, h]
            o = oi_ref[h, rows, :] + _dot(qe_ref[h, rows, :].astype(BF16), s0.astype(BF16))
            op_ref[h, rows, :] = _rms_gate(o, ng_ref[...], sg_ref[h, rows, :])
            vm = jnp.where(row_seq == sq, v, 0.0).astype(BF16)
            e_col = jnp.broadcast_to(et[:, sq * chunk:sq * chunk + 1], (HD, HD))
            so_ref[sq, h] = e_col * s0 + _dot(kdt, vm)
        return 0

    lax.fori_loop(0, HEADS, head, 0)


def _hgrn_sample_out_kernel(x_ref, op_ref, wo_ref, g_ref, b_ref, y_ref, ob_ref, *, chunk, alpha):
    batch = x_ref.shape[0] // chunk
    for h in range(HEADS):
        for t in range(chunk):
            ob_ref[t * batch:(t + 1) * batch, h * HD:(h + 1) * HD] = (
                op_ref[h, pl.ds(t, batch, stride=chunk), :].astype(BF16))
    _project_and_norm(x_ref, ob_ref, wo_ref, g_ref, b_ref, y_ref, alpha, NORM_SLAB)


def _hgrn_sample_call(x, s0, so_prev, prep, layer, g, b, *, seq_len, alpha):
    lower, wih, ng, wo = prep
    T = x.shape[0]
    chunk = seq_len
    heads_rows = jax.ShapeDtypeStruct((HEADS, T, HD), F32)
    heads_cols = jax.ShapeDtypeStruct((HEADS, HD, T), F32)
    full = lambda s: pl.BlockSpec(s, lambda i: (0,) * len(s))
    head_buf = pltpu.VMEM((T, HD), F32)
    specs_args = [_layer_operand(w) for w in (lower, (wih, layer))]
    oi, qe, v, sg, kdt, et = pl.pallas_call(
        functools.partial(_hgrn_sample_front_kernel, tm=T, chunk=chunk, layer=layer),
        grid=(1,),
        in_specs=[full(x.shape)] + [s for s, _ in specs_args],
        out_specs=[full(heads_rows.shape)] * 4 + [full(heads_cols.shape)] * 2,
        out_shape=[heads_rows] * 4 + [heads_cols] * 2,
        scratch_shapes=[pltpu.VMEM((T, D_MODEL), BF16), pltpu.VMEM((4, T, HD), F32),
                        head_buf, head_buf, head_buf,
                        pltpu.VMEM((T // BLK, BLK, BLK), BF16), head_buf],
        compiler_params=_params(),
        name="hgrn_sample_front",
    )(x, *[a for _, a in specs_args])
    seqs = BLK // chunk
    by_rows = pl.BlockSpec((HEADS, BLK, HD), lambda i: (0, i, 0))
    by_cols = pl.BlockSpec((HEADS, HD, BLK), lambda i: (0, 0, i))
    state = pl.BlockSpec((None, seqs, HEADS, HD, HD), lambda i: (layer, i, 0, 0, 0))
    ng_spec, ng_arr = _layer_operand((ng, layer))
    in_specs = [by_rows, by_rows, by_rows, by_rows, by_cols, by_cols, state, ng_spec]
    args = [qe, oi, v, sg, kdt, et, s0, ng_arr]
    in_specs.append(pl.BlockSpec(memory_space=pl.ANY))
    args.append(so_prev)
    aliases = {len(args) - 1: 1}
    op, so = pl.pallas_call(
        functools.partial(_hgrn_sample_state_kernel, chunk=chunk),
        grid=(T // BLK,),
        in_specs=in_specs,
        out_specs=[by_rows, state],
        out_shape=[heads_rows, jax.ShapeDtypeStruct(s0.shape, F32)],
        input_output_aliases=aliases,
        compiler_params=_params(),
        name="hgrn_sample_state",
    )(*args)
    specs_args = [_layer_operand(w) for w in ((wo, layer), g, b)]
    y = pl.pallas_call(
        functools.partial(_hgrn_sample_out_kernel, chunk=chunk, alpha=alpha),
        grid=(1,),
        in_specs=[full(x.shape), full(op.shape)] + [s for s, _ in specs_args],
        out_specs=full(x.shape),
        out_shape=jax.ShapeDtypeStruct(x.shape, F32),
        scratch_shapes=[pltpu.VMEM((T, D_MODEL), BF16)],
        compiler_params=_params(),
        name="hgrn_sample_out",
    )(x, op, *[a for _, a in specs_args])
    return y, so


def _to_time_major(a):
    return jnp.transpose(a, (1, 0, 2)).reshape(-1, a.shape[-1])


def _from_time_major(a, batch):
    return jnp.transpose(a.reshape(-1, batch, a.shape[-1]), (1, 0, 2))


def kernel(x_prompt, x_sample, state_rglru_h, state_rglru_conv, state_hgrn_s, state_ffn_conv,
           ln_g, ln_b, rg_w_in, rg_conv_w, rg_conv_b, rg_gate_w, rg_gate_b, rg_lambda, rg_w_out,
           hg_lower, hg_w_in, hg_norm_g, hg_w_out, ffn_w_in, ffn_conv_w, ffn_conv_b, ffn_w_out):
    depth = ln_g.shape[0]
    alpha = (2.0 * depth) ** 0.25
    pb, pl_len, _ = x_prompt.shape
    sb, sl_len, _ = x_sample.shape

    n_rg, n_ffn = rg_w_in.shape[0], ffn_w_in.shape[0]
    lng = ln_g.reshape(depth * 2, 1, D_MODEL)
    lnb = ln_b.reshape(depth * 2, 1, D_MODEL)
    rg_wi = _rglru_in_proj(rg_w_in)
    rg_wg = jnp.concatenate([rg_gate_w[:, 0], rg_gate_w[:, 1]], axis=-1).astype(BF16)
    rg_wo = rg_w_out.astype(BF16)
    rg_cb = rg_conv_b.reshape(n_rg, 1, D_MODEL)
    rg_lam = rg_lambda.reshape(n_rg, 1, D_MODEL)
    hg = _hgrn_prep(hg_lower, hg_w_in, hg_norm_g, hg_w_out)
    ffn_wi = ffn_w_in.astype(BF16)
    ffn_wo = ffn_w_out.astype(BF16)
    ffn_cb = ffn_conv_b.reshape(n_ffn, 1, D_FF)
    rg_cst = state_rglru_conv.reshape(n_rg, sb, -1)
    ffn_cst = state_ffn_conv.reshape(n_ffn, sb, -1)

    xp = x_prompt.reshape(pb * pl_len, D_MODEL)
    xs = _to_time_major(x_sample)
    p_h, p_rc, p_s, p_fc = [], [], [], []
    s_h, s_rc, s_fc = [], [], []
    s_s = jnp.zeros(state_hgrn_s.shape, F32)
    for i in range(depth):
        j = i // 2
        g0, b0 = (lng, 2 * i), (lnb, 2 * i)
        if i % 2 == 0:
            w = ((rg_wi, j), (rg_conv_w, j), (rg_cb, j), (rg_wg, j), (rg_gate_b, j), (rg_lam, j),
                 (rg_wo, j), g0, b0)
            xp, ho, co = _rglru_call(xp, None, None, *w, seq_len=pl_len, alpha=alpha)
            p_h.append(ho[:, 0])
            p_rc.append(co)
            xs, ho, co = _rglru_call(xs, (state_rglru_h, j), (rg_cst, j), *w,
                                     seq_len=sl_len, alpha=alpha)
            s_h.append(ho)
            s_rc.append(co)
        else:
            xp, so = _hgrn_prompt_call(xp, hg, j, g0, b0, seq_len=pl_len, alpha=alpha)
            p_s.append(so)
            xs, s_s = _hgrn_sample_call(xs, state_hgrn_s, s_s, hg, j, g0, b0,
                                        seq_len=sl_len, alpha=alpha)
        w = ((ffn_wi, i), (ffn_conv_w, i), (ffn_cb, i), (ffn_wo, i), (lng, 2 * i + 1), (lnb, 2 * i + 1))
        xp, fo = _ffn_call(xp, None, *w, seq_len=pl_len, alpha=alpha)
        p_fc.append(fo)
        xs, fo = _ffn_call(xs, (ffn_cst, i), *w, seq_len=sl_len, alpha=alpha)
        s_fc.append(fo)

    def stacked_tails(parts, width):
        return jnp.stack(parts).reshape(len(parts), sb, width, -1)

    return (xp.reshape(x_prompt.shape), _from_time_major(xs, sb),
            jnp.stack(p_h), jnp.stack(p_rc), jnp.stack(p_s), jnp.stack(p_fc),
            jnp.stack(s_h), stacked_tails(s_rc, RG_CONV - 1), s_s, stacked_tails(s_fc, FFN_CONV - 1))
```

```python
import functools

import jax
import jax.numpy as jnp
from jax import lax
from jax.experimental import pallas as pl
from jax.experimental.pallas import tpu as pltpu

F32 = jnp.float32
BF16 = jnp.bfloat16

D_MODEL = 1024
D_FF = 2816
HEADS = 8
HD = 128
RG_CONV = 4
FFN_CONV = 3
RG_C = 8.0
LN_EPS = 1e-5
RMS_EPS = 1e-6
F_FLOOR = 1e-30
SUBLANES = 8
LANES = 128
BLK = 128
PROMPT_CHUNK = 64
SAFE_DECAY = 80.0
FF_TILE = 256
TM_PROMPT = 1024
TM_FFN = 1024
NORM_SLAB = 256
VMEM_LIMIT = 56 * 1024 * 1024


def _dot(a, b):
    return jnp.dot(a, b, preferred_element_type=F32)


def _dot_nt(a, b):
    return lax.dot_general(a, b, (((1,), (1,)), ((), ())), preferred_element_type=F32)


def _sigmoid(x):
    return 0.5 * jnp.tanh(0.5 * x) + 0.5


def _resident(shape):
    nd = len(shape)
    return pl.BlockSpec(shape, lambda i: (0,) * nd, pipeline_mode=pl.Buffered(1))


def _layer_shape(w):
    return w[0].shape[1:] if isinstance(w, tuple) else w.shape


def _layer_operand(w):
    if not isinstance(w, tuple):
        return _resident(w.shape), w
    arr, layer = w
    tail = (0,) * (arr.ndim - 1)
    spec = pl.BlockSpec((None,) + arr.shape[1:], lambda i: (layer,) + tail,
                        pipeline_mode=pl.Buffered(1))
    return spec, arr


def _deepnorm(x, f, g, b, alpha):
    z = alpha * x + f
    mu = jnp.mean(z, axis=-1, keepdims=True)
    zc = z - mu
    var = jnp.mean(zc * zc, axis=-1, keepdims=True)
    return zc * lax.rsqrt(var + LN_EPS) * g + b


def _project_and_norm(x_ref, lhs_ref, wo_ref, g_ref, b_ref, y_ref, alpha, slab):
    rows = x_ref.shape[0]
    slab = min(slab, rows)
    for r0 in range(0, rows, slab):
        sl = pl.ds(r0, slab)
        f = _dot(lhs_ref[sl, :], wo_ref[...])
        y_ref[sl, :] = _deepnorm(x_ref[sl, :], f, g_ref[...], b_ref[...], alpha)


def _params(n_axes=1):
    return pltpu.CompilerParams(dimension_semantics=("arbitrary",) * n_axes,
                                vmem_limit_bytes=VMEM_LIMIT)


def _ffn_kernel(*refs, tm, step, tiles_per_seq, has_state, alpha):
    if has_state:
        (x_ref, st_ref, wi_ref, cw_ref, cb_ref, wo_ref, g_ref, b_ref,
         y_ref, so_ref, gbuf, hbuf) = refs
        carry = None
    else:
        (x_ref, wi_ref, cw_ref, cb_ref, wo_ref, g_ref, b_ref,
         y_ref, so_ref, gbuf, hbuf, carry) = refs
    halo = gbuf.shape[0] - tm
    xb = x_ref[...].astype(BF16)
    if not has_state:
        @pl.when(pl.program_id(0) % tiles_per_seq == 0)
        def _():
            carry[...] = jnp.zeros_like(carry)
    for j in range(D_FF // FF_TILE):
        c0 = j * FF_TILE
        g = _dot(xb, wi_ref[:, c0:c0 + FF_TILE])
        u = _dot(xb, wi_ref[:, D_FF + c0:D_FF + c0 + FF_TILE])
        if has_state:
            gbuf[0:halo, :] = st_ref[:, c0:c0 + FF_TILE]
        else:
            gbuf[0:halo, :] = carry[:, c0:c0 + FF_TILE]
        gbuf[halo:halo + tm, :] = g
        g1 = gbuf[halo - step:halo - step + tm, :]
        g2 = gbuf[halo - 2 * step:halo - 2 * step + tm, :]
        cw = cw_ref[:, c0:c0 + FF_TILE]
        gc = g * cw[2:3] + g1 * cw[1:2] + g2 * cw[0:1] + cb_ref[:, c0:c0 + FF_TILE]
        hbuf[:, c0:c0 + FF_TILE] = (jax.nn.gelu(gc) * u).astype(BF16)
        if has_state:
            so_ref[:, c0:c0 + FF_TILE] = gbuf[tm:tm + halo, :]
        else:
            carry[:, c0:c0 + FF_TILE] = gbuf[tm:tm + halo, :]
            so_ref[0, :, c0:c0 + FF_TILE] = gbuf[halo + tm - 2:halo + tm, :]
    _project_and_norm(x_ref, hbuf, wo_ref, g_ref, b_ref, y_ref, alpha, NORM_SLAB)


def _ffn_call(x, state_tm, wi, cw, cb, wo, g, b, *, seq_len, alpha):
    T = x.shape[0]
    has_state = state_tm is not None
    if has_state:
        step = T // seq_len
        tm, halo, tiles_per_seq = T, (FFN_CONV - 1) * step, 1
        st_shape = (halo, D_FF)
    else:
        tm, step, halo, tiles_per_seq = TM_FFN, 1, SUBLANES, seq_len // TM_FFN
    n_tiles = T // tm
    n_seq = T // seq_len
    row = pl.BlockSpec((tm, D_MODEL), lambda i: (i, 0))
    operands = ([state_tm] if has_state else []) + [wi, cw, cb, wo, g, b]
    specs_args = [_layer_operand(w) for w in operands]
    in_specs = [row] + [s for s, _ in specs_args]
    args = [x] + [a for _, a in specs_args]
    scratch = [pltpu.VMEM((halo + tm, FF_TILE), F32), pltpu.VMEM((tm, D_FF), BF16)]
    if has_state:
        so_shape = jax.ShapeDtypeStruct(st_shape, F32)
        so_spec = pl.BlockSpec(st_shape, lambda i: (0, 0))
    else:
        so_shape = jax.ShapeDtypeStruct((n_seq, FFN_CONV - 1, D_FF), F32)
        so_spec = pl.BlockSpec((1, FFN_CONV - 1, D_FF), lambda i: (i // tiles_per_seq, 0, 0))
        scratch.append(pltpu.VMEM((halo, D_FF), F32))
    return pl.pallas_call(
        functools.partial(_ffn_kernel, tm=tm, step=step, tiles_per_seq=tiles_per_seq,
                          has_state=has_state, alpha=alpha),
        grid=(n_tiles,),
        in_specs=in_specs,
        out_specs=[row, so_spec],
        out_shape=[jax.ShapeDtypeStruct((T, D_MODEL), F32), so_shape],
        scratch_shapes=scratch,
        compiler_params=_params(),
        name="ffn_state" if has_state else "ffn_prompt",
    )(*args)


def _rglru_gates(xc, gt, h, bg_ref, log_sig_lam):
    c0 = h * HD
    bg = bg_ref[:, c0:c0 + HD]
    r = _sigmoid(gt[:, 0:HD] + bg[0:1])
    ig = _sigmoid(gt[:, HD:2 * HD] + bg[1:2])
    a = jnp.exp(RG_C * r * log_sig_lam[:, c0:c0 + HD])
    return a, jnp.sqrt(jnp.maximum(1.0 - a * a, 0.0)) * ig * xc


def _rglru_state_kernel(x_ref, h0_ref, cst_ref, wi_ref, cw_ref, cb_ref, wg_ref, bg_ref, lam_ref,
                        wo_ref, g_ref, b_ref, y_ref, ho_ref, co_ref, xbuf, ybuf, *, tm, step, alpha):
    halo = xbuf.shape[0] - tm
    xb = x_ref[...].astype(BF16)
    xbuf[0:halo, :] = cst_ref[...]
    log_sig_lam = jax.nn.log_sigmoid(lam_ref[...])
    for h in range(HEADS):
        c0 = h * HD
        proj = _dot(xb, wi_ref[:, 2 * c0:2 * c0 + 2 * HD])
        gate = jax.nn.gelu(proj[:, 0:HD])
        xbuf[halo:halo + tm, c0:c0 + HD] = proj[:, HD:2 * HD]
        cw = cw_ref[:, c0:c0 + HD]
        xc = cb_ref[:, c0:c0 + HD] + proj[:, HD:2 * HD] * cw[RG_CONV - 1:RG_CONV]
        for j in range(RG_CONV - 1):
            back = (RG_CONV - 1 - j) * step
            xc = xc + xbuf[halo - back:halo - back + tm, c0:c0 + HD] * cw[j:j + 1]
        a, bv = _rglru_gates(xc, _dot(xc.astype(BF16), wg_ref[h]), h, bg_ref, log_sig_lam)
        hh = h0_ref[:, c0:c0 + HD]
        for t in range(tm // step):
            rows = slice(t * step, (t + 1) * step)
            hh = a[rows] * hh + bv[rows]
            ybuf[rows, c0:c0 + HD] = (hh * gate[rows]).astype(BF16)
        ho_ref[:, c0:c0 + HD] = hh
    co_ref[...] = xbuf[tm:tm + halo, :]
    _project_and_norm(x_ref, ybuf, wo_ref, g_ref, b_ref, y_ref, alpha, NORM_SLAB)


def _rglru_prompt_kernel(x_ref, wi_ref, cw_ref, cb_ref, wg_ref, bg_ref, lam_ref, wo_ref, g_ref, b_ref,
                         y_ref, ho_ref, co_ref,
                         xbuf, gbuf, hbuf, tot, hin, sup, ybuf, ccarry, hcarry,
                         *, tm, tiles_per_seq, alpha):
    halo = xbuf.shape[1] - tm
    groups = tm // SUBLANES
    supers = groups // SUBLANES
    taps = RG_CONV - 1

    @pl.when(pl.program_id(0) % tiles_per_seq == 0)
    def _():
        ccarry[...] = jnp.zeros_like(ccarry)
        hcarry[...] = jnp.zeros_like(hcarry)

    xb = x_ref[...].astype(BF16)
    log_sig_lam = jax.nn.log_sigmoid(lam_ref[...])

    def project(h):
        c0 = h * HD
        proj = _dot(xb, wi_ref[:, 2 * c0:2 * c0 + 2 * HD])
        gbuf[:, c0:c0 + HD] = jax.nn.gelu(proj[:, 0:HD])
        xs = xbuf.at[h]
        xs[0:halo, :] = ccarry[h]
        xs[halo:halo + tm, :] = proj[:, HD:2 * HD]
        ccarry[h] = xs[tm:tm + halo, :]
        co_ref[0, :, c0:c0 + HD] = xs[halo + tm - taps:halo + tm, :]

    def conv(h):
        c0 = h * HD
        xs = xbuf.at[h]
        cw = cw_ref[:, c0:c0 + HD]
        cb = cb_ref[:, c0:c0 + HD]
        lock = [xs[pl.ds(halo - taps + m, groups, stride=SUBLANES), :]
                for m in range(SUBLANES + taps)]
        xc = []
        for k in range(SUBLANES):
            acc = cb + lock[k + taps] * cw[taps:taps + 1]
            for j in range(taps):
                acc = acc + lock[k + j] * cw[j:j + 1]
            xc.append(acc)
        xc = jnp.concatenate(xc, axis=0)
        return xc, _dot(xc.astype(BF16), wg_ref[h])

    def scan(h, xc, gt):
        c0 = h * HD
        a, bv = _rglru_gates(xc, gt, h, bg_ref, log_sig_lam)
        pa, pb = a[0:groups], bv[0:groups]
        a_in, b_in = [pa], [pb]
        for k in range(1, SUBLANES):
            ak = a[k * groups:(k + 1) * groups]
            pb = ak * pb + bv[k * groups:(k + 1) * groups]
            pa = ak * pa
            a_in.append(pa)
            b_in.append(pb)
        tot[h, 0] = pa
        tot[h, 1] = pb
        qa = tot[h, 0, pl.ds(0, supers, stride=SUBLANES), :]
        qb = tot[h, 1, pl.ds(0, supers, stride=SUBLANES), :]
        a_sup, b_sup = [qa], [qb]
        for j in range(1, SUBLANES):
            aj = tot[h, 0, pl.ds(j, supers, stride=SUBLANES), :]
            qb = aj * qb + tot[h, 1, pl.ds(j, supers, stride=SUBLANES), :]
            qa = aj * qa
            a_sup.append(qa)
            b_sup.append(qb)
        hc = hcarry[h]
        for s in range(supers):
            sup[h, pl.ds(s, 1), :] = hc
            hc = qa[s:s + 1] * hc + qb[s:s + 1]
        hcarry[h] = hc
        ho_ref[0, :, c0:c0 + HD] = hc
        h_sup = sup[h]
        hin[h, pl.ds(0, supers, stride=SUBLANES), :] = h_sup
        for j in range(1, SUBLANES):
            hin[h, pl.ds(j, supers, stride=SUBLANES), :] = a_sup[j - 1] * h_sup + b_sup[j - 1]
        h_grp = hin[h]
        for k in range(SUBLANES):
            hbuf[h, pl.ds(k, groups, stride=SUBLANES), :] = a_in[k] * h_grp + b_in[k]

    convs = {}
    for h in range(HEADS + 2):
        if h < HEADS:
            project(h)
        if 1 <= h <= HEADS:
            convs[h - 1] = conv(h - 1)
        if h >= 2:
            scan(h - 2, *convs.pop(h - 2))
    for h in range(HEADS):
        c0 = h * HD
        ybuf[:, c0:c0 + HD] = (hbuf[h] * gbuf[:, c0:c0 + HD]).astype(BF16)
    _project_and_norm(x_ref, ybuf, wo_ref, g_ref, b_ref, y_ref, alpha, NORM_SLAB)


def _rglru_in_proj(w_in):
    lead = w_in.shape[:-1]
    w = w_in.astype(BF16).reshape(*lead, 2, HEADS, HD)
    return jnp.swapaxes(w, -3, -2).reshape(*lead, 2 * HEADS * HD)


def _rglru_call(x, h0, cst_tm, wi, cw, cb, wg, bg, lam, wo, g, b, *, seq_len, alpha):
    T = x.shape[0]
    has_state = h0 is not None
    n_seq = T // seq_len
    operands = ([h0, cst_tm] if has_state else []) + [wi, cw, cb, wg, bg, lam, wo, g, b]
    specs_args = [_layer_operand(w) for w in operands]
    args = [x] + [a for _, a in specs_args]
    if has_state:
        tm, halo = T, (RG_CONV - 1) * (T // seq_len)
        h0_shape, cst_shape = _layer_shape(h0), (halo, D_MODEL)
        body = functools.partial(_rglru_state_kernel, tm=tm, step=T // seq_len, alpha=alpha)
        row = pl.BlockSpec((tm, D_MODEL), lambda i: (i, 0))
        out_shape = [jax.ShapeDtypeStruct((T, D_MODEL), F32),
                     jax.ShapeDtypeStruct(h0_shape, F32),
                     jax.ShapeDtypeStruct(cst_shape, F32)]
        out_specs = [row, pl.BlockSpec(h0_shape, lambda i: (0, 0)),
                     pl.BlockSpec(cst_shape, lambda i: (0, 0))]
        scratch = [pltpu.VMEM((halo + tm, D_MODEL), F32), pltpu.VMEM((tm, D_MODEL), BF16)]
    else:
        tm, halo, tiles_per_seq = TM_PROMPT, SUBLANES, seq_len // TM_PROMPT
        groups = tm // SUBLANES
        body = functools.partial(_rglru_prompt_kernel, tm=tm, tiles_per_seq=tiles_per_seq, alpha=alpha)
        row = pl.BlockSpec((tm, D_MODEL), lambda i: (i, 0))
        out_shape = [jax.ShapeDtypeStruct((T, D_MODEL), F32),
                     jax.ShapeDtypeStruct((n_seq, 1, D_MODEL), F32),
                     jax.ShapeDtypeStruct((n_seq, RG_CONV - 1, D_MODEL), F32)]
        out_specs = [row,
                     pl.BlockSpec((1, 1, D_MODEL), lambda i: (i // tiles_per_seq, 0, 0)),
                     pl.BlockSpec((1, RG_CONV - 1, D_MODEL), lambda i: (i // tiles_per_seq, 0, 0))]
        scratch = [pltpu.VMEM((HEADS, halo + tm, HD), F32),
                   pltpu.VMEM((tm, D_MODEL), F32),
                   pltpu.VMEM((HEADS, tm, HD), F32),
                   pltpu.VMEM((HEADS, 2, groups, HD), F32),
                   pltpu.VMEM((HEADS, groups, HD), F32),
                   pltpu.VMEM((HEADS, groups // SUBLANES, HD), F32),
                   pltpu.VMEM((tm, D_MODEL), BF16),
                   pltpu.VMEM((HEADS, halo, HD), F32),
                   pltpu.VMEM((HEADS, 1, HD), F32)]
    in_specs = [row] + [s for s, _ in specs_args]
    n_tiles = T // tm
    return pl.pallas_call(
        body,
        grid=(n_tiles,),
        in_specs=in_specs,
        out_specs=out_specs,
        out_shape=out_shape,
        scratch_shapes=scratch,
        compiler_params=_params(),
        name="rglru_state" if has_state else "rglru_prompt",
    )(*args)


def _split3(x):
    hi = x.astype(BF16)
    r = x - hi.astype(F32)
    mid = r.astype(BF16)
    lo = (r - mid.astype(F32)).astype(BF16)
    return hi, mid, lo


def _lower_bound(lower_ref, h, layer):
    rows = [lower_ref[n, h] for n in range(lower_ref.shape[0])]
    m = functools.reduce(jnp.maximum, rows)
    es = [jnp.exp(r - m) for r in rows]
    tot = functools.reduce(lambda p, q: p + q, es)
    sm = [e / tot for e in es]
    cs = functools.reduce(lambda p, q: p + q, sm[:layer + 1])
    return jnp.maximum(cs - sm[0], 0.0)


def _chunk_cumsum(cb_ref, tb_ref, eb_ref, tm, chunk):
    groups = tm // SUBLANES
    p = cb_ref[pl.ds(0, groups, stride=SUBLANES), :]
    for k in range(1, SUBLANES):
        p = p + cb_ref[pl.ds(k, groups, stride=SUBLANES), :]
        cb_ref[pl.ds(k, groups, stride=SUBLANES), :] = p
    if chunk > SUBLANES:
        gc = chunk // SUBLANES
        nch = tm // chunk
        tb_ref[...] = p
        e = jnp.zeros((nch, HD), F32)
        eb_ref[pl.ds(0, nch, stride=gc), :] = e
        for j in range(1, gc):
            e = e + tb_ref[pl.ds(j - 1, nch, stride=gc), :]
            eb_ref[pl.ds(j, nch, stride=gc), :] = e
        off = eb_ref[...]
        for k in range(SUBLANES):
            cb_ref[pl.ds(k, groups, stride=SUBLANES), :] = (
                cb_ref[pl.ds(k, groups, stride=SUBLANES), :] + off)
    return p


def _hgrn_front(part, lower_ref, h, layer, tm, chunk,
                cb_ref, tb_ref, eb_ref, kk_ref, v_ref, sg_ref, qs_ref, qe_ref):
    lb = _lower_bound(lower_ref, h, layer)
    fg = lb + (1.0 - lb) * _sigmoid(part(1))
    cb_ref[...] = jnp.log(jnp.maximum(fg, F_FLOOR))
    kk_ref[...] = 1.0 - fg
    q = part(0)
    qs = q * _sigmoid(q) * (HD ** -0.5)
    v_ref[...] = part(2)
    gg = part(3)
    sg_ref[...] = gg * _sigmoid(gg)
    totals = _chunk_cumsum(cb_ref, tb_ref, eb_ref, tm, chunk)
    cum = cb_ref[...]
    qs_ref[...] = qs
    qe_ref[...] = qs * jnp.exp(cum)
    safe = jnp.max(-cum) <= SAFE_DECAY
    return safe, totals


def _same_chunk_mask(chunk):
    ri = lax.broadcasted_iota(jnp.int32, (BLK, BLK), 0)
    ci = lax.broadcasted_iota(jnp.int32, (BLK, BLK), 1)
    lc = chunk.bit_length() - 1
    return ((ri >> lc) == (ci >> lc)) & (ci <= ri)


def _inverse_decayed_keys(kk, cum):
    return kk * jnp.exp(jnp.minimum(-cum, SAFE_DECAY))


def _robust_same_chunk_scores(rows, chunk, cb_ref, kk_ref, qs_ref):
    ri = lax.broadcasted_iota(jnp.int32, (BLK, BLK), 0)
    ci = lax.broadcasted_iota(jnp.int32, (BLK, BLK), 1)
    cum = cb_ref[rows, :]
    qs = qs_ref[rows, :]
    kk = kk_ref[rows, :]
    hi, mid, lo = _split3(cum)
    acc = jnp.where(ri == ci, _dot_nt(qs.astype(BF16), kk.astype(BF16)), 0.0)
    for lvl in range(chunk.bit_length() - 1):
        m = 1 << lvl
        pivot = ((ri >> (lvl + 1)) << (lvl + 1)) + (m - 1)
        sel = jnp.where(ci == pivot, 1.0, 0.0).astype(BF16)
        ref_cum = _dot(sel, hi) + _dot(sel, mid) + _dot(sel, lo)
        e = jnp.exp(-jnp.abs(cum - ref_cum))
        keep = (((ri >> (lvl + 1)) == (ci >> (lvl + 1)))
                & ((ri & (2 * m - 1)) >= m) & ((ci & (2 * m - 1)) < m))
        s = _dot_nt((qs * e).astype(BF16), (kk * e).astype(BF16))
        acc = acc + jnp.where(keep, s, 0.0)
    return acc


def _rms_gate(o, ng, sg):
    return o * lax.rsqrt(jnp.mean(o * o, axis=-1, keepdims=True) + RMS_EPS) * ng * sg


def _hgrn_prompt_kernel(x_ref, lower_ref, wi_ref, ng_ref, wo_ref, g_ref, b_ref, y_ref, so_ref,
                        xb_ref, p_ref, cb_ref, tb_ref, eb_ref, kk_ref, v_ref, sg_ref, qs_ref,
                        qe_ref, kd_ref, a_ref, oh_ref, ds_ref, ec_ref, st_ref, ob_ref, s_ref,
                        *, tm, layer, tiles_per_seq, alpha):
    chunk = PROMPT_CHUNK
    assert BLK == 2 * chunk
    n_blocks = tm // BLK
    i = pl.program_id(0)

    @pl.when(i % tiles_per_seq == 0)
    def _():
        s_ref[...] = jnp.zeros_like(s_ref)

    xb_ref[...] = x_ref[...].astype(BF16)
    second = lax.broadcasted_iota(jnp.int32, (BLK, HD), 0) >= chunk
    ri = lax.broadcasted_iota(jnp.int32, (BLK, BLK), 0)
    ci = lax.broadcasted_iota(jnp.int32, (BLK, BLK), 1)
    cross = (ri >= chunk) & (ci < chunk)
    same = _same_chunk_mask(chunk)

    def project(h, slot):
        p_ref[slot] = _dot(xb_ref[...], wi_ref[h])

    def views(slot):
        return tuple(r.at[slot] for r in (cb_ref, tb_ref, eb_ref, kk_ref, v_ref, sg_ref, qs_ref,
                                          qe_ref, kd_ref, a_ref, oh_ref, ds_ref, ec_ref))

    def scores(h, slot):
        cb, tb, eb, kk_s, v_s, sg_s, qs_s, qe_s, kd_s, a_s, _, ds_s, ec_s = views(slot)
        safe, _ = _hgrn_front(lambda k: p_ref[slot, :, k * HD:(k + 1) * HD], lower_ref, h, layer,
                              tm, chunk, cb, tb, eb, kk_s, v_s, sg_s, qs_s, qe_s)
        for nb in range(n_blocks):
            rows = pl.ds(nb * BLK, BLK)
            cum = cb[rows, :]
            kk = kk_s[rows, :]
            last0 = cb[pl.ds(nb * BLK + chunk - 1, 1), :]
            last1 = cb[pl.ds(nb * BLK + BLK - 1, 1), :]
            kd = kk * jnp.exp(jnp.where(second, last1, last0) - cum)
            kd_s[rows, :] = kd
            keys = jnp.concatenate([_inverse_decayed_keys(kk, cum), kd], axis=0).astype(BF16)
            s2 = _dot_nt(qe_s[rows, :].astype(BF16), keys)
            a = jnp.where(same, s2[:, 0:BLK], 0.0) + jnp.where(cross, s2[:, BLK:2 * BLK], 0.0)
            a_s[nb] = a.astype(BF16)
            k_blk = kd * jnp.where(second, 1.0, jnp.exp(last1))
            ds_s[nb] = _dot(k_blk.T.astype(BF16), v_s[rows, :].astype(BF16))
            ec_s[nb] = jnp.broadcast_to(jnp.exp(last0 + last1), (HD, HD)).T
        return safe

    def robust_scores(slot):
        cb, _, _, kk_s, _, _, qs_s, qe_s, kd_s, a_s, _, _, _ = views(slot)

        def fix(nb, _):
            rows = pl.ds(pl.multiple_of(nb * BLK, BLK), BLK)
            s1 = _dot_nt(qe_s[rows, :].astype(BF16), kd_s[rows, :].astype(BF16))
            a = _robust_same_chunk_scores(rows, chunk, cb, kk_s, qs_s)
            a_s[nb] = (a + jnp.where(cross, s1, 0.0)).astype(BF16)
            return 0
        lax.fori_loop(0, n_blocks, fix, 0)

    def advance_state(h, slot):
        ds_s, ec_s = views(slot)[11:13]
        state = s_ref[h]
        for nb in range(n_blocks):
            st_ref[slot, nb] = state
            state = ec_s[nb] * state + ds_s[nb]
        s_ref[h] = state

    def outputs(h, slot):
        cb, _, _, _, v_s, sg_s, _, qe_s, _, a_s, oh_s, _, _ = views(slot)
        for nb in range(n_blocks):
            rows = pl.ds(nb * BLK, BLK)
            last0 = cb[pl.ds(nb * BLK + chunk - 1, 1), :]
            q_blk = qe_s[rows, :] * jnp.where(second, jnp.exp(last0), 1.0)
            lhs = jnp.concatenate([a_s[nb], q_blk.astype(BF16)], axis=1)
            rhs = jnp.concatenate([v_s[rows, :].astype(BF16), st_ref[slot, nb].astype(BF16)], axis=0)
            oh_s[rows, :] = _dot(lhs, rhs)
        ob_ref[h] = _rms_gate(oh_s[...], ng_ref[...], sg_s[...]).astype(BF16)

    project(0, 0)

    def pair(h0, last):
        project(h0 + 1, 1)
        safe0 = scores(h0, 0)
        advance_state(h0, 0)
        outputs(h0, 0)
        if not last:
            project(h0 + 2, 0)
        safe1 = scores(h0 + 1, 1)
        advance_state(h0 + 1, 1)
        outputs(h0 + 1, 1)

        @pl.when(jnp.logical_not(jnp.logical_and(safe0, safe1)))
        def _():
            for slot in range(2):
                robust_scores(slot)
                outputs(h0 + slot, slot)
        return 0

    lax.fori_loop(0, HEADS // 2 - 1, lambda hh, _: pair(2 * hh, False), 0)
    pair(HEADS - 2, True)
    for h in range(HEADS):
        xb_ref[:, h * HD:(h + 1) * HD] = ob_ref[h]
    _project_and_norm(x_ref, xb_ref, wo_ref, g_ref, b_ref, y_ref, alpha, NORM_SLAB)

    @pl.when(i % tiles_per_seq == tiles_per_seq - 1)
    def _():
        so_ref[0] = s_ref[...]


def _hgrn_prep(hg_lower, wi, ng, wo):
    n_layers = hg_lower.shape[0]
    lower = hg_lower.reshape(n_layers, HEADS, 1, HD)
    wih = (wi.astype(BF16).reshape(n_layers, D_MODEL, 4, HEADS, HD).transpose(0, 3, 1, 2, 4)
           .reshape(n_layers, HEADS, D_MODEL, 4 * HD))
    return lower, wih, ng.reshape(n_layers, 1, HD), wo.astype(BF16)


def _hgrn_prompt_call(x, prep, layer, g, b, *, seq_len, alpha):
    lower, wih, ng, wo = prep
    T = x.shape[0]
    tm = TM_PROMPT
    tiles_per_seq = seq_len // tm
    n_seq = T // seq_len
    row = pl.BlockSpec((tm, D_MODEL), lambda i: (i, 0))
    specs_args = [_layer_operand(w) for w in (lower, (wih, layer), (ng, layer), (wo, layer), g, b)]
    head_buf = pltpu.VMEM((2, tm, HD), F32)
    scratch = [pltpu.VMEM((tm, D_MODEL), BF16),
               pltpu.VMEM((2, tm, 4 * HD), F32),
               head_buf,
               pltpu.VMEM((2, tm // SUBLANES, HD), F32),
               pltpu.VMEM((2, tm // SUBLANES, HD), F32),
               head_buf, head_buf, head_buf, head_buf, head_buf,
               head_buf,
               pltpu.VMEM((2, tm // BLK, BLK, BLK), BF16),
               head_buf,
               pltpu.VMEM((2, tm // BLK, HD, HD), F32),
               pltpu.VMEM((2, tm // BLK, HD, HD), F32),
               pltpu.VMEM((2, tm // BLK, HD, HD), F32),
               pltpu.VMEM((HEADS, tm, HD), BF16),
               pltpu.VMEM((HEADS, HD, HD), F32)]
    y, so = pl.pallas_call(
        functools.partial(_hgrn_prompt_kernel, tm=tm, layer=layer,
                          tiles_per_seq=tiles_per_seq, alpha=alpha),
        grid=(T // tm,),
        in_specs=[row] + [s for s, _ in specs_args],
        out_specs=[row, pl.BlockSpec((1, HEADS, HD, HD), lambda i: (i // tiles_per_seq, 0, 0, 0))],
        out_shape=[jax.ShapeDtypeStruct((T, D_MODEL), F32),
                   jax.ShapeDtypeStruct((n_seq, HEADS, HD, HD), F32)],
        scratch_shapes=scratch,
        compiler_params=_params(),
        name="hgrn_prompt",
    )(x, *[a for _, a in specs_args])
    return y, so


def _hgrn_sample_front_kernel(x_ref, lower_ref, wi_ref,
                              oi_ref, qe_ref, v_ref, sg_ref, kdt_ref, et_ref,
                              xb_ref, p_ref, cb_ref, kk_ref, qs_ref, a_ref, last_ref,
                              *, tm, chunk, layer):
    assert chunk == SUBLANES
    xb_ref[...] = x_ref[...].astype(BF16)
    groups = tm // SUBLANES
    n_blocks = tm // BLK
    same = _same_chunk_mask(chunk)

    def head(h, _):
        proj = _dot(xb_ref[...], wi_ref[h])
        for t in range(chunk):
            for k in range(4):
                p_ref[k, pl.ds(t, groups, stride=chunk), :] = (
                    proj[t * groups:(t + 1) * groups, k * HD:(k + 1) * HD])
        safe, totals = _hgrn_front(lambda k: p_ref[k], lower_ref, h, layer, tm, chunk,
                                   cb_ref, None, None, kk_ref, v_ref.at[h], sg_ref.at[h],
                                   qs_ref, qe_ref.at[h])
        for k in range(SUBLANES):
            last_ref[pl.ds(k, groups, stride=SUBLANES), :] = totals
        for nb in range(n_blocks):
            r0 = nb * BLK
            rows = pl.ds(r0, BLK)
            cum = cb_ref[rows, :]
            kk = kk_ref[rows, :]
            s = _dot_nt(qe_ref[h, rows, :].astype(BF16), _inverse_decayed_keys(kk, cum).astype(BF16))
            a_ref[nb] = jnp.where(same, s, 0.0).astype(BF16)
            last = last_ref[rows, :]
            kdt_ref[h, :, r0:r0 + BLK] = (kk * jnp.exp(last - cum)).T
            et_ref[h, :, r0:r0 + BLK] = jnp.exp(last).T

        @pl.when(jnp.logical_not(safe))
        def _():
            def fix(nb, _):
                rows = pl.ds(pl.multiple_of(nb * BLK, BLK), BLK)
                a_ref[nb] = _robust_same_chunk_scores(rows, chunk, cb_ref, kk_ref, qs_ref).astype(BF16)
                return 0
            lax.fori_loop(0, n_blocks, fix, 0)

        for nb in range(n_blocks):
            rows = pl.ds(nb * BLK, BLK)
            oi_ref[h, rows, :] = _dot(a_ref[nb], v_ref[h, rows, :].astype(BF16))
        return 0

    lax.fori_loop(0, HEADS, head, 0)


def _hgrn_sample_state_kernel(qe_ref, oi_ref, v_ref, sg_ref, kdt_ref, et_ref, s0_ref, ng_ref,
                              *rest, chunk):
    op_ref, so_ref = rest[-2:]
    row_seq = lax.broadcasted_iota(jnp.int32, (BLK, HD), 0) >> (chunk.bit_length() - 1)

    def head(h, _):
        kdt = kdt_ref[h].astype(BF16)
        et = et_ref[h]
        v = v_ref[h]
        for sq in range(BLK // chunk):
            rows = slice(sq * chunk, (sq + 1) * chunk)
            s0 = s0_ref[sq, h]
            o = oi_ref[h, rows, :] + _dot(qe_ref[h, rows, :].astype(BF16), s0.astype(BF16))
            op_ref[h, rows, :] = _rms_gate(o, ng_ref[...], sg_ref[h, rows, :])
            vm = jnp.where(row_seq == sq, v, 0.0).astype(BF16)
            e_col = jnp.broadcast_to(et[:, sq * chunk:sq * chunk + 1], (HD, HD))
            so_ref[sq, h] = e_col * s0 + _dot(kdt, vm)
        return 0

    lax.fori_loop(0, HEADS, head, 0)


def _hgrn_sample_out_kernel(x_ref, op_ref, wo_ref, g_ref, b_ref, y_ref, ob_ref, *, chunk, alpha):
    batch = x_ref.shape[0] // chunk
    for h in range(HEADS):
        for t in range(chunk):
            ob_ref[t * batch:(t + 1) * batch, h * HD:(h + 1) * HD] = (
                op_ref[h, pl.ds(t, batch, stride=chunk), :].astype(BF16))
    _project_and_norm(x_ref, ob_ref, wo_ref, g_ref, b_ref, y_ref, alpha, NORM_SLAB)


def _hgrn_sample_call(x, s0, so_prev, prep, layer, g, b, *, seq_len, alpha):
    lower, wih, ng, wo = prep
    T = x.shape[0]
    chunk = seq_len
    heads_rows = jax.ShapeDtypeStruct((HEADS, T, HD), F32)
    heads_cols = jax.ShapeDtypeStruct((HEADS, HD, T), F32)
    full = lambda s: pl.BlockSpec(s, lambda i: (0,) * len(s))
    head_buf = pltpu.VMEM((T, HD), F32)
    specs_args = [_layer_operand(w) for w in (lower, (wih, layer))]
    oi, qe, v, sg, kdt, et = pl.pallas_call(
        functools.partial(_hgrn_sample_front_kernel, tm=T, chunk=chunk, layer=layer),
        grid=(1,),
        in_specs=[full(x.shape)] + [s for s, _ in specs_args],
        out_specs=[full(heads_rows.shape)] * 4 + [full(heads_cols.shape)] * 2,
        out_shape=[heads_rows] * 4 + [heads_cols] * 2,
        scratch_shapes=[pltpu.VMEM((T, D_MODEL), BF16), pltpu.VMEM((4, T, HD), F32),
                        head_buf, head_buf, head_buf,
                        pltpu.VMEM((T // BLK, BLK, BLK), BF16), head_buf],
        compiler_params=_params(),
        name="hgrn_sample_front",
    )(x, *[a for _, a in specs_args])
    seqs = BLK // chunk
    by_rows = pl.BlockSpec((HEADS, BLK, HD), lambda i: (0, i, 0))
    by_cols = pl.BlockSpec((HEADS, HD, BLK), lambda i: (0, 0, i))
    state = pl.BlockSpec((None, seqs, HEADS, HD, HD), lambda i: (layer, i, 0, 0, 0))
    ng_spec, ng_arr = _layer_operand((ng, layer))
    in_specs = [by_rows, by_rows, by_rows, by_rows, by_cols, by_cols, state, ng_spec]
    args = [qe, oi, v, sg, kdt, et, s0, ng_arr]
    in_specs.append(pl.BlockSpec(memory_space=pl.ANY))
    args.append(so_prev)
    aliases = {len(args) - 1: 1}
    op, so = pl.pallas_call(
        functools.partial(_hgrn_sample_state_kernel, chunk=chunk),
        grid=(T // BLK,),
        in_specs=in_specs,
        out_specs=[by_rows, state],
        out_shape=[heads_rows, jax.ShapeDtypeStruct(s0.shape, F32)],
        input_output_aliases=aliases,
        compiler_params=_params(),
        name="hgrn_sample_state",
    )(*args)
    specs_args = [_layer_operand(w) for w in ((wo, layer), g, b)]
    y = pl.pallas_call(
        functools.partial(_hgrn_sample_out_kernel, chunk=chunk, alpha=alpha),
        grid=(1,),
        in_specs=[full(x.shape), full(op.shape)] + [s for s, _ in specs_args],
        out_specs=full(x.shape),
        out_shape=jax.ShapeDtypeStruct(x.shape, F32),
        scratch_shapes=[pltpu.VMEM((T, D_MODEL), BF16)],
        compiler_params=_params(),
        name="hgrn_sample_out",
    )(x, op, *[a for _, a in specs_args])
    return y, so


def _to_time_major(a):
    return jnp.transpose(a, (1, 0, 2)).reshape(-1, a.shape[-1])


def _from_time_major(a, batch):
    return jnp.transpose(a.reshape(-1, batch, a.shape[-1]), (1, 0, 2))


def kernel(x_prompt, x_sample, state_rglru_h, state_rglru_conv, state_hgrn_s, state_ffn_conv,
           ln_g, ln_b, rg_w_in, rg_conv_w, rg_conv_b, rg_gate_w, rg_gate_b, rg_lambda, rg_w_out,
           hg_lower, hg_w_in, hg_norm_g, hg_w_out, ffn_w_in, ffn_conv_w, ffn_conv_b, ffn_w_out):
    depth = ln_g.shape[0]
    alpha = (2.0 * depth) ** 0.25
    pb, pl_len, _ = x_prompt.shape
    sb, sl_len, _ = x_sample.shape

    n_rg, n_ffn = rg_w_in.shape[0], ffn_w_in.shape[0]
    lng = ln_g.reshape(depth * 2, 1, D_MODEL)
    lnb = ln_b.reshape(depth * 2, 1, D_MODEL)
    rg_wi = _rglru_in_proj(rg_w_in)
    rg_wg = jnp.concatenate([rg_gate_w[:, 0], rg_gate_w[:, 1]], axis=-1).astype(BF16)
    rg_wo = rg_w_out.astype(BF16)
    rg_cb = rg_conv_b.reshape(n_rg, 1, D_MODEL)
    rg_lam = rg_lambda.reshape(n_rg, 1, D_MODEL)
    hg = _hgrn_prep(hg_lower, hg_w_in, hg_norm_g, hg_w_out)
    ffn_wi = ffn_w_in.astype(BF16)
    ffn_wo = ffn_w_out.astype(BF16)
    ffn_cb = ffn_conv_b.reshape(n_ffn, 1, D_FF)
    rg_cst = jnp.transpose(state_rglru_conv, (0, 2, 1, 3)).reshape(n_rg, -1, D_MODEL)
    ffn_cst = jnp.transpose(state_ffn_conv, (0, 2, 1, 3)).reshape(n_ffn, -1, D_FF)

    xp = x_prompt.reshape(pb * pl_len, D_MODEL)
    xs = _to_time_major(x_sample)
    p_h, p_rc, p_s, p_fc = [], [], [], []
    s_h, s_rc, s_fc = [], [], []
    s_s = jnp.zeros(state_hgrn_s.shape, F32)
    for i in range(depth):
        j = i // 2
        g0, b0 = (lng, 2 * i), (lnb, 2 * i)
        if i % 2 == 0:
            w = ((rg_wi, j), (rg_conv_w, j), (rg_cb, j), (rg_wg, j), (rg_gate_b, j), (rg_lam, j),
                 (rg_wo, j), g0, b0)
            xp, ho, co = _rglru_call(xp, None, None, *w, seq_len=pl_len, alpha=alpha)
            p_h.append(ho[:, 0])
            p_rc.append(co)
            xs, ho, co = _rglru_call(xs, (state_rglru_h, j), (rg_cst, j), *w,
                                     seq_len=sl_len, alpha=alpha)
            s_h.append(ho)
            s_rc.append(co)
        else:
            xp, so = _hgrn_prompt_call(xp, hg, j, g0, b0, seq_len=pl_len, alpha=alpha)
            p_s.append(so)
            xs, s_s = _hgrn_sample_call(xs, state_hgrn_s, s_s, hg, j, g0, b0,
                                        seq_len=sl_len, alpha=alpha)
        w = ((ffn_wi, i), (ffn_conv_w, i), (ffn_cb, i), (ffn_wo, i), (lng, 2 * i + 1), (lnb, 2 * i + 1))
        xp, fo = _ffn_call(xp, None, *w, seq_len=pl_len, alpha=alpha)
        p_fc.append(fo)
        xs, fo = _ffn_call(xs, (ffn_cst, i), *w, seq_len=sl_len, alpha=alpha)
        s_fc.append(fo)

    def stacked_from_time_major(parts):
        a = jnp.stack(parts)
        a = a.reshape(a.shape[0], -1, sb, a.shape[-1])
        return jnp.transpose(a, (0, 2, 1, 3))

    return (xp.reshape(x_prompt.shape), _from_time_major(xs, sb),
            jnp.stack(p_h), jnp.stack(p_rc), jnp.stack(p_s), jnp.stack(p_fc),
            jnp.stack(s_h), stacked_from_time_major(s_rc), s_s, stacked_from_time_major(s_fc))
```

```python
import functools

import jax
import jax.numpy as jnp
from jax import lax
from jax.experimental import pallas as pl
from jax.experimental.pallas import tpu as pltpu

F32 = jnp.float32
BF16 = jnp.bfloat16

D_MODEL = 1024
D_FF = 2816
HEADS = 8
HD = 128
RG_CONV = 4
FFN_CONV = 3
RG_C = 8.0
LN_EPS = 1e-5
RMS_EPS = 1e-6
F_FLOOR = 1e-30
SUBLANES = 8
LANES = 128
BLK = 128
PROMPT_CHUNK = 64
SAFE_DECAY = 80.0
FF_TILE = 256
TM_PROMPT = 1024
TM_FFN = 1024
NORM_SLAB = 256
VMEM_LIMIT = 56 * 1024 * 1024


def _dot(a, b):
    return jnp.dot(a, b, preferred_element_type=F32)


def _dot_nt(a, b):
    return lax.dot_general(a, b, (((1,), (1,)), ((), ())), preferred_element_type=F32)


def _sigmoid(x):
    return 0.5 * jnp.tanh(0.5 * x) + 0.5


def _gelu(x):
    c = (2.0 / jnp.pi) ** 0.5
    return x * (0.5 * jnp.tanh(x * (c + (c * 0.044715) * (x * x))) + 0.5)


def _sqrt_nonneg(y):
    return jnp.where(y > 0.0, y * lax.rsqrt(y), 0.0)


def _resident(shape):
    nd = len(shape)
    return pl.BlockSpec(shape, lambda *_: (0,) * nd, pipeline_mode=pl.Buffered(1))


def _layer_shape(w):
    return w[0].shape[1:] if isinstance(w, tuple) else w.shape


def _layer_operand(w):
    if not isinstance(w, tuple):
        return _resident(w.shape), w
    arr, layer = w
    tail = (0,) * (arr.ndim - 1)
    spec = pl.BlockSpec((None,) + arr.shape[1:], lambda *_: (layer,) + tail,
                        pipeline_mode=pl.Buffered(1))
    return spec, arr


def _deepnorm(x, f, g, b, alpha):
    z = alpha * x + f
    mu = jnp.mean(z, axis=-1, keepdims=True)
    zc = z - mu
    var = jnp.mean(zc * zc, axis=-1, keepdims=True)
    return zc * lax.rsqrt(var + LN_EPS) * g + b


def _project_and_norm(x_ref, lhs_ref, wo_ref, g_ref, b_ref, y_ref, alpha, slab):
    rows = x_ref.shape[0]
    slab = min(slab, rows)
    for r0 in range(0, rows, slab):
        sl = pl.ds(r0, slab)
        f = _dot(lhs_ref[sl, :], wo_ref[...])
        y_ref[sl, :] = _deepnorm(x_ref[sl, :], f, g_ref[...], b_ref[...], alpha)


def _params(n_axes=1):
    return pltpu.CompilerParams(dimension_semantics=("arbitrary",) * n_axes,
                                vmem_limit_bytes=VMEM_LIMIT)


def _ffn_kernel(*refs, tm, step, tiles_per_seq, has_state, alpha):
    if has_state:
        (x_ref, st_ref, wi_ref, cw_ref, cb_ref, wo_ref, g_ref, b_ref,
         y_ref, so_ref, gbuf, hbuf) = refs
        carry = None
    else:
        (x_ref, wi_ref, cw_ref, cb_ref, wo_ref, g_ref, b_ref,
         y_ref, so_ref, gbuf, hbuf, carry) = refs
    halo = gbuf.shape[0] - tm
    xb = x_ref[...].astype(BF16)
    if not has_state:
        @pl.when(pl.program_id(0) % tiles_per_seq == 0)
        def _():
            carry[...] = jnp.zeros_like(carry)
    for j in range(D_FF // FF_TILE):
        c0 = j * FF_TILE
        g = _dot(xb, wi_ref[:, c0:c0 + FF_TILE])
        u = _dot(xb, wi_ref[:, D_FF + c0:D_FF + c0 + FF_TILE])
        if has_state:
            gbuf[0:halo, :] = st_ref[:, c0:c0 + FF_TILE]
        else:
            gbuf[0:halo, :] = carry[:, c0:c0 + FF_TILE]
        gbuf[halo:halo + tm, :] = g
        g1 = gbuf[halo - step:halo - step + tm, :]
        g2 = gbuf[halo - 2 * step:halo - 2 * step + tm, :]
        cw = cw_ref[:, c0:c0 + FF_TILE]
        gc = g * cw[2:3] + g1 * cw[1:2] + g2 * cw[0:1] + cb_ref[:, c0:c0 + FF_TILE]
        hbuf[:, c0:c0 + FF_TILE] = (_gelu(gc) * u).astype(BF16)
        if has_state:
            so_ref[:, c0:c0 + FF_TILE] = gbuf[tm:tm + halo, :]
        else:
            carry[:, c0:c0 + FF_TILE] = gbuf[tm:tm + halo, :]
            so_ref[0, :, c0:c0 + FF_TILE] = gbuf[halo + tm - 2:halo + tm, :]
    _project_and_norm(x_ref, hbuf, wo_ref, g_ref, b_ref, y_ref, alpha, NORM_SLAB)


def _ffn_call(x, state_tm, wi, cw, cb, wo, g, b, *, seq_len, alpha):
    T = x.shape[0]
    has_state = state_tm is not None
    if has_state:
        step = T // seq_len
        tm, halo, tiles_per_seq = T, (FFN_CONV - 1) * step, 1
        st_shape = (halo, D_FF)
    else:
        tm, step, halo, tiles_per_seq = TM_FFN, 1, SUBLANES, seq_len // TM_FFN
    n_tiles = T // tm
    n_seq = T // seq_len
    row = pl.BlockSpec((tm, D_MODEL), lambda i: (i, 0))
    operands = ([state_tm] if has_state else []) + [wi, cw, cb, wo, g, b]
    specs_args = [_layer_operand(w) for w in operands]
    in_specs = [row] + [s for s, _ in specs_args]
    args = [x] + [a for _, a in specs_args]
    scratch = [pltpu.VMEM((halo + tm, FF_TILE), F32), pltpu.VMEM((tm, D_FF), BF16)]
    if has_state:
        so_shape = jax.ShapeDtypeStruct(st_shape, F32)
        so_spec = pl.BlockSpec(st_shape, lambda i: (0, 0))
    else:
        so_shape = jax.ShapeDtypeStruct((n_seq, FFN_CONV - 1, D_FF), F32)
        so_spec = pl.BlockSpec((1, FFN_CONV - 1, D_FF), lambda i: (i // tiles_per_seq, 0, 0))
        scratch.append(pltpu.VMEM((halo, D_FF), F32))
    return pl.pallas_call(
        functools.partial(_ffn_kernel, tm=tm, step=step, tiles_per_seq=tiles_per_seq,
                          has_state=has_state, alpha=alpha),
        grid=(n_tiles,),
        in_specs=in_specs,
        out_specs=[row, so_spec],
        out_shape=[jax.ShapeDtypeStruct((T, D_MODEL), F32), so_shape],
        scratch_shapes=scratch,
        compiler_params=_params(),
        name="ffn_state" if has_state else "ffn_prompt",
    )(*args)


def _rglru_gates(xc, gt, h, bg_ref, log_sig_lam):
    c0 = h * HD
    bg = bg_ref[:, c0:c0 + HD]
    r = _sigmoid(gt[:, 0:HD] + bg[0:1])
    ig = _sigmoid(gt[:, HD:2 * HD] + bg[1:2])
    a = jnp.exp(RG_C * r * log_sig_lam[:, c0:c0 + HD])
    return a, _sqrt_nonneg(jnp.maximum(1.0 - a * a, 0.0)) * ig * xc


def _rglru_state_kernel(x_ref, h0_ref, cst_ref, wi_ref, cw_ref, cb_ref, wg_ref, bg_ref, lam_ref,
                        wo_ref, g_ref, b_ref, y_ref, ho_ref, co_ref, xbuf, ybuf, *, tm, step, alpha):
    halo = xbuf.shape[0] - tm
    xb = x_ref[...].astype(BF16)
    xbuf[0:halo, :] = cst_ref[...]
    log_sig_lam = jax.nn.log_sigmoid(lam_ref[...])
    for h in range(HEADS):
        c0 = h * HD
        proj = _dot(xb, wi_ref[:, 2 * c0:2 * c0 + 2 * HD])
        gate = _gelu(proj[:, 0:HD])
        xbuf[halo:halo + tm, c0:c0 + HD] = proj[:, HD:2 * HD]
        cw = cw_ref[:, c0:c0 + HD]
        xc = cb_ref[:, c0:c0 + HD] + proj[:, HD:2 * HD] * cw[RG_CONV - 1:RG_CONV]
        for j in range(RG_CONV - 1):
            back = (RG_CONV - 1 - j) * step
            xc = xc + xbuf[halo - back:halo - back + tm, c0:c0 + HD] * cw[j:j + 1]
        a, bv = _rglru_gates(xc, _dot(xc.astype(BF16), wg_ref[h]), h, bg_ref, log_sig_lam)
        hh = h0_ref[:, c0:c0 + HD]
        for t in range(tm // step):
            rows = slice(t * step, (t + 1) * step)
            hh = a[rows] * hh + bv[rows]
            ybuf[rows, c0:c0 + HD] = (hh * gate[rows]).astype(BF16)
        ho_ref[:, c0:c0 + HD] = hh
    co_ref[...] = xbuf[tm:tm + halo, :]
    _project_and_norm(x_ref, ybuf, wo_ref, g_ref, b_ref, y_ref, alpha, NORM_SLAB)


def _rglru_prompt_kernel(x_ref, wi_ref, cw_ref, cb_ref, wg_ref, bg_ref, lam_ref, wo_ref, g_ref, b_ref,
                         y_ref, ho_ref, co_ref,
                         xbuf, gbuf, hbuf, tot, hin, sup, ybuf, ccarry, hcarry,
                         *, tm, tiles_per_seq, alpha):
    halo = xbuf.shape[1] - tm
    groups = tm // SUBLANES
    supers = groups // SUBLANES
    taps = RG_CONV - 1

    @pl.when(pl.program_id(0) % tiles_per_seq == 0)
    def _():
        ccarry[...] = jnp.zeros_like(ccarry)
        hcarry[...] = jnp.zeros_like(hcarry)

    xb = x_ref[...].astype(BF16)
    log_sig_lam = jax.nn.log_sigmoid(lam_ref[...])

    def project(h):
        c0 = h * HD
        proj = _dot(xb, wi_ref[:, 2 * c0:2 * c0 + 2 * HD])
        gbuf[:, c0:c0 + HD] = _gelu(proj[:, 0:HD])
        xs = xbuf.at[h]
        xs[0:halo, :] = ccarry[h]
        xs[halo:halo + tm, :] = proj[:, HD:2 * HD]
        ccarry[h] = xs[tm:tm + halo, :]
        co_ref[0, :, c0:c0 + HD] = xs[halo + tm - taps:halo + tm, :]

    def conv(h):
        c0 = h * HD
        xs = xbuf.at[h]
        cw = cw_ref[:, c0:c0 + HD]
        cb = cb_ref[:, c0:c0 + HD]
        lock = [xs[pl.ds(halo - taps + m, groups, stride=SUBLANES), :]
                for m in range(SUBLANES + taps)]
        xc = []
        for k in range(SUBLANES):
            acc = cb + lock[k + taps] * cw[taps:taps + 1]
            for j in range(taps):
                acc = acc + lock[k + j] * cw[j:j + 1]
            xc.append(acc)
        xc = jnp.concatenate(xc, axis=0)
        return xc, _dot(xc.astype(BF16), wg_ref[h])

    def scan(h, xc, gt):
        c0 = h * HD
        a, bv = _rglru_gates(xc, gt, h, bg_ref, log_sig_lam)
        pa, pb = a[0:groups], bv[0:groups]
        a_in, b_in = [pa], [pb]
        for k in range(1, SUBLANES):
            ak = a[k * groups:(k + 1) * groups]
            pb = ak * pb + bv[k * groups:(k + 1) * groups]
            pa = ak * pa
            a_in.append(pa)
            b_in.append(pb)
        tot[h, 0] = pa
        tot[h, 1] = pb
        qa = tot[h, 0, pl.ds(0, supers, stride=SUBLANES), :]
        qb = tot[h, 1, pl.ds(0, supers, stride=SUBLANES), :]
        a_sup, b_sup = [qa], [qb]
        for j in range(1, SUBLANES):
            aj = tot[h, 0, pl.ds(j, supers, stride=SUBLANES), :]
            qb = aj * qb + tot[h, 1, pl.ds(j, supers, stride=SUBLANES), :]
            qa = aj * qa
            a_sup.append(qa)
            b_sup.append(qb)
        hc = hcarry[h]
        for s in range(supers):
            sup[h, pl.ds(s, 1), :] = hc
            hc = qa[s:s + 1] * hc + qb[s:s + 1]
        hcarry[h] = hc
        ho_ref[0, :, c0:c0 + HD] = hc
        h_sup = sup[h]
        hin[h, pl.ds(0, supers, stride=SUBLANES), :] = h_sup
        for j in range(1, SUBLANES):
            hin[h, pl.ds(j, supers, stride=SUBLANES), :] = a_sup[j - 1] * h_sup + b_sup[j - 1]
        h_grp = hin[h]
        for k in range(SUBLANES):
            hbuf[h, pl.ds(k, groups, stride=SUBLANES), :] = a_in[k] * h_grp + b_in[k]

    convs = {}
    for h in range(HEADS + 2):
        if h < HEADS:
            project(h)
        if 1 <= h <= HEADS:
            convs[h - 1] = conv(h - 1)
        if h >= 2:
            scan(h - 2, *convs.pop(h - 2))
    for h in range(HEADS):
        c0 = h * HD
        ybuf[:, c0:c0 + HD] = (hbuf[h] * gbuf[:, c0:c0 + HD]).astype(BF16)
    _project_and_norm(x_ref, ybuf, wo_ref, g_ref, b_ref, y_ref, alpha, NORM_SLAB)


def _rglru_in_proj(w_in):
    lead = w_in.shape[:-1]
    w = w_in.astype(BF16).reshape(*lead, 2, HEADS, HD)
    return jnp.swapaxes(w, -3, -2).reshape(*lead, 2 * HEADS * HD)


def _rglru_call(x, h0, cst_tm, wi, cw, cb, wg, bg, lam, wo, g, b, *, seq_len, alpha):
    T = x.shape[0]
    has_state = h0 is not None
    n_seq = T // seq_len
    operands = ([h0, cst_tm] if has_state else []) + [wi, cw, cb, wg, bg, lam, wo, g, b]
    specs_args = [_layer_operand(w) for w in operands]
    args = [x] + [a for _, a in specs_args]
    if has_state:
        tm, halo = T, (RG_CONV - 1) * (T // seq_len)
        h0_shape, cst_shape = _layer_shape(h0), (halo, D_MODEL)
        body = functools.partial(_rglru_state_kernel, tm=tm, step=T // seq_len, alpha=alpha)
        row = pl.BlockSpec((tm, D_MODEL), lambda i: (i, 0))
        out_shape = [jax.ShapeDtypeStruct((T, D_MODEL), F32),
                     jax.ShapeDtypeStruct(h0_shape, F32),
                     jax.ShapeDtypeStruct(cst_shape, F32)]
        out_specs = [row, pl.BlockSpec(h0_shape, lambda i: (0, 0)),
                     pl.BlockSpec(cst_shape, lambda i: (0, 0))]
        scratch = [pltpu.VMEM((halo + tm, D_MODEL), F32), pltpu.VMEM((tm, D_MODEL), BF16)]
    else:
        tm, halo, tiles_per_seq = TM_PROMPT, SUBLANES, seq_len // TM_PROMPT
        groups = tm // SUBLANES
        body = functools.partial(_rglru_prompt_kernel, tm=tm, tiles_per_seq=tiles_per_seq, alpha=alpha)
        row = pl.BlockSpec((tm, D_MODEL), lambda i: (i, 0))
        out_shape = [jax.ShapeDtypeStruct((T, D_MODEL), F32),
                     jax.ShapeDtypeStruct((n_seq, 1, D_MODEL), F32),
                     jax.ShapeDtypeStruct((n_seq, RG_CONV - 1, D_MODEL), F32)]
        out_specs = [row,
                     pl.BlockSpec((1, 1, D_MODEL), lambda i: (i // tiles_per_seq, 0, 0)),
                     pl.BlockSpec((1, RG_CONV - 1, D_MODEL), lambda i: (i // tiles_per_seq, 0, 0))]
        scratch = [pltpu.VMEM((HEADS, halo + tm, HD), F32),
                   pltpu.VMEM((tm, D_MODEL), F32),
                   pltpu.VMEM((HEADS, tm, HD), F32),
                   pltpu.VMEM((HEADS, 2, groups, HD), F32),
                   pltpu.VMEM((HEADS, groups, HD), F32),
                   pltpu.VMEM((HEADS, groups // SUBLANES, HD), F32),
                   pltpu.VMEM((tm, D_MODEL), BF16),
                   pltpu.VMEM((HEADS, halo, HD), F32),
                   pltpu.VMEM((HEADS, 1, HD), F32)]
    in_specs = [row] + [s for s, _ in specs_args]
    n_tiles = T // tm
    return pl.pallas_call(
        body,
        grid=(n_tiles,),
        in_specs=in_specs,
        out_specs=out_specs,
        out_shape=out_shape,
        scratch_shapes=scratch,
        compiler_params=_params(),
        name="rglru_state" if has_state else "rglru_prompt",
    )(*args)


def _split3(x):
    hi = x.astype(BF16)
    r = x - hi.astype(F32)
    mid = r.astype(BF16)
    lo = (r - mid.astype(F32)).astype(BF16)
    return hi, mid, lo


def _lower_bound(lower_ref, h, layer):
    rows = [lower_ref[n, h] for n in range(lower_ref.shape[0])]
    m = functools.reduce(jnp.maximum, rows)
    es = [jnp.exp(r - m) for r in rows]
    tot = functools.reduce(lambda p, q: p + q, es)
    sm = [e / tot for e in es]
    cs = functools.reduce(lambda p, q: p + q, sm[:layer + 1])
    return jnp.maximum(cs - sm[0], 0.0)


def _chunk_cumsum(cb_ref, tb_ref, eb_ref, tm, chunk):
    groups = tm // SUBLANES
    p = cb_ref[pl.ds(0, groups, stride=SUBLANES), :]
    for k in range(1, SUBLANES):
        p = p + cb_ref[pl.ds(k, groups, stride=SUBLANES), :]
        cb_ref[pl.ds(k, groups, stride=SUBLANES), :] = p
    if chunk > SUBLANES:
        gc = chunk // SUBLANES
        nch = tm // chunk
        tb_ref[...] = p
        e = jnp.zeros((nch, HD), F32)
        eb_ref[pl.ds(0, nch, stride=gc), :] = e
        for j in range(1, gc):
            e = e + tb_ref[pl.ds(j - 1, nch, stride=gc), :]
            eb_ref[pl.ds(j, nch, stride=gc), :] = e
        off = eb_ref[...]
        for k in range(SUBLANES):
            cb_ref[pl.ds(k, groups, stride=SUBLANES), :] = (
                cb_ref[pl.ds(k, groups, stride=SUBLANES), :] + off)
    return p


def _hgrn_front(part, lower_ref, h, layer, tm, chunk,
                cb_ref, tb_ref, eb_ref, kk_ref, v_ref, sg_ref, qs_ref, qe_ref):
    lb = _lower_bound(lower_ref, h, layer)
    fg = lb + (1.0 - lb) * _sigmoid(part(1))
    cb_ref[...] = jnp.log(jnp.maximum(fg, F_FLOOR))
    kk_ref[...] = 1.0 - fg
    q = part(0)
    qs = q * _sigmoid(q) * (HD ** -0.5)
    v_ref[...] = part(2)
    gg = part(3)
    sg_ref[...] = gg * _sigmoid(gg)
    totals = _chunk_cumsum(cb_ref, tb_ref, eb_ref, tm, chunk)
    cum = cb_ref[...]
    qs_ref[...] = qs
    qe_ref[...] = qs * jnp.exp(cum)
    safe = jnp.max(-cum) <= SAFE_DECAY
    return safe, totals


def _same_chunk_mask(chunk):
    ri = lax.broadcasted_iota(jnp.int32, (BLK, BLK), 0)
    ci = lax.broadcasted_iota(jnp.int32, (BLK, BLK), 1)
    lc = chunk.bit_length() - 1
    return ((ri >> lc) == (ci >> lc)) & (ci <= ri)


def _inverse_decayed_keys(kk, cum):
    return kk * jnp.exp(jnp.minimum(-cum, SAFE_DECAY))


def _robust_same_chunk_scores(rows, chunk, cb_ref, kk_ref, qs_ref):
    ri = lax.broadcasted_iota(jnp.int32, (BLK, BLK), 0)
    ci = lax.broadcasted_iota(jnp.int32, (BLK, BLK), 1)
    cum = cb_ref[rows, :]
    qs = qs_ref[rows, :]
    kk = kk_ref[rows, :]
    hi, mid, lo = _split3(cum)
    acc = jnp.where(ri == ci, _dot_nt(qs.astype(BF16), kk.astype(BF16)), 0.0)
    for lvl in range(chunk.bit_length() - 1):
        m = 1 << lvl
        pivot = ((ri >> (lvl + 1)) << (lvl + 1)) + (m - 1)
        sel = jnp.where(ci == pivot, 1.0, 0.0).astype(BF16)
        ref_cum = _dot(sel, hi) + _dot(sel, mid) + _dot(sel, lo)
        e = jnp.exp(-jnp.abs(cum - ref_cum))
        keep = (((ri >> (lvl + 1)) == (ci >> (lvl + 1)))
                & ((ri & (2 * m - 1)) >= m) & ((ci & (2 * m - 1)) < m))
        s = _dot_nt((qs * e).astype(BF16), (kk * e).astype(BF16))
        acc = acc + jnp.where(keep, s, 0.0)
    return acc


def _rms_gate(o, ng, sg):
    return o * lax.rsqrt(jnp.mean(o * o, axis=-1, keepdims=True) + RMS_EPS) * ng * sg


def _hgrn_prompt_kernel(x_ref, lower_ref, wi_ref, ng_ref, wo_ref, g_ref, b_ref, y_ref, so_ref,
                        xb_ref, p_ref, cb_ref, tb_ref, eb_ref, kk_ref, v_ref, sg_ref, qs_ref,
                        qe_ref, kd_ref, a_ref, oh_ref, ds_ref, ec_ref, st_ref, ob_ref, s_ref,
                        *, tm, layer, tiles_per_seq, alpha):
    chunk = PROMPT_CHUNK
    assert BLK == 2 * chunk
    n_blocks = tm // BLK
    i = pl.program_id(0)

    @pl.when(i % tiles_per_seq == 0)
    def _():
        s_ref[...] = jnp.zeros_like(s_ref)

    xb_ref[...] = x_ref[...].astype(BF16)
    second = lax.broadcasted_iota(jnp.int32, (BLK, HD), 0) >= chunk
    ri = lax.broadcasted_iota(jnp.int32, (BLK, BLK), 0)
    ci = lax.broadcasted_iota(jnp.int32, (BLK, BLK), 1)
    cross = (ri >= chunk) & (ci < chunk)
    same = _same_chunk_mask(chunk)

    def project(h, slot):
        p_ref[slot] = _dot(xb_ref[...], wi_ref[h])

    def views(slot):
        return tuple(r.at[slot] for r in (cb_ref, tb_ref, eb_ref, kk_ref, v_ref, sg_ref, qs_ref,
                                          qe_ref, kd_ref, a_ref, oh_ref, ds_ref, ec_ref))

    def scores(h, slot):
        cb, tb, eb, kk_s, v_s, sg_s, qs_s, qe_s, kd_s, a_s, _, ds_s, ec_s = views(slot)
        safe, _ = _hgrn_front(lambda k: p_ref[slot, :, k * HD:(k + 1) * HD], lower_ref, h, layer,
                              tm, chunk, cb, tb, eb, kk_s, v_s, sg_s, qs_s, qe_s)
        for nb in range(n_blocks):
            rows = pl.ds(nb * BLK, BLK)
            cum = cb[rows, :]
            kk = kk_s[rows, :]
            last0 = cb[pl.ds(nb * BLK + chunk - 1, 1), :]
            last1 = cb[pl.ds(nb * BLK + BLK - 1, 1), :]
            kd = kk * jnp.exp(jnp.where(second, last1, last0) - cum)
            kd_s[rows, :] = kd
            keys = jnp.concatenate([_inverse_decayed_keys(kk, cum), kd], axis=0).astype(BF16)
            s2 = _dot_nt(qe_s[rows, :].astype(BF16), keys)
            a = jnp.where(same, s2[:, 0:BLK], 0.0) + jnp.where(cross, s2[:, BLK:2 * BLK], 0.0)
            a_s[nb] = a.astype(BF16)
            k_blk = kd * jnp.where(second, 1.0, jnp.exp(last1))
            ds_s[nb] = _dot(k_blk.T.astype(BF16), v_s[rows, :].astype(BF16))
            ec_s[nb] = jnp.broadcast_to(jnp.exp(last0 + last1), (HD, HD)).T
        return safe

    def robust_scores(slot):
        cb, _, _, kk_s, _, _, qs_s, qe_s, kd_s, a_s, _, _, _ = views(slot)

        def fix(nb, _):
            rows = pl.ds(pl.multiple_of(nb * BLK, BLK), BLK)
            s1 = _dot_nt(qe_s[rows, :].astype(BF16), kd_s[rows, :].astype(BF16))
            a = _robust_same_chunk_scores(rows, chunk, cb, kk_s, qs_s)
            a_s[nb] = (a + jnp.where(cross, s1, 0.0)).astype(BF16)
            return 0
        lax.fori_loop(0, n_blocks, fix, 0)

    def advance_state(h, slot):
        ds_s, ec_s = views(slot)[11:13]
        state = s_ref[h]
        for nb in range(n_blocks):
            st_ref[slot, nb] = state
            state = ec_s[nb] * state + ds_s[nb]
        s_ref[h] = state

    def outputs(h, slot):
        cb, _, _, _, v_s, sg_s, _, qe_s, _, a_s, oh_s, _, _ = views(slot)
        for nb in range(n_blocks):
            rows = pl.ds(nb * BLK, BLK)
            last0 = cb[pl.ds(nb * BLK + chunk - 1, 1), :]
            q_blk = qe_s[rows, :] * jnp.where(second, jnp.exp(last0), 1.0)
            lhs = jnp.concatenate([a_s[nb], q_blk.astype(BF16)], axis=1)
            rhs = jnp.concatenate([v_s[rows, :].astype(BF16), st_ref[slot, nb].astype(BF16)], axis=0)
            oh_s[rows, :] = _dot(lhs, rhs)
        ob_ref[h] = _rms_gate(oh_s[...], ng_ref[...], sg_s[...]).astype(BF16)

    project(0, 0)

    def pair(h0, last):
        project(h0 + 1, 1)
        safe0 = scores(h0, 0)
        advance_state(h0, 0)
        outputs(h0, 0)
        if not last:
            project(h0 + 2, 0)
        safe1 = scores(h0 + 1, 1)
        advance_state(h0 + 1, 1)
        outputs(h0 + 1, 1)

        @pl.when(jnp.logical_not(jnp.logical_and(safe0, safe1)))
        def _():
            for slot in range(2):
                robust_scores(slot)
                outputs(h0 + slot, slot)
        return 0

    lax.fori_loop(0, HEADS // 2 - 1, lambda hh, _: pair(2 * hh, False), 0)
    pair(HEADS - 2, True)
    for h in range(HEADS):
        xb_ref[:, h * HD:(h + 1) * HD] = ob_ref[h]
    _project_and_norm(x_ref, xb_ref, wo_ref, g_ref, b_ref, y_ref, alpha, NORM_SLAB)

    @pl.when(i % tiles_per_seq == tiles_per_seq - 1)
    def _():
        so_ref[0] = s_ref[...]


def _hgrn_prep(hg_lower, wi, ng, wo):
    n_layers = hg_lower.shape[0]
    lower = hg_lower.reshape(n_layers, HEADS, 1, HD)
    wih = (wi.astype(BF16).reshape(n_layers, D_MODEL, 4, HEADS, HD).transpose(0, 3, 1, 2, 4)
           .reshape(n_layers, HEADS, D_MODEL, 4 * HD))
    return lower, wih, ng.reshape(n_layers, 1, HD), wo.astype(BF16)


def _hgrn_prompt_call(x, prep, layer, g, b, *, seq_len, alpha):
    lower, wih, ng, wo = prep
    T = x.shape[0]
    tm = TM_PROMPT
    tiles_per_seq = seq_len // tm
    n_seq = T // seq_len
    row = pl.BlockSpec((tm, D_MODEL), lambda i: (i, 0))
    specs_args = [_layer_operand(w) for w in (lower, (wih, layer), (ng, layer), (wo, layer), g, b)]
    head_buf = pltpu.VMEM((2, tm, HD), F32)
    scratch = [pltpu.VMEM((tm, D_MODEL), BF16),
               pltpu.VMEM((2, tm, 4 * HD), F32),
               head_buf,
               pltpu.VMEM((2, tm // SUBLANES, HD), F32),
               pltpu.VMEM((2, tm // SUBLANES, HD), F32),
               head_buf, head_buf, head_buf, head_buf, head_buf,
               head_buf,
               pltpu.VMEM((2, tm // BLK, BLK, BLK), BF16),
               head_buf,
               pltpu.VMEM((2, tm // BLK, HD, HD), F32),
               pltpu.VMEM((2, tm // BLK, HD, HD), F32),
               pltpu.VMEM((2, tm // BLK, HD, HD), F32),
               pltpu.VMEM((HEADS, tm, HD), BF16),
               pltpu.VMEM((HEADS, HD, HD), F32)]
    y, so = pl.pallas_call(
        functools.partial(_hgrn_prompt_kernel, tm=tm, layer=layer,
                          tiles_per_seq=tiles_per_seq, alpha=alpha),
        grid=(T // tm,),
        in_specs=[row] + [s for s, _ in specs_args],
        out_specs=[row, pl.BlockSpec((1, HEADS, HD, HD), lambda i: (i // tiles_per_seq, 0, 0, 0))],
        out_shape=[jax.ShapeDtypeStruct((T, D_MODEL), F32),
                   jax.ShapeDtypeStruct((n_seq, HEADS, HD, HD), F32)],
        scratch_shapes=scratch,
        compiler_params=_params(),
        name="hgrn_prompt",
    )(x, *[a for _, a in specs_args])
    return y, so


def _hgrn_sample_front_kernel(x_ref, lower_ref, wi_ref,
                              oi_ref, qe_ref, v_ref, sg_ref, kdt_ref, et_ref,
                              xb_ref, p_ref, cb_ref, kk_ref, qs_ref, a_ref, last_ref,
                              *, tm, chunk, layer):
    assert chunk == SUBLANES
    xb_ref[...] = x_ref[...].astype(BF16)
    groups = tm // SUBLANES
    n_blocks = tm // BLK
    same = _same_chunk_mask(chunk)

    def head(h, _):
        proj = _dot(xb_ref[...], wi_ref[h])
        for t in range(chunk):
            for k in range(4):
                p_ref[k, pl.ds(t, groups, stride=chunk), :] = (
                    proj[t * groups:(t + 1) * groups, k * HD:(k + 1) * HD])
        safe, totals = _hgrn_front(lambda k: p_ref[k], lower_ref, h, layer, tm, chunk,
                                   cb_ref, None, None, kk_ref, v_ref.at[h], sg_ref.at[h],
                                   qs_ref, qe_ref.at[h])
        for k in range(SUBLANES):
            last_ref[pl.ds(k, groups, stride=SUBLANES), :] = totals
        for nb in range(n_blocks):
            r0 = nb * BLK
            rows = pl.ds(r0, BLK)
            cum = cb_ref[rows, :]
            kk = kk_ref[rows, :]
            s = _dot_nt(qe_ref[h, rows, :].astype(BF16), _inverse_decayed_keys(kk, cum).astype(BF16))
            a_ref[nb] = jnp.where(same, s, 0.0).astype(BF16)
            last = last_ref[rows, :]
            kdt_ref[h, :, r0:r0 + BLK] = (kk * jnp.exp(last - cum)).T
            et_ref[h, :, r0:r0 + BLK] = jnp.exp(last).T

        @pl.when(jnp.logical_not(safe))
        def _():
            def fix(nb, _):
                rows = pl.ds(pl.multiple_of(nb * BLK, BLK), BLK)
                a_ref[nb] = _robust_same_chunk_scores(rows, chunk, cb_ref, kk_ref, qs_ref).astype(BF16)
                return 0
            lax.fori_loop(0, n_blocks, fix, 0)

        for nb in range(n_blocks):
            rows = pl.ds(nb * BLK, BLK)
            oi_ref[h, rows, :] = _dot(a_ref[nb], v_ref[h, rows, :].astype(BF16))
        return 0

    lax.fori_loop(0, HEADS, head, 0)


def _hgrn_sample_state_kernel(qe_ref, oi_ref, v_ref, sg_ref, kdt_ref, et_ref, s0_ref, ng_ref,
                              *rest, chunk):
    op_ref, so_ref = rest[-2:]
    row_seq = lax.broadcasted_iota(jnp.int32, (BLK, HD), 0) >> (chunk.bit_length() - 1)

    @pl.when(pl.program_id(0) > 0)
    def _():
        so_ref[...] = jnp.zeros_like(so_ref)

    def head(h, _):
        kdt = kdt_ref[h].astype(BF16)
        et = et_ref[h]
        v = v_ref[h]
        for sq in range(BLK // chunk):
            rows = slice(sq * chunk, (sq + 1) * chunk)
            s0 = s0_ref[sq, h]
            o = oi_ref[h, rows, :] + _dot(qe_ref[h, rows, :].astype(BF16), s0.astype(BF16))
            op_ref[h, rows, :] = _rms_gate(o, ng_ref[...], sg_ref[h, rows, :])
            vm = jnp.where(row_seq == sq, v, 0.0).astype(BF16)
            e_col = jnp.broadcast_to(et[:, sq * chunk:sq * chunk + 1], (HD, HD))
            so_ref[sq, h] = e_col * s0 + _dot(kdt, vm)
        return 0

    @pl.when(pl.program_id(0) == 0)
    def _():
        lax.fori_loop(0, HEADS, head, 0)


def _hgrn_sample_out_kernel(x_ref, op_ref, wo_ref, g_ref, b_ref, y_ref, ob_ref, *, chunk, alpha):
    batch = x_ref.shape[0] // chunk
    for h in range(HEADS):
        for t in range(chunk):
            ob_ref[t * batch:(t + 1) * batch, h * HD:(h + 1) * HD] = (
                op_ref[h, pl.ds(t, batch, stride=chunk), :].astype(BF16))
    _project_and_norm(x_ref, ob_ref, wo_ref, g_ref, b_ref, y_ref, alpha, NORM_SLAB)


def _hgrn_sample_call(x, s0, so_prev, prep, layer, g, b, *, seq_len, alpha):
    lower, wih, ng, wo = prep
    T = x.shape[0]
    chunk = seq_len
    heads_rows = jax.ShapeDtypeStruct((HEADS, T, HD), F32)
    heads_cols = jax.ShapeDtypeStruct((HEADS, HD, T), F32)
    full = lambda s: pl.BlockSpec(s, lambda i: (0,) * len(s))
    head_buf = pltpu.VMEM((T, HD), F32)
    specs_args = [_layer_operand(w) for w in (lower, (wih, layer))]
    oi, qe, v, sg, kdt, et = pl.pallas_call(
        functools.partial(_hgrn_sample_front_kernel, tm=T, chunk=chunk, layer=layer),
        grid=(1,),
        in_specs=[full(x.shape)] + [s for s, _ in specs_args],
        out_specs=[full(heads_rows.shape)] * 4 + [full(heads_cols.shape)] * 2,
        out_shape=[heads_rows] * 4 + [heads_cols] * 2,
        scratch_shapes=[pltpu.VMEM((T, D_MODEL), BF16), pltpu.VMEM((4, T, HD), F32),
                        head_buf, head_buf, head_buf,
                        pltpu.VMEM((T // BLK, BLK, BLK), BF16), head_buf],
        compiler_params=_params(),
        name="hgrn_sample_front",
    )(x, *[a for _, a in specs_args])
    seqs = BLK // chunk
    steps = T // BLK
    n_layers = s0.shape[0]
    passes = n_layers if so_prev is None else 1
    blk = lambda p, i: jnp.where(p == 0, i, steps - 1)
    by_rows = pl.BlockSpec((HEADS, BLK, HD), lambda p, i: (0, blk(p, i), 0))
    by_cols = pl.BlockSpec((HEADS, HD, BLK), lambda p, i: (0, 0, blk(p, i)))
    state_in = pl.BlockSpec((None, seqs, HEADS, HD, HD), lambda p, i: (layer, blk(p, i), 0, 0, 0))
    state_out = pl.BlockSpec((None, seqs, HEADS, HD, HD),
                             lambda p, i: ((layer + p) % n_layers, i, 0, 0, 0))
    ng_spec, ng_arr = _layer_operand((ng, layer))
    in_specs = [by_rows, by_rows, by_rows, by_rows, by_cols, by_cols, state_in, ng_spec]
    args = [qe, oi, v, sg, kdt, et, s0, ng_arr]
    aliases = {}
    if so_prev is not None:
        in_specs.append(pl.BlockSpec(memory_space=pl.ANY))
        args.append(so_prev)
        aliases = {len(args) - 1: 1}
    op, so = pl.pallas_call(
        functools.partial(_hgrn_sample_state_kernel, chunk=chunk),
        grid=(passes, steps),
        in_specs=in_specs,
        out_specs=[by_rows, state_out],
        out_shape=[heads_rows, jax.ShapeDtypeStruct(s0.shape, F32)],
        input_output_aliases=aliases,
        compiler_params=_params(2),
        name="hgrn_sample_state",
    )(*args)
    specs_args = [_layer_operand(w) for w in ((wo, layer), g, b)]
    y = pl.pallas_call(
        functools.partial(_hgrn_sample_out_kernel, chunk=chunk, alpha=alpha),
        grid=(1,),
        in_specs=[full(x.shape), full(op.shape)] + [s for s, _ in specs_args],
        out_specs=full(x.shape),
        out_shape=jax.ShapeDtypeStruct(x.shape, F32),
        scratch_shapes=[pltpu.VMEM((T, D_MODEL), BF16)],
        compiler_params=_params(),
        name="hgrn_sample_out",
    )(x, op, *[a for _, a in specs_args])
    return y, so


def _to_time_major(a):
    return jnp.transpose(a, (1, 0, 2)).reshape(-1, a.shape[-1])


def _from_time_major(a, batch):
    return jnp.transpose(a.reshape(-1, batch, a.shape[-1]), (1, 0, 2))


def kernel(x_prompt, x_sample, state_rglru_h, state_rglru_conv, state_hgrn_s, state_ffn_conv,
           ln_g, ln_b, rg_w_in, rg_conv_w, rg_conv_b, rg_gate_w, rg_gate_b, rg_lambda, rg_w_out,
           hg_lower, hg_w_in, hg_norm_g, hg_w_out, ffn_w_in, ffn_conv_w, ffn_conv_b, ffn_w_out):
    depth = ln_g.shape[0]
    alpha = (2.0 * depth) ** 0.25
    pb, pl_len, _ = x_prompt.shape
    sb, sl_len, _ = x_sample.shape

    n_rg, n_ffn = rg_w_in.shape[0], ffn_w_in.shape[0]
    lng = ln_g.reshape(depth * 2, 1, D_MODEL)
    lnb = ln_b.reshape(depth * 2, 1, D_MODEL)
    rg_wi = _rglru_in_proj(rg_w_in)
    rg_wg = jnp.concatenate([rg_gate_w[:, 0], rg_gate_w[:, 1]], axis=-1).astype(BF16)
    rg_wo = rg_w_out.astype(BF16)
    rg_cb = rg_conv_b.reshape(n_rg, 1, D_MODEL)
    rg_lam = rg_lambda.reshape(n_rg, 1, D_MODEL)
    hg = _hgrn_prep(hg_lower, hg_w_in, hg_norm_g, hg_w_out)
    ffn_wi = ffn_w_in.astype(BF16)
    ffn_wo = ffn_w_out.astype(BF16)
    ffn_cb = ffn_conv_b.reshape(n_ffn, 1, D_FF)
    rg_cst = jnp.transpose(state_rglru_conv, (0, 2, 1, 3)).reshape(n_rg, -1, D_MODEL)
    ffn_cst = jnp.transpose(state_ffn_conv, (0, 2, 1, 3)).reshape(n_ffn, -1, D_FF)

    xp = x_prompt.reshape(pb * pl_len, D_MODEL)
    xs = _to_time_major(x_sample)
    p_h, p_rc, p_s, p_fc = [], [], [], []
    s_h, s_rc, s_fc = [], [], []
    s_s = None
    for i in range(depth):
        j = i // 2
        g0, b0 = (lng, 2 * i), (lnb, 2 * i)
        if i % 2 == 0:
            w = ((rg_wi, j), (rg_conv_w, j), (rg_cb, j), (rg_wg, j), (rg_gate_b, j), (rg_lam, j),
                 (rg_wo, j), g0, b0)
            xp, ho, co = _rglru_call(xp, None, None, *w, seq_len=pl_len, alpha=alpha)
            p_h.append(ho[:, 0])
            p_rc.append(co)
            xs, ho, co = _rglru_call(xs, (state_rglru_h, j), (rg_cst, j), *w,
                                     seq_len=sl_len, alpha=alpha)
            s_h.append(ho)
            s_rc.append(co)
        else:
            xp, so = _hgrn_prompt_call(xp, hg, j, g0, b0, seq_len=pl_len, alpha=alpha)
            p_s.append(so)
            xs, s_s = _hgrn_sample_call(xs, state_hgrn_s, s_s, hg, j, g0, b0,
                                        seq_len=sl_len, alpha=alpha)
        w = ((ffn_wi, i), (ffn_conv_w, i), (ffn_cb, i), (ffn_wo, i), (lng, 2 * i + 1), (lnb, 2 * i + 1))
        xp, fo = _ffn_call(xp, None, *w, seq_len=pl_len, alpha=alpha)
        p_fc.append(fo)
        xs, fo = _ffn_call(xs, (ffn_cst, i), *w, seq_len=sl_len, alpha=alpha)
        s_fc.append(fo)

    def stacked_from_time_major(parts):
        a = jnp.stack(parts)
        a = a.reshape(a.shape[0], -1, sb, a.shape[-1])
        return jnp.transpose(a, (0, 2, 1, 3))

    return (xp.reshape(x_prompt.shape), _from_time_major(xs, sb),
            jnp.stack(p_h), jnp.stack(p_rc), jnp.stack(p_s), jnp.stack(p_fc),
            jnp.stack(s_h), stacked_from_time_major(s_rc), s_s, stacked_from_time_major(s_fc))
```

```python
import functools

import jax
import jax.numpy as jnp
from jax import lax
from jax.experimental import pallas as pl
from jax.experimental.pallas import tpu as pltpu

F32 = jnp.float32
BF16 = jnp.bfloat16

D_MODEL = 1024
D_FF = 2816
HEADS = 8
HD = 128
RG_CONV = 4
FFN_CONV = 3
RG_C = 8.0
LN_EPS = 1e-5
RMS_EPS = 1e-6
F_FLOOR = 1e-30
SUBLANES = 8
BLK = 128
PROMPT_CHUNK = 64
SAFE_DECAY = 80.0
FF_TILE = 256
TM_PROMPT = 1024
TM_FFN = 1024
NORM_SLAB = 256
VMEM_V7X = 64 * 1024 * 1024
VMEM_LIMIT = VMEM_V7X - 8 * 1024 * 1024


def _dot(a, b):
    return jnp.dot(a, b, preferred_element_type=F32)


def _dot_nt(a, b):
    return lax.dot_general(a, b, (((1,), (1,)), ((), ())), preferred_element_type=F32)


def _sigmoid(x):
    return 0.5 * jnp.tanh(0.5 * x) + 0.5


def _gelu(x):
    c = (2.0 / jnp.pi) ** 0.5
    return x * (0.5 * jnp.tanh(x * (c + (c * 0.044715) * (x * x))) + 0.5)


def _sqrt_nonneg(y):
    return jnp.where(y > 0.0, y * lax.rsqrt(y), 0.0)


def _resident(shape):
    nd = len(shape)
    return pl.BlockSpec(shape, lambda *_: (0,) * nd, pipeline_mode=pl.Buffered(1))


def _layer_shape(w):
    return w[0].shape[1:] if isinstance(w, tuple) else w.shape


def _layer_operand(w):
    if not isinstance(w, tuple):
        return _resident(w.shape), w
    arr, layer = w
    tail = (0,) * (arr.ndim - 1)
    spec = pl.BlockSpec((None,) + arr.shape[1:], lambda *_: (layer,) + tail,
                        pipeline_mode=pl.Buffered(1))
    return spec, arr


def _deepnorm(x, f, g, b, alpha):
    z = alpha * x + f
    mu = jnp.mean(z, axis=-1, keepdims=True)
    zc = z - mu
    var = jnp.mean(zc * zc, axis=-1, keepdims=True)
    return zc * lax.rsqrt(var + LN_EPS) * g + b


def _project_and_norm(x_ref, lhs_ref, wo_ref, g_ref, b_ref, y_ref, alpha, slab):
    rows = x_ref.shape[0]
    slab = min(slab, rows)
    for r0 in range(0, rows, slab):
        sl = pl.ds(r0, slab)
        f = _dot(lhs_ref[sl, :], wo_ref[...])
        y_ref[sl, :] = _deepnorm(x_ref[sl, :], f, g_ref[...], b_ref[...], alpha)


def _params(n_axes=1):
    return pltpu.CompilerParams(dimension_semantics=("arbitrary",) * n_axes,
                                vmem_limit_bytes=VMEM_LIMIT)


def _ffn_kernel(*refs, tm, step, tiles_per_seq, has_state, alpha):
    if has_state:
        (x_ref, st_ref, wi_ref, cw_ref, cb_ref, wo_ref, g_ref, b_ref,
         y_ref, so_ref, gbuf, hbuf) = refs
        carry = None
    else:
        (x_ref, wi_ref, cw_ref, cb_ref, wo_ref, g_ref, b_ref,
         y_ref, so_ref, gbuf, hbuf, carry) = refs
    halo = gbuf.shape[0] - tm
    xb = x_ref[...].astype(BF16)
    if not has_state:
        @pl.when(pl.program_id(0) % tiles_per_seq == 0)
        def _():
            carry[...] = jnp.zeros_like(carry)
    for j in range(D_FF // FF_TILE):
        c0 = j * FF_TILE
        g = _dot(xb, wi_ref[:, c0:c0 + FF_TILE])
        u = _dot(xb, wi_ref[:, D_FF + c0:D_FF + c0 + FF_TILE])
        if has_state:
            gbuf[0:halo, :] = st_ref[:, c0:c0 + FF_TILE]
        else:
            gbuf[0:halo, :] = carry[:, c0:c0 + FF_TILE]
        gbuf[halo:halo + tm, :] = g
        g1 = gbuf[halo - step:halo - step + tm, :]
        g2 = gbuf[halo - 2 * step:halo - 2 * step + tm, :]
        cw = cw_ref[:, c0:c0 + FF_TILE]
        gc = g * cw[2:3] + g1 * cw[1:2] + g2 * cw[0:1] + cb_ref[:, c0:c0 + FF_TILE]
        hbuf[:, c0:c0 + FF_TILE] = (_gelu(gc) * u).astype(BF16)
        if has_state:
            so_ref[:, c0:c0 + FF_TILE] = gbuf[tm:tm + halo, :]
        else:
            carry[:, c0:c0 + FF_TILE] = gbuf[tm:tm + halo, :]
            so_ref[0, :, c0:c0 + FF_TILE] = gbuf[halo + tm - 2:halo + tm, :]
    _project_and_norm(x_ref, hbuf, wo_ref, g_ref, b_ref, y_ref, alpha, NORM_SLAB)


def _ffn_call(x, state_tm, wi, cw, cb, wo, g, b, *, seq_len, alpha):
    T = x.shape[0]
    has_state = state_tm is not None
    if has_state:
        step = T // seq_len
        tm, halo, tiles_per_seq = T, (FFN_CONV - 1) * step, 1
        st_shape = (halo, D_FF)
    else:
        tm, step, halo, tiles_per_seq = TM_FFN, 1, SUBLANES, seq_len // TM_FFN
    n_tiles = T // tm
    n_seq = T // seq_len
    row = pl.BlockSpec((tm, D_MODEL), lambda i: (i, 0))
    operands = ([state_tm] if has_state else []) + [wi, cw, cb, wo, g, b]
    specs_args = [_layer_operand(w) for w in operands]
    in_specs = [row] + [s for s, _ in specs_args]
    args = [x] + [a for _, a in specs_args]
    scratch = [pltpu.VMEM((halo + tm, FF_TILE), F32), pltpu.VMEM((tm, D_FF), BF16)]
    if has_state:
        so_shape = jax.ShapeDtypeStruct(st_shape, F32)
        so_spec = pl.BlockSpec(st_shape, lambda i: (0, 0))
    else:
        so_shape = jax.ShapeDtypeStruct((n_seq, FFN_CONV - 1, D_FF), F32)
        so_spec = pl.BlockSpec((1, FFN_CONV - 1, D_FF), lambda i: (i // tiles_per_seq, 0, 0))
        scratch.append(pltpu.VMEM((halo, D_FF), F32))
    return pl.pallas_call(
        functools.partial(_ffn_kernel, tm=tm, step=step, tiles_per_seq=tiles_per_seq,
                          has_state=has_state, alpha=alpha),
        grid=(n_tiles,),
        in_specs=in_specs,
        out_specs=[row, so_spec],
        out_shape=[jax.ShapeDtypeStruct((T, D_MODEL), F32), so_shape],
        scratch_shapes=scratch,
        compiler_params=_params(),
        name="ffn_state" if has_state else "ffn_prompt",
    )(*args)


def _rglru_gates(xc, gt, h, bg_ref, log_sig_lam):
    c0 = h * HD
    bg = bg_ref[:, c0:c0 + HD]
    r = _sigmoid(gt[:, 0:HD] + bg[0:1])
    ig = _sigmoid(gt[:, HD:2 * HD] + bg[1:2])
    a = jnp.exp(RG_C * r * log_sig_lam[:, c0:c0 + HD])
    return a, _sqrt_nonneg(jnp.maximum(1.0 - a * a, 0.0)) * ig * xc


def _rglru_state_kernel(x_ref, h0_ref, cst_ref, wi_ref, cw_ref, cb_ref, wg_ref, bg_ref, lam_ref,
                        wo_ref, g_ref, b_ref, y_ref, ho_ref, co_ref, xbuf, ybuf, *, tm, step, alpha):
    halo = xbuf.shape[0] - tm
    xb = x_ref[...].astype(BF16)
    xbuf[0:halo, :] = cst_ref[...]
    log_sig_lam = jax.nn.log_sigmoid(lam_ref[...])
    for h in range(HEADS):
        c0 = h * HD
        proj = _dot(xb, wi_ref[:, 2 * c0:2 * c0 + 2 * HD])
        gate = _gelu(proj[:, 0:HD])
        xbuf[halo:halo + tm, c0:c0 + HD] = proj[:, HD:2 * HD]
        cw = cw_ref[:, c0:c0 + HD]
        xc = cb_ref[:, c0:c0 + HD] + proj[:, HD:2 * HD] * cw[RG_CONV - 1:RG_CONV]
        for j in range(RG_CONV - 1):
            back = (RG_CONV - 1 - j) * step
            xc = xc + xbuf[halo - back:halo - back + tm, c0:c0 + HD] * cw[j:j + 1]
        a, bv = _rglru_gates(xc, _dot(xc.astype(BF16), wg_ref[h]), h, bg_ref, log_sig_lam)
        hh = h0_ref[:, c0:c0 + HD]
        for t in range(tm // step):
            rows = slice(t * step, (t + 1) * step)
            hh = a[rows] * hh + bv[rows]
            ybuf[rows, c0:c0 + HD] = (hh * gate[rows]).astype(BF16)
        ho_ref[:, c0:c0 + HD] = hh
    co_ref[...] = xbuf[tm:tm + halo, :]
    _project_and_norm(x_ref, ybuf, wo_ref, g_ref, b_ref, y_ref, alpha, NORM_SLAB)


def _rglru_prompt_kernel(x_ref, wi_ref, cw_ref, cb_ref, wg_ref, bg_ref, lam_ref, wo_ref, g_ref, b_ref,
                         y_ref, ho_ref, co_ref,
                         xbuf, gbuf, hbuf, tot, hin, sup, ybuf, ccarry, hcarry,
                         *, tm, tiles_per_seq, alpha):
    halo = xbuf.shape[1] - tm
    groups = tm // SUBLANES
    supers = groups // SUBLANES
    taps = RG_CONV - 1

    @pl.when(pl.program_id(0) % tiles_per_seq == 0)
    def _():
        ccarry[...] = jnp.zeros_like(ccarry)
        hcarry[...] = jnp.zeros_like(hcarry)

    xb = x_ref[...].astype(BF16)
    log_sig_lam = jax.nn.log_sigmoid(lam_ref[...])

    def project(h):
        c0 = h * HD
        proj = _dot(xb, wi_ref[:, 2 * c0:2 * c0 + 2 * HD])
        gbuf[:, c0:c0 + HD] = _gelu(proj[:, 0:HD])
        xs = xbuf.at[h]
        xs[0:halo, :] = ccarry[h]
        xs[halo:halo + tm, :] = proj[:, HD:2 * HD]
        ccarry[h] = xs[tm:tm + halo, :]
        co_ref[0, :, c0:c0 + HD] = xs[halo + tm - taps:halo + tm, :]

    def conv(h):
        c0 = h * HD
        xs = xbuf.at[h]
        cw = cw_ref[:, c0:c0 + HD]
        cb = cb_ref[:, c0:c0 + HD]
        lock = [xs[pl.ds(halo - taps + m, groups, stride=SUBLANES), :]
                for m in range(SUBLANES + taps)]
        xc = []
        for k in range(SUBLANES):
            acc = cb + lock[k + taps] * cw[taps:taps + 1]
            for j in range(taps):
                acc = acc + lock[k + j] * cw[j:j + 1]
            xc.append(acc)
        xc = jnp.concatenate(xc, axis=0)
        return xc, _dot(xc.astype(BF16), wg_ref[h])

    def scan(h, xc, gt):
        c0 = h * HD
        a, bv = _rglru_gates(xc, gt, h, bg_ref, log_sig_lam)
        pa, pb = a[0:groups], bv[0:groups]
        a_in, b_in = [pa], [pb]
        for k in range(1, SUBLANES):
            ak = a[k * groups:(k + 1) * groups]
            pb = ak * pb + bv[k * groups:(k + 1) * groups]
            pa = ak * pa
            a_in.append(pa)
            b_in.append(pb)
        tot[h, 0] = pa
        tot[h, 1] = pb
        qa = tot[h, 0, pl.ds(0, supers, stride=SUBLANES), :]
        qb = tot[h, 1, pl.ds(0, supers, stride=SUBLANES), :]
        a_sup, b_sup = [qa], [qb]
        for j in range(1, SUBLANES):
            aj = tot[h, 0, pl.ds(j, supers, stride=SUBLANES), :]
            qb = aj * qb + tot[h, 1, pl.ds(j, supers, stride=SUBLANES), :]
            qa = aj * qa
            a_sup.append(qa)
            b_sup.append(qb)
        hc = hcarry[h]
        for s in range(supers):
            sup[h, pl.ds(s, 1), :] = hc
            hc = qa[s:s + 1] * hc + qb[s:s + 1]
        hcarry[h] = hc
        ho_ref[0, :, c0:c0 + HD] = hc
        h_sup = sup[h]
        hin[h, pl.ds(0, supers, stride=SUBLANES), :] = h_sup
        for j in range(1, SUBLANES):
            hin[h, pl.ds(j, supers, stride=SUBLANES), :] = a_sup[j - 1] * h_sup + b_sup[j - 1]
        h_grp = hin[h]
        for k in range(SUBLANES):
            hbuf[h, pl.ds(k, groups, stride=SUBLANES), :] = a_in[k] * h_grp + b_in[k]

    convs = {}
    for h in range(HEADS + 2):
        if h < HEADS:
            project(h)
        if 1 <= h <= HEADS:
            convs[h - 1] = conv(h - 1)
        if h >= 2:
            scan(h - 2, *convs.pop(h - 2))
    for h in range(HEADS):
        c0 = h * HD
        ybuf[:, c0:c0 + HD] = (hbuf[h] * gbuf[:, c0:c0 + HD]).astype(BF16)
    _project_and_norm(x_ref, ybuf, wo_ref, g_ref, b_ref, y_ref, alpha, NORM_SLAB)


def _rglru_in_proj(w_in):
    lead = w_in.shape[:-1]
    w = w_in.astype(BF16).reshape(*lead, 2, HEADS, HD)
    return jnp.swapaxes(w, -3, -2).reshape(*lead, 2 * HEADS * HD)


def _rglru_call(x, h0, cst_tm, wi, cw, cb, wg, bg, lam, wo, g, b, *, seq_len, alpha):
    T = x.shape[0]
    has_state = h0 is not None
    n_seq = T // seq_len
    operands = ([h0, cst_tm] if has_state else []) + [wi, cw, cb, wg, bg, lam, wo, g, b]
    specs_args = [_layer_operand(w) for w in operands]
    args = [x] + [a for _, a in specs_args]
    if has_state:
        tm, halo = T, (RG_CONV - 1) * (T // seq_len)
        h0_shape, cst_shape = _layer_shape(h0), (halo, D_MODEL)
        body = functools.partial(_rglru_state_kernel, tm=tm, step=T // seq_len, alpha=alpha)
        row = pl.BlockSpec((tm, D_MODEL), lambda i: (i, 0))
        out_shape = [jax.ShapeDtypeStruct((T, D_MODEL), F32),
                     jax.ShapeDtypeStruct(h0_shape, F32),
                     jax.ShapeDtypeStruct(cst_shape, F32)]
        out_specs = [row, pl.BlockSpec(h0_shape, lambda i: (0, 0)),
                     pl.BlockSpec(cst_shape, lambda i: (0, 0))]
        scratch = [pltpu.VMEM((halo + tm, D_MODEL), F32), pltpu.VMEM((tm, D_MODEL), BF16)]
    else:
        tm, halo, tiles_per_seq = TM_PROMPT, SUBLANES, seq_len // TM_PROMPT
        groups = tm // SUBLANES
        body = functools.partial(_rglru_prompt_kernel, tm=tm, tiles_per_seq=tiles_per_seq, alpha=alpha)
        row = pl.BlockSpec((tm, D_MODEL), lambda i: (i, 0))
        out_shape = [jax.ShapeDtypeStruct((T, D_MODEL), F32),
                     jax.ShapeDtypeStruct((n_seq, 1, D_MODEL), F32),
                     jax.ShapeDtypeStruct((n_seq, RG_CONV - 1, D_MODEL), F32)]
        out_specs = [row,
                     pl.BlockSpec((1, 1, D_MODEL), lambda i: (i // tiles_per_seq, 0, 0)),
                     pl.BlockSpec((1, RG_CONV - 1, D_MODEL), lambda i: (i // tiles_per_seq, 0, 0))]
        scratch = [pltpu.VMEM((HEADS, halo + tm, HD), F32),
                   pltpu.VMEM((tm, D_MODEL), F32),
                   pltpu.VMEM((HEADS, tm, HD), F32),
                   pltpu.VMEM((HEADS, 2, groups, HD), F32),
                   pltpu.VMEM((HEADS, groups, HD), F32),
                   pltpu.VMEM((HEADS, groups // SUBLANES, HD), F32),
                   pltpu.VMEM((tm, D_MODEL), BF16),
                   pltpu.VMEM((HEADS, halo, HD), F32),
                   pltpu.VMEM((HEADS, 1, HD), F32)]
    in_specs = [row] + [s for s, _ in specs_args]
    n_tiles = T // tm
    return pl.pallas_call(
        body,
        grid=(n_tiles,),
        in_specs=in_specs,
        out_specs=out_specs,
        out_shape=out_shape,
        scratch_shapes=scratch,
        compiler_params=_params(),
        name="rglru_state" if has_state else "rglru_prompt",
    )(*args)


def _split3(x):
    hi = x.astype(BF16)
    r = x - hi.astype(F32)
    mid = r.astype(BF16)
    lo = (r - mid.astype(F32)).astype(BF16)
    return hi, mid, lo


def _lower_bound(lower_ref, h, layer):
    rows = [lower_ref[n, h] for n in range(lower_ref.shape[0])]
    m = functools.reduce(jnp.maximum, rows)
    es = [jnp.exp(r - m) for r in rows]
    tot = functools.reduce(lambda p, q: p + q, es)
    sm = [e / tot for e in es]
    cs = functools.reduce(lambda p, q: p + q, sm[:layer + 1])
    return jnp.maximum(cs - sm[0], 0.0)


def _chunk_cumsum(cb_ref, tb_ref, eb_ref, tm, chunk):
    groups = tm // SUBLANES
    p = cb_ref[pl.ds(0, groups, stride=SUBLANES), :]
    for k in range(1, SUBLANES):
        p = p + cb_ref[pl.ds(k, groups, stride=SUBLANES), :]
        cb_ref[pl.ds(k, groups, stride=SUBLANES), :] = p
    if chunk > SUBLANES:
        gc = chunk // SUBLANES
        nch = tm // chunk
        tb_ref[...] = p
        e = jnp.zeros((nch, HD), F32)
        eb_ref[pl.ds(0, nch, stride=gc), :] = e
        for j in range(1, gc):
            e = e + tb_ref[pl.ds(j - 1, nch, stride=gc), :]
            eb_ref[pl.ds(j, nch, stride=gc), :] = e
        off = eb_ref[...]
        for k in range(SUBLANES):
            cb_ref[pl.ds(k, groups, stride=SUBLANES), :] = (
                cb_ref[pl.ds(k, groups, stride=SUBLANES), :] + off)
    return p


def _hgrn_front(part, lower_ref, h, layer, tm, chunk,
                cb_ref, tb_ref, eb_ref, kk_ref, v_ref, sg_ref, qs_ref, qe_ref):
    lb = _lower_bound(lower_ref, h, layer)
    fg = lb + (1.0 - lb) * _sigmoid(part(1))
    cb_ref[...] = jnp.log(jnp.maximum(fg, F_FLOOR))
    kk_ref[...] = 1.0 - fg
    q = part(0)
    qs = q * _sigmoid(q) * (HD ** -0.5)
    v_ref[...] = part(2)
    gg = part(3)
    sg_ref[...] = gg * _sigmoid(gg)
    totals = _chunk_cumsum(cb_ref, tb_ref, eb_ref, tm, chunk)
    cum = cb_ref[...]
    qs_ref[...] = qs
    qe_ref[...] = qs * jnp.exp(cum)
    safe = jnp.max(-cum) <= SAFE_DECAY
    return safe, totals


def _same_chunk_mask(chunk):
    ri = lax.broadcasted_iota(jnp.int32, (BLK, BLK), 0)
    ci = lax.broadcasted_iota(jnp.int32, (BLK, BLK), 1)
    lc = chunk.bit_length() - 1
    return ((ri >> lc) == (ci >> lc)) & (ci <= ri)


def _inverse_decayed_keys(kk, cum):
    return kk * jnp.exp(jnp.minimum(-cum, SAFE_DECAY))


def _robust_same_chunk_scores(rows, chunk, cb_ref, kk_ref, qs_ref):
    ri = lax.broadcasted_iota(jnp.int32, (BLK, BLK), 0)
    ci = lax.broadcasted_iota(jnp.int32, (BLK, BLK), 1)
    cum = cb_ref[rows, :]
    qs = qs_ref[rows, :]
    kk = kk_ref[rows, :]
    hi, mid, lo = _split3(cum)
    acc = jnp.where(ri == ci, _dot_nt(qs.astype(BF16), kk.astype(BF16)), 0.0)
    for lvl in range(chunk.bit_length() - 1):
        m = 1 << lvl
        pivot = ((ri >> (lvl + 1)) << (lvl + 1)) + (m - 1)
        sel = jnp.where(ci == pivot, 1.0, 0.0).astype(BF16)
        ref_cum = _dot(sel, hi) + _dot(sel, mid) + _dot(sel, lo)
        e = jnp.exp(-jnp.abs(cum - ref_cum))
        keep = (((ri >> (lvl + 1)) == (ci >> (lvl + 1)))
                & ((ri & (2 * m - 1)) >= m) & ((ci & (2 * m - 1)) < m))
        s = _dot_nt((qs * e).astype(BF16), (kk * e).astype(BF16))
        acc = acc + jnp.where(keep, s, 0.0)
    return acc


def _rms_gate(o, ng, sg):
    return o * lax.rsqrt(jnp.mean(o * o, axis=-1, keepdims=True) + RMS_EPS) * ng * sg


def _hgrn_prompt_kernel(x_ref, lower_ref, wi_ref, ng_ref, wo_ref, g_ref, b_ref, y_ref, so_ref,
                        xb_ref, p_ref, cb_ref, tb_ref, eb_ref, kk_ref, v_ref, sg_ref, qs_ref,
                        qe_ref, kd_ref, a_ref, oh_ref, ds_ref, ec_ref, st_ref, ob_ref, s_ref,
                        *, tm, layer, tiles_per_seq, alpha):
    chunk = PROMPT_CHUNK
    assert BLK == 2 * chunk
    n_blocks = tm // BLK
    i = pl.program_id(0)

    @pl.when(i % tiles_per_seq == 0)
    def _():
        s_ref[...] = jnp.zeros_like(s_ref)

    xb_ref[...] = x_ref[...].astype(BF16)
    second = lax.broadcasted_iota(jnp.int32, (BLK, HD), 0) >= chunk
    ri = lax.broadcasted_iota(jnp.int32, (BLK, BLK), 0)
    ci = lax.broadcasted_iota(jnp.int32, (BLK, BLK), 1)
    cross = (ri >= chunk) & (ci < chunk)
    same = _same_chunk_mask(chunk)

    def project(h, slot):
        p_ref[slot] = _dot(xb_ref[...], wi_ref[h])

    def views(slot):
        return tuple(r.at[slot] for r in (cb_ref, tb_ref, eb_ref, kk_ref, v_ref, sg_ref, qs_ref,
                                          qe_ref, kd_ref, a_ref, oh_ref, ds_ref, ec_ref))

    def scores(h, slot):
        cb, tb, eb, kk_s, v_s, sg_s, qs_s, qe_s, kd_s, a_s, _, ds_s, ec_s = views(slot)
        safe, _ = _hgrn_front(lambda k: p_ref[slot, :, k * HD:(k + 1) * HD], lower_ref, h, layer,
                              tm, chunk, cb, tb, eb, kk_s, v_s, sg_s, qs_s, qe_s)
        for nb in range(n_blocks):
            rows = pl.ds(nb * BLK, BLK)
            cum = cb[rows, :]
            kk = kk_s[rows, :]
            last0 = cb[pl.ds(nb * BLK + chunk - 1, 1), :]
            last1 = cb[pl.ds(nb * BLK + BLK - 1, 1), :]
            kd = kk * jnp.exp(jnp.where(second, last1, last0) - cum)
            kd_s[rows, :] = kd
            keys = jnp.concatenate([_inverse_decayed_keys(kk, cum), kd], axis=0).astype(BF16)
            s2 = _dot_nt(qe_s[rows, :].astype(BF16), keys)
            a = jnp.where(same, s2[:, 0:BLK], 0.0) + jnp.where(cross, s2[:, BLK:2 * BLK], 0.0)
            a_s[nb] = a.astype(BF16)
            k_blk = kd * jnp.where(second, 1.0, jnp.exp(last1))
            ds_s[nb] = _dot(k_blk.T.astype(BF16), v_s[rows, :].astype(BF16))
            ec_s[nb] = jnp.broadcast_to(jnp.exp(last0 + last1), (HD, HD)).T
        return safe

    def robust_scores(slot):
        cb, _, _, kk_s, _, _, qs_s, qe_s, kd_s, a_s, _, _, _ = views(slot)

        def fix(nb, _):
            rows = pl.ds(pl.multiple_of(nb * BLK, BLK), BLK)
            s1 = _dot_nt(qe_s[rows, :].astype(BF16), kd_s[rows, :].astype(BF16))
            a = _robust_same_chunk_scores(rows, chunk, cb, kk_s, qs_s)
            a_s[nb] = (a + jnp.where(cross, s1, 0.0)).astype(BF16)
            return 0
        lax.fori_loop(0, n_blocks, fix, 0)

    def advance_state(h, slot):
        ds_s, ec_s = views(slot)[11:13]
        state = s_ref[h]
        for nb in range(n_blocks):
            st_ref[slot, nb] = state
            state = ec_s[nb] * state + ds_s[nb]
        s_ref[h] = state

    def outputs(h, slot):
        cb, _, _, _, v_s, sg_s, _, qe_s, _, a_s, oh_s, _, _ = views(slot)
        for nb in range(n_blocks):
            rows = pl.ds(nb * BLK, BLK)
            last0 = cb[pl.ds(nb * BLK + chunk - 1, 1), :]
            q_blk = qe_s[rows, :] * jnp.where(second, jnp.exp(last0), 1.0)
            lhs = jnp.concatenate([a_s[nb], q_blk.astype(BF16)], axis=1)
            rhs = jnp.concatenate([v_s[rows, :].astype(BF16), st_ref[slot, nb].astype(BF16)], axis=0)
            oh_s[rows, :] = _dot(lhs, rhs)
        ob_ref[h] = _rms_gate(oh_s[...], ng_ref[...], sg_s[...]).astype(BF16)

    project(0, 0)

    def pair(h0, last):
        project(h0 + 1, 1)
        safe0 = scores(h0, 0)
        advance_state(h0, 0)
        outputs(h0, 0)
        if not last:
            project(h0 + 2, 0)
        safe1 = scores(h0 + 1, 1)
        advance_state(h0 + 1, 1)
        outputs(h0 + 1, 1)

        @pl.when(jnp.logical_not(jnp.logical_and(safe0, safe1)))
        def _():
            for slot in range(2):
                robust_scores(slot)
                outputs(h0 + slot, slot)
        return 0

    lax.fori_loop(0, HEADS // 2 - 1, lambda hh, _: pair(2 * hh, False), 0)
    pair(HEADS - 2, True)
    for h in range(HEADS):
        xb_ref[:, h * HD:(h + 1) * HD] = ob_ref[h]
    _project_and_norm(x_ref, xb_ref, wo_ref, g_ref, b_ref, y_ref, alpha, NORM_SLAB)

    @pl.when(i % tiles_per_seq == tiles_per_seq - 1)
    def _():
        so_ref[0] = s_ref[...]


def _hgrn_prep(hg_lower, wi, ng, wo):
    n_layers = hg_lower.shape[0]
    lower = hg_lower.reshape(n_layers, HEADS, 1, HD)
    wih = (wi.astype(BF16).reshape(n_layers, D_MODEL, 4, HEADS, HD).transpose(0, 3, 1, 2, 4)
           .reshape(n_layers, HEADS, D_MODEL, 4 * HD))
    return lower, wih, ng.reshape(n_layers, 1, HD), wo.astype(BF16)


def _hgrn_prompt_call(x, prep, layer, g, b, *, seq_len, alpha):
    lower, wih, ng, wo = prep
    T = x.shape[0]
    tm = TM_PROMPT
    tiles_per_seq = seq_len // tm
    n_seq = T // seq_len
    row = pl.BlockSpec((tm, D_MODEL), lambda i: (i, 0))
    specs_args = [_layer_operand(w) for w in (lower, (wih, layer), (ng, layer), (wo, layer), g, b)]
    head_buf = pltpu.VMEM((2, tm, HD), F32)
    scratch = [pltpu.VMEM((tm, D_MODEL), BF16),
               pltpu.VMEM((2, tm, 4 * HD), F32),
               head_buf,
               pltpu.VMEM((2, tm // SUBLANES, HD), F32),
               pltpu.VMEM((2, tm // SUBLANES, HD), F32),
               head_buf, head_buf, head_buf, head_buf, head_buf,
               head_buf,
               pltpu.VMEM((2, tm // BLK, BLK, BLK), BF16),
               head_buf,
               pltpu.VMEM((2, tm // BLK, HD, HD), F32),
               pltpu.VMEM((2, tm // BLK, HD, HD), F32),
               pltpu.VMEM((2, tm // BLK, HD, HD), F32),
               pltpu.VMEM((HEADS, tm, HD), BF16),
               pltpu.VMEM((HEADS, HD, HD), F32)]
    y, so = pl.pallas_call(
        functools.partial(_hgrn_prompt_kernel, tm=tm, layer=layer,
                          tiles_per_seq=tiles_per_seq, alpha=alpha),
        grid=(T // tm,),
        in_specs=[row] + [s for s, _ in specs_args],
        out_specs=[row, pl.BlockSpec((1, HEADS, HD, HD), lambda i: (i // tiles_per_seq, 0, 0, 0))],
        out_shape=[jax.ShapeDtypeStruct((T, D_MODEL), F32),
                   jax.ShapeDtypeStruct((n_seq, HEADS, HD, HD), F32)],
        scratch_shapes=scratch,
        compiler_params=_params(),
        name="hgrn_prompt",
    )(x, *[a for _, a in specs_args])
    return y, so


def _hgrn_sample_front_kernel(x_ref, lower_ref, wi_ref,
                              oi_ref, qe_ref, v_ref, sg_ref, kdt_ref, et_ref,
                              xb_ref, p_ref, cb_ref, kk_ref, qs_ref, a_ref, last_ref,
                              *, tm, chunk, layer):
    assert chunk == SUBLANES
    xb_ref[...] = x_ref[...].astype(BF16)
    groups = tm // SUBLANES
    n_blocks = tm // BLK
    same = _same_chunk_mask(chunk)

    def project(h, slot):
        proj = _dot(xb_ref[...], wi_ref[h])
        for t in range(chunk):
            for k in range(4):
                p_ref[slot, k, pl.ds(t, groups, stride=chunk), :] = (
                    proj[t * groups:(t + 1) * groups, k * HD:(k + 1) * HD])

    def head(h, slot):
        safe, totals = _hgrn_front(lambda k: p_ref[slot, k], lower_ref, h, layer, tm, chunk,
                                   cb_ref, None, None, kk_ref, v_ref.at[h], sg_ref.at[h],
                                   qs_ref, qe_ref.at[h])
        for k in range(SUBLANES):
            last_ref[pl.ds(k, groups, stride=SUBLANES), :] = totals
        for nb in range(n_blocks):
            r0 = nb * BLK
            rows = pl.ds(r0, BLK)
            cum = cb_ref[rows, :]
            kk = kk_ref[rows, :]
            s = _dot_nt(qe_ref[h, rows, :].astype(BF16), _inverse_decayed_keys(kk, cum).astype(BF16))
            a_ref[nb] = jnp.where(same, s, 0.0).astype(BF16)
            last = last_ref[rows, :]
            kdt_ref[h, :, r0:r0 + BLK] = (kk * jnp.exp(last - cum)).T
            et_ref[h, :, r0:r0 + BLK] = jnp.exp(last).T

        @pl.when(jnp.logical_not(safe))
        def _():
            def fix(nb, _):
                rows = pl.ds(pl.multiple_of(nb * BLK, BLK), BLK)
                a_ref[nb] = _robust_same_chunk_scores(rows, chunk, cb_ref, kk_ref, qs_ref).astype(BF16)
                return 0
            lax.fori_loop(0, n_blocks, fix, 0)

        for nb in range(n_blocks):
            rows = pl.ds(nb * BLK, BLK)
            oi_ref[h, rows, :] = _dot(a_ref[nb], v_ref[h, rows, :].astype(BF16))

    project(0, 0)

    def pair(h0, last):
        project(h0 + 1, 1)
        head(h0, 0)
        if not last:
            project(h0 + 2, 0)
        head(h0 + 1, 1)
        return 0

    lax.fori_loop(0, HEADS // 2 - 1, lambda hh, _: pair(2 * hh, False), 0)
    pair(HEADS - 2, True)


def _hgrn_sample_state_kernel(qe_ref, oi_ref, v_ref, sg_ref, kdt_ref, et_ref, s0_ref, ng_ref,
                              *rest, chunk):
    op_ref, so_ref = rest[-2:]
    row_seq = lax.broadcasted_iota(jnp.int32, (BLK, HD), 0) >> (chunk.bit_length() - 1)

    @pl.when(pl.program_id(0) > 0)
    def _():
        so_ref[...] = jnp.zeros_like(so_ref)

    def head(h, _):
        kdt = kdt_ref[h].astype(BF16)
        et = et_ref[h]
        v = v_ref[h]
        for sq in range(BLK // chunk):
            rows = slice(sq * chunk, (sq + 1) * chunk)
            s0 = s0_ref[sq, h]
            o = oi_ref[h, rows, :] + _dot(qe_ref[h, rows, :].astype(BF16), s0.astype(BF16))
            op_ref[h, rows, :] = _rms_gate(o, ng_ref[...], sg_ref[h, rows, :])
            vm = jnp.where(row_seq == sq, v, 0.0).astype(BF16)
            e_col = jnp.broadcast_to(et[:, sq * chunk:sq * chunk + 1], (HD, HD))
            so_ref[sq, h] = e_col * s0 + _dot(kdt, vm)
        return 0

    @pl.when(pl.program_id(0) == 0)
    def _():
        lax.fori_loop(0, HEADS, head, 0)


def _hgrn_sample_out_kernel(x_ref, op_ref, wo_ref, g_ref, b_ref, y_ref, ob_ref, *, chunk, alpha):
    batch = x_ref.shape[0] // chunk
    for h in range(HEADS):
        for t in range(chunk):
            ob_ref[t * batch:(t + 1) * batch, h * HD:(h + 1) * HD] = (
                op_ref[h, pl.ds(t, batch, stride=chunk), :].astype(BF16))
    _project_and_norm(x_ref, ob_ref, wo_ref, g_ref, b_ref, y_ref, alpha, NORM_SLAB)


def _hgrn_sample_call(x, s0, so_prev, prep, layer, g, b, *, seq_len, alpha):
    lower, wih, ng, wo = prep
    T = x.shape[0]
    chunk = seq_len
    heads_rows = jax.ShapeDtypeStruct((HEADS, T, HD), F32)
    heads_cols = jax.ShapeDtypeStruct((HEADS, HD, T), F32)
    full = lambda s: pl.BlockSpec(s, lambda i: (0,) * len(s))
    head_buf = pltpu.VMEM((T, HD), F32)
    specs_args = [_layer_operand(w) for w in (lower, (wih, layer))]
    oi, qe, v, sg, kdt, et = pl.pallas_call(
        functools.partial(_hgrn_sample_front_kernel, tm=T, chunk=chunk, layer=layer),
        grid=(1,),
        in_specs=[full(x.shape)] + [s for s, _ in specs_args],
        out_specs=[full(heads_rows.shape)] * 4 + [full(heads_cols.shape)] * 2,
        out_shape=[heads_rows] * 4 + [heads_cols] * 2,
        scratch_shapes=[pltpu.VMEM((T, D_MODEL), BF16), pltpu.VMEM((2, 4, T, HD), F32),
                        head_buf, head_buf, head_buf,
                        pltpu.VMEM((T // BLK, BLK, BLK), BF16), head_buf],
        compiler_params=_params(),
        name="hgrn_sample_front",
    )(x, *[a for _, a in specs_args])
    seqs = BLK // chunk
    steps = T // BLK
    n_layers = s0.shape[0]
    passes = n_layers if so_prev is None else 1
    blk = lambda p, i: jnp.where(p == 0, i, steps - 1)
    by_rows = pl.BlockSpec((HEADS, BLK, HD), lambda p, i: (0, blk(p, i), 0))
    by_cols = pl.BlockSpec((HEADS, HD, BLK), lambda p, i: (0, 0, blk(p, i)))
    state_in = pl.BlockSpec((None, seqs, HEADS, HD, HD), lambda p, i: (layer, blk(p, i), 0, 0, 0))
    state_out = pl.BlockSpec((None, seqs, HEADS, HD, HD),
                             lambda p, i: ((layer + p) % n_layers, i, 0, 0, 0))
    ng_spec, ng_arr = _layer_operand((ng, layer))
    in_specs = [by_rows, by_rows, by_rows, by_rows, by_cols, by_cols, state_in, ng_spec]
    args = [qe, oi, v, sg, kdt, et, s0, ng_arr]
    aliases = {}
    if so_prev is not None:
        in_specs.append(pl.BlockSpec(memory_space=pl.ANY))
        args.append(so_prev)
        aliases = {len(args) - 1: 1}
    op, so = pl.pallas_call(
        functools.partial(_hgrn_sample_state_kernel, chunk=chunk),
        grid=(passes, steps),
        in_specs=in_specs,
        out_specs=[by_rows, state_out],
        out_shape=[heads_rows, jax.ShapeDtypeStruct(s0.shape, F32)],
        input_output_aliases=aliases,
        compiler_params=_params(2),
        name="hgrn_sample_state",
    )(*args)
    specs_args = [_layer_operand(w) for w in ((wo, layer), g, b)]
    y = pl.pallas_call(
        functools.partial(_hgrn_sample_out_kernel, chunk=chunk, alpha=alpha),
        grid=(1,),
        in_specs=[full(x.shape), full(op.shape)] + [s for s, _ in specs_args],
        out_specs=full(x.shape),
        out_shape=jax.ShapeDtypeStruct(x.shape, F32),
        scratch_shapes=[pltpu.VMEM((T, D_MODEL), BF16)],
        compiler_params=_params(),
        name="hgrn_sample_out",
    )(x, op, *[a for _, a in specs_args])
    return y, so


def _to_time_major(a):
    return jnp.transpose(a, (1, 0, 2)).reshape(-1, a.shape[-1])


def _from_time_major(a, batch):
    return jnp.transpose(a.reshape(-1, batch, a.shape[-1]), (1, 0, 2))


def kernel(x_prompt, x_sample, state_rglru_h, state_rglru_conv, state_hgrn_s, state_ffn_conv,
           ln_g, ln_b, rg_w_in, rg_conv_w, rg_conv_b, rg_gate_w, rg_gate_b, rg_lambda, rg_w_out,
           hg_lower, hg_w_in, hg_norm_g, hg_w_out, ffn_w_in, ffn_conv_w, ffn_conv_b, ffn_w_out):
    depth = ln_g.shape[0]
    alpha = (2.0 * depth) ** 0.25
    pb, pl_len, _ = x_prompt.shape
    sb, sl_len, _ = x_sample.shape

    n_rg, n_ffn = rg_w_in.shape[0], ffn_w_in.shape[0]
    lng = ln_g.reshape(depth * 2, 1, D_MODEL)
    lnb = ln_b.reshape(depth * 2, 1, D_MODEL)
    rg_wi = _rglru_in_proj(rg_w_in)
    rg_wg = jnp.concatenate([rg_gate_w[:, 0], rg_gate_w[:, 1]], axis=-1).astype(BF16)
    rg_wo = rg_w_out.astype(BF16)
    rg_cb = rg_conv_b.reshape(n_rg, 1, D_MODEL)
    rg_lam = rg_lambda.reshape(n_rg, 1, D_MODEL)
    hg = _hgrn_prep(hg_lower, hg_w_in, hg_norm_g, hg_w_out)
    ffn_wi = ffn_w_in.astype(BF16)
    ffn_wo = ffn_w_out.astype(BF16)
    ffn_cb = ffn_conv_b.reshape(n_ffn, 1, D_FF)
    rg_cst = jnp.transpose(state_rglru_conv, (0, 2, 1, 3)).reshape(n_rg, -1, D_MODEL)
    ffn_cst = jnp.transpose(state_ffn_conv, (0, 2, 1, 3)).reshape(n_ffn, -1, D_FF)

    xp = x_prompt.reshape(pb * pl_len, D_MODEL)
    xs = _to_time_major(x_sample)
    p_h, p_rc, p_s, p_fc = [], [], [], []
    s_h, s_rc, s_fc = [], [], []
    s_s = None
    for i in range(depth):
        j = i // 2
        g0, b0 = (lng, 2 * i), (lnb, 2 * i)
        if i % 2 == 0:
            w = ((rg_wi, j), (rg_conv_w, j), (rg_cb, j), (rg_wg, j), (rg_gate_b, j), (rg_lam, j),
                 (rg_wo, j), g0, b0)
            xp, ho, co = _rglru_call(xp, None, None, *w, seq_len=pl_len, alpha=alpha)
            p_h.append(ho[:, 0])
            p_rc.append(co)
            xs, ho, co = _rglru_call(xs, (state_rglru_h, j), (rg_cst, j), *w,
                                     seq_len=sl_len, alpha=alpha)
            s_h.append(ho)
            s_rc.append(co)
        else:
            xp, so = _hgrn_prompt_call(xp, hg, j, g0, b0, seq_len=pl_len, alpha=alpha)
            p_s.append(so)
            xs, s_s = _hgrn_sample_call(xs, state_hgrn_s, s_s, hg, j, g0, b0,
                                        seq_len=sl_len, alpha=alpha)
        w = ((ffn_wi, i), (ffn_conv_w, i), (ffn_cb, i), (ffn_wo, i), (lng, 2 * i + 1), (lnb, 2 * i + 1))
        xp, fo = _ffn_call(xp, None, *w, seq_len=pl_len, alpha=alpha)
        p_fc.append(fo)
        xs, fo = _ffn_call(xs, (ffn_cst, i), *w, seq_len=sl_len, alpha=alpha)
        s_fc.append(fo)

    def stacked_from_time_major(parts):
        a = jnp.stack(parts)
        a = a.reshape(a.shape[0], -1, sb, a.shape[-1])
        return jnp.transpose(a, (0, 2, 1, 3))

    return (xp.reshape(x_prompt.shape), _from_time_major(xs, sb),
            jnp.stack(p_h), jnp.stack(p_rc), jnp.stack(p_s), jnp.stack(p_fc),
            jnp.stack(s_h), stacked_from_time_major(s_rc), s_s, stacked_from_time_major(s_fc))
```

```python
import functools

import jax
import jax.numpy as jnp
from jax import lax
from jax.experimental import pallas as pl
from jax.experimental.pallas import tpu as pltpu

F32 = jnp.float32
BF16 = jnp.bfloat16

D_MODEL = 1024
D_FF = 2816
HEADS = 8
HD = 128
RG_CONV = 4
FFN_CONV = 3
RG_C = 8.0
LN_EPS = 1e-5
RMS_EPS = 1e-6
F_FLOOR = 1e-30
SUBLANES = 8
BLK = 128
PROMPT_CHUNK = 64
SAFE_DECAY = 80.0
FF_TILE = 256
TM_PROMPT = 1024
TM_FFN = 1024
NORM_SLAB = 256
VMEM_V7X = 64 * 1024 * 1024
VMEM_LIMIT = VMEM_V7X - 8 * 1024 * 1024


def _dot(a, b):
    return jnp.dot(a, b, preferred_element_type=F32)


def _dot_nt(a, b):
    return lax.dot_general(a, b, (((1,), (1,)), ((), ())), preferred_element_type=F32)


def _sigmoid(x):
    return 0.5 * jnp.tanh(0.5 * x) + 0.5


def _gelu(x):
    c = (2.0 / jnp.pi) ** 0.5
    return x * (0.5 * jnp.tanh(x * (c + (c * 0.044715) * (x * x))) + 0.5)


def _sqrt_nonneg(y):
    return jnp.where(y > 0.0, y * lax.rsqrt(y), 0.0)


def _resident(shape):
    nd = len(shape)
    return pl.BlockSpec(shape, lambda *_: (0,) * nd, pipeline_mode=pl.Buffered(1))


def _layer_shape(w):
    return w[0].shape[1:] if isinstance(w, tuple) else w.shape


def _layer_operand(w):
    if not isinstance(w, tuple):
        return _resident(w.shape), w
    arr, layer = w
    tail = (0,) * (arr.ndim - 1)
    spec = pl.BlockSpec((None,) + arr.shape[1:], lambda *_: (layer,) + tail,
                        pipeline_mode=pl.Buffered(1))
    return spec, arr


def _deepnorm(x, f, g, b, alpha):
    z = alpha * x + f
    mu = jnp.mean(z, axis=-1, keepdims=True)
    zc = z - mu
    var = jnp.mean(zc * zc, axis=-1, keepdims=True)
    return zc * lax.rsqrt(var + LN_EPS) * g + b


def _project_and_norm(x_ref, lhs_ref, wo_ref, g_ref, b_ref, y_ref, alpha, slab):
    rows = x_ref.shape[0]
    slab = min(slab, rows)
    for r0 in range(0, rows, slab):
        sl = pl.ds(r0, slab)
        f = _dot(lhs_ref[sl, :], wo_ref[...])
        y_ref[sl, :] = _deepnorm(x_ref[sl, :], f, g_ref[...], b_ref[...], alpha)


def _params(n_axes=1):
    return pltpu.CompilerParams(dimension_semantics=("arbitrary",) * n_axes,
                                vmem_limit_bytes=VMEM_LIMIT)


def _ffn_kernel(*refs, tm, step, tiles_per_seq, has_state, alpha):
    if has_state:
        (x_ref, st_ref, wi_ref, cw_ref, cb_ref, wo_ref, g_ref, b_ref,
         y_ref, so_ref, gbuf, hbuf) = refs
        carry = None
    else:
        (x_ref, wi_ref, cw_ref, cb_ref, wo_ref, g_ref, b_ref,
         y_ref, so_ref, gbuf, hbuf, carry) = refs
    halo = gbuf.shape[0] - tm
    xb = x_ref[...].astype(BF16)
    if not has_state:
        @pl.when(pl.program_id(0) % tiles_per_seq == 0)
        def _():
            carry[...] = jnp.zeros_like(carry)
    for j in range(D_FF // FF_TILE):
        c0 = j * FF_TILE
        g = _dot(xb, wi_ref[:, c0:c0 + FF_TILE])
        u = _dot(xb, wi_ref[:, D_FF + c0:D_FF + c0 + FF_TILE])
        if has_state:
            gbuf[0:halo, :] = st_ref[:, c0:c0 + FF_TILE]
        else:
            gbuf[0:halo, :] = carry[:, c0:c0 + FF_TILE]
        gbuf[halo:halo + tm, :] = g
        g1 = gbuf[halo - step:halo - step + tm, :]
        g2 = gbuf[halo - 2 * step:halo - 2 * step + tm, :]
        cw = cw_ref[:, c0:c0 + FF_TILE]
        gc = g * cw[2:3] + g1 * cw[1:2] + g2 * cw[0:1] + cb_ref[:, c0:c0 + FF_TILE]
        hbuf[:, c0:c0 + FF_TILE] = (_gelu(gc) * u).astype(BF16)
        if has_state:
            so_ref[:, c0:c0 + FF_TILE] = gbuf[tm:tm + halo, :]
        else:
            carry[:, c0:c0 + FF_TILE] = gbuf[tm:tm + halo, :]
            so_ref[0, :, c0:c0 + FF_TILE] = gbuf[halo + tm - 2:halo + tm, :]
    _project_and_norm(x_ref, hbuf, wo_ref, g_ref, b_ref, y_ref, alpha, NORM_SLAB)


def _ffn_call(x, state_tm, wi, cw, cb, wo, g, b, *, seq_len, alpha):
    T = x.shape[0]
    has_state = state_tm is not None
    if has_state:
        step = T // seq_len
        tm, halo, tiles_per_seq = T, (FFN_CONV - 1) * step, 1
        st_shape = (halo, D_FF)
    else:
        tm, step, halo, tiles_per_seq = TM_FFN, 1, SUBLANES, seq_len // TM_FFN
    n_tiles = T // tm
    n_seq = T // seq_len
    row = pl.BlockSpec((tm, D_MODEL), lambda i: (i, 0))
    operands = ([state_tm] if has_state else []) + [wi, cw, cb, wo, g, b]
    specs_args = [_layer_operand(w) for w in operands]
    in_specs = [row] + [s for s, _ in specs_args]
    args = [x] + [a for _, a in specs_args]
    scratch = [pltpu.VMEM((halo + tm, FF_TILE), F32), pltpu.VMEM((tm, D_FF), BF16)]
    if has_state:
        so_shape = jax.ShapeDtypeStruct(st_shape, F32)
        so_spec = pl.BlockSpec(st_shape, lambda i: (0, 0))
    else:
        so_shape = jax.ShapeDtypeStruct((n_seq, FFN_CONV - 1, D_FF), F32)
        so_spec = pl.BlockSpec((1, FFN_CONV - 1, D_FF), lambda i: (i // tiles_per_seq, 0, 0))
        scratch.append(pltpu.VMEM((halo, D_FF), F32))
    return pl.pallas_call(
        functools.partial(_ffn_kernel, tm=tm, step=step, tiles_per_seq=tiles_per_seq,
                          has_state=has_state, alpha=alpha),
        grid=(n_tiles,),
        in_specs=in_specs,
        out_specs=[row, so_spec],
        out_shape=[jax.ShapeDtypeStruct((T, D_MODEL), F32), so_shape],
        scratch_shapes=scratch,
        compiler_params=_params(),
        name="ffn_state" if has_state else "ffn_prompt",
    )(*args)


def _rglru_gates(xc, gt, h, bg_ref, log_sig_lam):
    c0 = h * HD
    bg = bg_ref[:, c0:c0 + HD]
    ig = _sigmoid(gt[:, HD:2 * HD] + bg[1:2])
    half = (0.5 * RG_C) * log_sig_lam[:, c0:c0 + HD]
    a = jnp.exp(half * jnp.tanh(0.5 * (gt[:, 0:HD] + bg[0:1])) + half)
    return a, _sqrt_nonneg(jnp.maximum(1.0 - a * a, 0.0)) * ig * xc


def _rglru_state_kernel(x_ref, h0_ref, cst_ref, wi_ref, cw_ref, cb_ref, wg_ref, bg_ref, lam_ref,
                        wo_ref, g_ref, b_ref, y_ref, ho_ref, co_ref, xbuf, ybuf, *, tm, step, alpha):
    halo = xbuf.shape[0] - tm
    xb = x_ref[...].astype(BF16)
    xbuf[0:halo, :] = cst_ref[...]
    log_sig_lam = jax.nn.log_sigmoid(lam_ref[...])
    for h in range(HEADS):
        c0 = h * HD
        proj = _dot(xb, wi_ref[:, 2 * c0:2 * c0 + 2 * HD])
        gate = _gelu(proj[:, 0:HD])
        xbuf[halo:halo + tm, c0:c0 + HD] = proj[:, HD:2 * HD]
        cw = cw_ref[:, c0:c0 + HD]
        xc = cb_ref[:, c0:c0 + HD] + proj[:, HD:2 * HD] * cw[RG_CONV - 1:RG_CONV]
        for j in range(RG_CONV - 1):
            back = (RG_CONV - 1 - j) * step
            xc = xc + xbuf[halo - back:halo - back + tm, c0:c0 + HD] * cw[j:j + 1]
        a, bv = _rglru_gates(xc, _dot(xc.astype(BF16), wg_ref[h]), h, bg_ref, log_sig_lam)
        hh = h0_ref[:, c0:c0 + HD]
        for t in range(tm // step):
            rows = slice(t * step, (t + 1) * step)
            hh = a[rows] * hh + bv[rows]
            ybuf[rows, c0:c0 + HD] = (hh * gate[rows]).astype(BF16)
        ho_ref[:, c0:c0 + HD] = hh
    co_ref[...] = xbuf[tm:tm + halo, :]
    _project_and_norm(x_ref, ybuf, wo_ref, g_ref, b_ref, y_ref, alpha, NORM_SLAB)


def _rglru_prompt_kernel(x_ref, wi_ref, cw_ref, cb_ref, wg_ref, bg_ref, lam_ref, wo_ref, g_ref, b_ref,
                         y_ref, ho_ref, co_ref,
                         xbuf, gbuf, hbuf, tot, hin, sup, ybuf, ccarry, hcarry,
                         *, tm, tiles_per_seq, alpha):
    halo = xbuf.shape[1] - tm
    groups = tm // SUBLANES
    supers = groups // SUBLANES
    taps = RG_CONV - 1

    @pl.when(pl.program_id(0) % tiles_per_seq == 0)
    def _():
        ccarry[...] = jnp.zeros_like(ccarry)
        hcarry[...] = jnp.zeros_like(hcarry)

    xb = x_ref[...].astype(BF16)
    log_sig_lam = jax.nn.log_sigmoid(lam_ref[...])

    def project(h):
        c0 = h * HD
        proj = _dot(xb, wi_ref[:, 2 * c0:2 * c0 + 2 * HD])
        gbuf[:, c0:c0 + HD] = _gelu(proj[:, 0:HD])
        xs = xbuf.at[h]
        xs[0:halo, :] = ccarry[h]
        xs[halo:halo + tm, :] = proj[:, HD:2 * HD]
        ccarry[h] = xs[tm:tm + halo, :]
        co_ref[0, :, c0:c0 + HD] = xs[halo + tm - taps:halo + tm, :]

    def conv(h):
        c0 = h * HD
        xs = xbuf.at[h]
        cw = cw_ref[:, c0:c0 + HD]
        cb = cb_ref[:, c0:c0 + HD]
        lock = [xs[pl.ds(halo - taps + m, groups, stride=SUBLANES), :]
                for m in range(SUBLANES + taps)]
        xc = []
        for k in range(SUBLANES):
            acc = cb + lock[k + taps] * cw[taps:taps + 1]
            for j in range(taps):
                acc = acc + lock[k + j] * cw[j:j + 1]
            xc.append(acc)
        xc = jnp.concatenate(xc, axis=0)
        return xc, _dot(xc.astype(BF16), wg_ref[h])

    def scan(h, xc, gt):
        c0 = h * HD
        a, bv = _rglru_gates(xc, gt, h, bg_ref, log_sig_lam)
        pa, pb = a[0:groups], bv[0:groups]
        for k in range(1, SUBLANES):
            ak = a[k * groups:(k + 1) * groups]
            pb = ak * pb + bv[k * groups:(k + 1) * groups]
            pa = ak * pa
        tot[h, 0] = pa
        tot[h, 1] = pb
        qa = tot[h, 0, pl.ds(0, supers, stride=SUBLANES), :]
        qb = tot[h, 1, pl.ds(0, supers, stride=SUBLANES), :]
        a_sup, b_sup = [qa], [qb]
        for j in range(1, SUBLANES):
            aj = tot[h, 0, pl.ds(j, supers, stride=SUBLANES), :]
            qb = aj * qb + tot[h, 1, pl.ds(j, supers, stride=SUBLANES), :]
            qa = aj * qa
            a_sup.append(qa)
            b_sup.append(qb)
        hc = hcarry[h]
        for s in range(supers):
            sup[h, pl.ds(s, 1), :] = hc
            hc = qa[s:s + 1] * hc + qb[s:s + 1]
        hcarry[h] = hc
        ho_ref[0, :, c0:c0 + HD] = hc
        h_sup = sup[h]
        hin[h, pl.ds(0, supers, stride=SUBLANES), :] = h_sup
        for j in range(1, SUBLANES):
            hin[h, pl.ds(j, supers, stride=SUBLANES), :] = a_sup[j - 1] * h_sup + b_sup[j - 1]
        hk = hin[h]
        for k in range(SUBLANES):
            hk = a[k * groups:(k + 1) * groups] * hk + bv[k * groups:(k + 1) * groups]
            hbuf[h, pl.ds(k, groups, stride=SUBLANES), :] = hk

    convs = {}
    for h in range(HEADS + 2):
        if h < HEADS:
            project(h)
        if 1 <= h <= HEADS:
            convs[h - 1] = conv(h - 1)
        if h >= 2:
            scan(h - 2, *convs.pop(h - 2))
    for h in range(HEADS):
        c0 = h * HD
        ybuf[:, c0:c0 + HD] = (hbuf[h] * gbuf[:, c0:c0 + HD]).astype(BF16)
    _project_and_norm(x_ref, ybuf, wo_ref, g_ref, b_ref, y_ref, alpha, NORM_SLAB)


def _rglru_in_proj(w_in):
    lead = w_in.shape[:-1]
    w = w_in.astype(BF16).reshape(*lead, 2, HEADS, HD)
    return jnp.swapaxes(w, -3, -2).reshape(*lead, 2 * HEADS * HD)


def _rglru_call(x, h0, cst_tm, wi, cw, cb, wg, bg, lam, wo, g, b, *, seq_len, alpha):
    T = x.shape[0]
    has_state = h0 is not None
    n_seq = T // seq_len
    operands = ([h0, cst_tm] if has_state else []) + [wi, cw, cb, wg, bg, lam, wo, g, b]
    specs_args = [_layer_operand(w) for w in operands]
    args = [x] + [a for _, a in specs_args]
    if has_state:
        tm, halo = T, (RG_CONV - 1) * (T // seq_len)
        h0_shape, cst_shape = _layer_shape(h0), (halo, D_MODEL)
        body = functools.partial(_rglru_state_kernel, tm=tm, step=T // seq_len, alpha=alpha)
        row = pl.BlockSpec((tm, D_MODEL), lambda i: (i, 0))
        out_shape = [jax.ShapeDtypeStruct((T, D_MODEL), F32),
                     jax.ShapeDtypeStruct(h0_shape, F32),
                     jax.ShapeDtypeStruct(cst_shape, F32)]
        out_specs = [row, pl.BlockSpec(h0_shape, lambda i: (0, 0)),
                     pl.BlockSpec(cst_shape, lambda i: (0, 0))]
        scratch = [pltpu.VMEM((halo + tm, D_MODEL), F32), pltpu.VMEM((tm, D_MODEL), BF16)]
    else:
        tm, halo, tiles_per_seq = TM_PROMPT, SUBLANES, seq_len // TM_PROMPT
        groups = tm // SUBLANES
        body = functools.partial(_rglru_prompt_kernel, tm=tm, tiles_per_seq=tiles_per_seq, alpha=alpha)
        row = pl.BlockSpec((tm, D_MODEL), lambda i: (i, 0))
        out_shape = [jax.ShapeDtypeStruct((T, D_MODEL), F32),
                     jax.ShapeDtypeStruct((n_seq, 1, D_MODEL), F32),
                     jax.ShapeDtypeStruct((n_seq, RG_CONV - 1, D_MODEL), F32)]
        out_specs = [row,
                     pl.BlockSpec((1, 1, D_MODEL), lambda i: (i // tiles_per_seq, 0, 0)),
                     pl.BlockSpec((1, RG_CONV - 1, D_MODEL), lambda i: (i // tiles_per_seq, 0, 0))]
        scratch = [pltpu.VMEM((HEADS, halo + tm, HD), F32),
                   pltpu.VMEM((tm, D_MODEL), F32),
                   pltpu.VMEM((HEADS, tm, HD), F32),
                   pltpu.VMEM((HEADS, 2, groups, HD), F32),
                   pltpu.VMEM((HEADS, groups, HD), F32),
                   pltpu.VMEM((HEADS, groups // SUBLANES, HD), F32),
                   pltpu.VMEM((tm, D_MODEL), BF16),
                   pltpu.VMEM((HEADS, halo, HD), F32),
                   pltpu.VMEM((HEADS, 1, HD), F32)]
    in_specs = [row] + [s for s, _ in specs_args]
    n_tiles = T // tm
    return pl.pallas_call(
        body,
        grid=(n_tiles,),
        in_specs=in_specs,
        out_specs=out_specs,
        out_shape=out_shape,
        scratch_shapes=scratch,
        compiler_params=_params(),
        name="rglru_state" if has_state else "rglru_prompt",
    )(*args)


def _split3(x):
    hi = x.astype(BF16)
    r = x - hi.astype(F32)
    mid = r.astype(BF16)
    lo = (r - mid.astype(F32)).astype(BF16)
    return hi, mid, lo


def _lower_bound(lower_ref, h, layer):
    rows = [lower_ref[n, h] for n in range(lower_ref.shape[0])]
    m = functools.reduce(jnp.maximum, rows)
    es = [jnp.exp(r - m) for r in rows]
    tot = functools.reduce(lambda p, q: p + q, es)
    sm = [e / tot for e in es]
    cs = functools.reduce(lambda p, q: p + q, sm[:layer + 1])
    return jnp.maximum(cs - sm[0], 0.0)


def _chunk_cumsum(cb_ref, tb_ref, eb_ref, tm, chunk):
    groups = tm // SUBLANES
    p = cb_ref[pl.ds(0, groups, stride=SUBLANES), :]
    for k in range(1, SUBLANES):
        p = p + cb_ref[pl.ds(k, groups, stride=SUBLANES), :]
        cb_ref[pl.ds(k, groups, stride=SUBLANES), :] = p
    if chunk > SUBLANES:
        gc = chunk // SUBLANES
        nch = tm // chunk
        tb_ref[...] = p
        e = jnp.zeros((nch, HD), F32)
        eb_ref[pl.ds(0, nch, stride=gc), :] = e
        for j in range(1, gc):
            e = e + tb_ref[pl.ds(j - 1, nch, stride=gc), :]
            eb_ref[pl.ds(j, nch, stride=gc), :] = e
        off = eb_ref[...]
        for k in range(SUBLANES):
            cb_ref[pl.ds(k, groups, stride=SUBLANES), :] = (
                cb_ref[pl.ds(k, groups, stride=SUBLANES), :] + off)
    return p


def _hgrn_front(part, lower_ref, h, layer, tm, chunk,
                cb_ref, tb_ref, eb_ref, kk_ref, v_ref, sg_ref, qs_ref, qe_ref):
    lb = _lower_bound(lower_ref, h, layer)
    fg = lb + (1.0 - lb) * _sigmoid(part(1))
    cb_ref[...] = jnp.log(jnp.maximum(fg, F_FLOOR))
    kk_ref[...] = 1.0 - fg
    q = part(0)
    qs = q * _sigmoid(q) * (HD ** -0.5)
    v_ref[...] = part(2)
    gg = part(3)
    sg_ref[...] = gg * _sigmoid(gg)
    totals = _chunk_cumsum(cb_ref, tb_ref, eb_ref, tm, chunk)
    cum = cb_ref[...]
    qs_ref[...] = qs
    qe_ref[...] = qs * jnp.exp(cum)
    safe = jnp.max(-cum) <= SAFE_DECAY
    return safe, totals


def _same_chunk_mask(chunk):
    ri = lax.broadcasted_iota(jnp.int32, (BLK, BLK), 0)
    ci = lax.broadcasted_iota(jnp.int32, (BLK, BLK), 1)
    lc = chunk.bit_length() - 1
    return ((ri >> lc) == (ci >> lc)) & (ci <= ri)


def _inverse_decayed_keys(kk, cum):
    return kk * jnp.exp(jnp.minimum(-cum, SAFE_DECAY))


def _robust_same_chunk_scores(rows, chunk, cb_ref, kk_ref, qs_ref):
    ri = lax.broadcasted_iota(jnp.int32, (BLK, BLK), 0)
    ci = lax.broadcasted_iota(jnp.int32, (BLK, BLK), 1)
    cum = cb_ref[rows, :]
    qs = qs_ref[rows, :]
    kk = kk_ref[rows, :]
    hi, mid, lo = _split3(cum)
    acc = jnp.where(ri == ci, _dot_nt(qs.astype(BF16), kk.astype(BF16)), 0.0)
    for lvl in range(chunk.bit_length() - 1):
        m = 1 << lvl
        pivot = ((ri >> (lvl + 1)) << (lvl + 1)) + (m - 1)
        sel = jnp.where(ci == pivot, 1.0, 0.0).astype(BF16)
        ref_cum = _dot(sel, hi) + _dot(sel, mid) + _dot(sel, lo)
        e = jnp.exp(-jnp.abs(cum - ref_cum))
        keep = (((ri >> (lvl + 1)) == (ci >> (lvl + 1)))
                & ((ri & (2 * m - 1)) >= m) & ((ci & (2 * m - 1)) < m))
        s = _dot_nt((qs * e).astype(BF16), (kk * e).astype(BF16))
        acc = acc + jnp.where(keep, s, 0.0)
    return acc


def _rms_gate(o, ng, sg):
    return o * lax.rsqrt(jnp.mean(o * o, axis=-1, keepdims=True) + RMS_EPS) * ng * sg


def _hgrn_prompt_kernel(x_ref, lower_ref, wi_ref, ng_ref, wo_ref, g_ref, b_ref, y_ref, so_ref,
                        xb_ref, p_ref, cb_ref, tb_ref, eb_ref, kk_ref, v_ref, sg_ref, qs_ref,
                        qe_ref, kd_ref, a_ref, oh_ref, ds_ref, ec_ref, st_ref, ob_ref, s_ref,
                        *, tm, layer, tiles_per_seq, alpha):
    chunk = PROMPT_CHUNK
    assert BLK == 2 * chunk
    n_blocks = tm // BLK
    i = pl.program_id(0)

    @pl.when(i % tiles_per_seq == 0)
    def _():
        s_ref[...] = jnp.zeros_like(s_ref)

    xb_ref[...] = x_ref[...].astype(BF16)
    second = lax.broadcasted_iota(jnp.int32, (BLK, HD), 0) >= chunk
    ri = lax.broadcasted_iota(jnp.int32, (BLK, BLK), 0)
    ci = lax.broadcasted_iota(jnp.int32, (BLK, BLK), 1)
    cross = (ri >= chunk) & (ci < chunk)
    same = _same_chunk_mask(chunk)

    def project(h, slot):
        p_ref[slot] = _dot(xb_ref[...], wi_ref[h])

    def views(slot):
        return tuple(r.at[slot] for r in (cb_ref, tb_ref, eb_ref, kk_ref, v_ref, sg_ref, qs_ref,
                                          qe_ref, kd_ref, a_ref, oh_ref, ds_ref, ec_ref))

    def scores(h, slot):
        cb, tb, eb, kk_s, v_s, sg_s, qs_s, qe_s, kd_s, a_s, _, ds_s, ec_s = views(slot)
        safe, _ = _hgrn_front(lambda k: p_ref[slot, :, k * HD:(k + 1) * HD], lower_ref, h, layer,
                              tm, chunk, cb, tb, eb, kk_s, v_s, sg_s, qs_s, qe_s)
        for nb in range(n_blocks):
            rows = pl.ds(nb * BLK, BLK)
            cum = cb[rows, :]
            kk = kk_s[rows, :]
            last0 = cb[pl.ds(nb * BLK + chunk - 1, 1), :]
            last1 = cb[pl.ds(nb * BLK + BLK - 1, 1), :]
            kd = kk * jnp.exp(jnp.where(second, last1, last0) - cum)
            kd_s[rows, :] = kd
            keys = jnp.concatenate([_inverse_decayed_keys(kk, cum), kd], axis=0).astype(BF16)
            s2 = _dot_nt(qe_s[rows, :].astype(BF16), keys)
            a = jnp.where(same, s2[:, 0:BLK], 0.0) + jnp.where(cross, s2[:, BLK:2 * BLK], 0.0)
            a_s[nb] = a.astype(BF16)
            k_blk = kd * jnp.where(second, 1.0, jnp.exp(last1))
            ds_s[nb] = lax.dot_general(k_blk.astype(BF16), v_s[rows, :].astype(BF16),
                                       (((0,), (0,)), ((), ())), preferred_element_type=F32)
            ec_s[nb] = jnp.broadcast_to(jnp.exp(last0 + last1), (HD, HD)).T
        return safe

    def robust_scores(slot):
        cb, _, _, kk_s, _, _, qs_s, qe_s, kd_s, a_s, _, _, _ = views(slot)

        def fix(nb, _):
            rows = pl.ds(pl.multiple_of(nb * BLK, BLK), BLK)
            s1 = _dot_nt(qe_s[rows, :].astype(BF16), kd_s[rows, :].astype(BF16))
            a = _robust_same_chunk_scores(rows, chunk, cb, kk_s, qs_s)
            a_s[nb] = (a + jnp.where(cross, s1, 0.0)).astype(BF16)
            return 0
        lax.fori_loop(0, n_blocks, fix, 0)

    def advance_state(h, slot):
        ds_s, ec_s = views(slot)[11:13]
        state = s_ref[h]
        for nb in range(n_blocks):
            st_ref[slot, nb] = state
            state = ec_s[nb] * state + ds_s[nb]
        s_ref[h] = state

    def outputs(h, slot):
        cb, _, _, _, v_s, sg_s, _, qe_s, _, a_s, oh_s, _, _ = views(slot)
        for nb in range(n_blocks):
            rows = pl.ds(nb * BLK, BLK)
            last0 = cb[pl.ds(nb * BLK + chunk - 1, 1), :]
            q_blk = qe_s[rows, :] * jnp.where(second, jnp.exp(last0), 1.0)
            lhs = jnp.concatenate([a_s[nb], q_blk.astype(BF16)], axis=1)
            rhs = jnp.concatenate([v_s[rows, :].astype(BF16), st_ref[slot, nb].astype(BF16)], axis=0)
            oh_s[rows, :] = _dot(lhs, rhs)
        ob_ref[h] = _rms_gate(oh_s[...], ng_ref[...], sg_s[...]).astype(BF16)

    project(0, 0)

    def pair(h0, last):
        project(h0 + 1, 1)
        safe0 = scores(h0, 0)
        advance_state(h0, 0)
        outputs(h0, 0)
        if not last:
            project(h0 + 2, 0)
        safe1 = scores(h0 + 1, 1)
        advance_state(h0 + 1, 1)
        outputs(h0 + 1, 1)

        @pl.when(jnp.logical_not(jnp.logical_and(safe0, safe1)))
        def _():
            for slot in range(2):
                robust_scores(slot)
                outputs(h0 + slot, slot)
        return 0

    lax.fori_loop(0, HEADS // 2 - 1, lambda hh, _: pair(2 * hh, False), 0)
    pair(HEADS - 2, True)
    for h in range(HEADS):
        xb_ref[:, h * HD:(h + 1) * HD] = ob_ref[h]
    _project_and_norm(x_ref, xb_ref, wo_ref, g_ref, b_ref, y_ref, alpha, NORM_SLAB)

    @pl.when(i % tiles_per_seq == tiles_per_seq - 1)
    def _():
        so_ref[0] = s_ref[...]


def _hgrn_prep(hg_lower, wi, ng, wo):
    n_layers = hg_lower.shape[0]
    lower = hg_lower.reshape(n_layers, HEADS, 1, HD)
    wih = (wi.astype(BF16).reshape(n_layers, D_MODEL, 4, HEADS, HD).transpose(0, 3, 1, 2, 4)
           .reshape(n_layers, HEADS, D_MODEL, 4 * HD))
    return lower, wih, ng.reshape(n_layers, 1, HD), wo.astype(BF16)


def _hgrn_prompt_call(x, prep, layer, g, b, *, seq_len, alpha):
    lower, wih, ng, wo = prep
    T = x.shape[0]
    tm = TM_PROMPT
    tiles_per_seq = seq_len // tm
    n_seq = T // seq_len
    row = pl.BlockSpec((tm, D_MODEL), lambda i: (i, 0))
    specs_args = [_layer_operand(w) for w in (lower, (wih, layer), (ng, layer), (wo, layer), g, b)]
    head_buf = pltpu.VMEM((2, tm, HD), F32)
    scratch = [pltpu.VMEM((tm, D_MODEL), BF16),
               pltpu.VMEM((2, tm, 4 * HD), F32),
               head_buf,
               pltpu.VMEM((2, tm // SUBLANES, HD), F32),
               pltpu.VMEM((2, tm // SUBLANES, HD), F32),
               head_buf, head_buf, head_buf, head_buf, head_buf,
               head_buf,
               pltpu.VMEM((2, tm // BLK, BLK, BLK), BF16),
               head_buf,
               pltpu.VMEM((2, tm // BLK, HD, HD), F32),
               pltpu.VMEM((2, tm // BLK, HD, HD), F32),
               pltpu.VMEM((2, tm // BLK, HD, HD), F32),
               pltpu.VMEM((HEADS, tm, HD), BF16),
               pltpu.VMEM((HEADS, HD, HD), F32)]
    y, so = pl.pallas_call(
        functools.partial(_hgrn_prompt_kernel, tm=tm, layer=layer,
                          tiles_per_seq=tiles_per_seq, alpha=alpha),
        grid=(T // tm,),
        in_specs=[row] + [s for s, _ in specs_args],
        out_specs=[row, pl.BlockSpec((1, HEADS, HD, HD), lambda i: (i // tiles_per_seq, 0, 0, 0))],
        out_shape=[jax.ShapeDtypeStruct((T, D_MODEL), F32),
                   jax.ShapeDtypeStruct((n_seq, HEADS, HD, HD), F32)],
        scratch_shapes=scratch,
        compiler_params=_params(),
        name="hgrn_prompt",
    )(x, *[a for _, a in specs_args])
    return y, so


def _hgrn_sample_front_kernel(x_ref, lower_ref, wi_ref,
                              oi_ref, qe_ref, v_ref, sg_ref, kdt_ref, et_ref,
                              xb_ref, p_ref, cb_ref, kk_ref, qs_ref, a_ref, last_ref,
                              *, tm, chunk, layer):
    assert chunk == SUBLANES
    xb_ref[...] = x_ref[...].astype(BF16)
    groups = tm // SUBLANES
    n_blocks = tm // BLK
    same = _same_chunk_mask(chunk)

    def project(h, slot):
        proj = _dot(xb_ref[...], wi_ref[h])
        for t in range(chunk):
            for k in range(4):
                p_ref[slot, k, pl.ds(t, groups, stride=chunk), :] = (
                    proj[t * groups:(t + 1) * groups, k * HD:(k + 1) * HD])

    def head(h, slot):
        safe, totals = _hgrn_front(lambda k: p_ref[slot, k], lower_ref, h, layer, tm, chunk,
                                   cb_ref, None, None, kk_ref, v_ref.at[h], sg_ref.at[h],
                                   qs_ref, qe_ref.at[h])
        for k in range(SUBLANES):
            last_ref[pl.ds(k, groups, stride=SUBLANES), :] = totals
        for nb in range(n_blocks):
            r0 = nb * BLK
            rows = pl.ds(r0, BLK)
            cum = cb_ref[rows, :]
            kk = kk_ref[rows, :]
            s = _dot_nt(qe_ref[h, rows, :].astype(BF16), _inverse_decayed_keys(kk, cum).astype(BF16))
            a_ref[nb] = jnp.where(same, s, 0.0).astype(BF16)
            last = last_ref[rows, :]
            kdt_ref[h, :, r0:r0 + BLK] = (kk * jnp.exp(last - cum)).T
            et_ref[h, :, r0:r0 + BLK] = jnp.exp(last).T

        @pl.when(jnp.logical_not(safe))
        def _():
            def fix(nb, _):
                rows = pl.ds(pl.multiple_of(nb * BLK, BLK), BLK)
                a_ref[nb] = _robust_same_chunk_scores(rows, chunk, cb_ref, kk_ref, qs_ref).astype(BF16)
                return 0
            lax.fori_loop(0, n_blocks, fix, 0)

        for nb in range(n_blocks):
            rows = pl.ds(nb * BLK, BLK)
            oi_ref[h, rows, :] = _dot(a_ref[nb], v_ref[h, rows, :].astype(BF16))

    project(0, 0)

    def pair(h0, last):
        project(h0 + 1, 1)
        head(h0, 0)
        if not last:
            project(h0 + 2, 0)
        head(h0 + 1, 1)
        return 0

    lax.fori_loop(0, HEADS // 2 - 1, lambda hh, _: pair(2 * hh, False), 0)
    pair(HEADS - 2, True)


def _hgrn_sample_state_kernel(qe_ref, oi_ref, v_ref, sg_ref, kdt_ref, et_ref, s0_ref, ng_ref,
                              *rest, chunk):
    op_ref, so_ref = rest[-2:]
    row_seq = lax.broadcasted_iota(jnp.int32, (BLK, HD), 0) >> (chunk.bit_length() - 1)

    @pl.when(pl.program_id(0) > 0)
    def _():
        so_ref[...] = jnp.zeros_like(so_ref)

    def head(h, _):
        kdt = kdt_ref[h].astype(BF16)
        et = et_ref[h]
        v = v_ref[h]
        for sq in range(BLK // chunk):
            rows = slice(sq * chunk, (sq + 1) * chunk)
            s0 = s0_ref[sq, h]
            o = oi_ref[h, rows, :] + _dot(qe_ref[h, rows, :].astype(BF16), s0.astype(BF16))
            op_ref[h, rows, :] = _rms_gate(o, ng_ref[...], sg_ref[h, rows, :])
            vm = jnp.where(row_seq == sq, v, 0.0).astype(BF16)
            e_col = jnp.broadcast_to(et[:, sq * chunk:sq * chunk + 1], (HD, HD))
            so_ref[sq, h] = e_col * s0 + _dot(kdt, vm)
        return 0

    @pl.when(pl.program_id(0) == 0)
    def _():
        lax.fori_loop(0, HEADS, head, 0)


def _hgrn_sample_out_kernel(x_ref, op_ref, wo_ref, g_ref, b_ref, y_ref, ob_ref, *, chunk, alpha):
    batch = x_ref.shape[0] // chunk
    for h in range(HEADS):
        for t in range(chunk):
            ob_ref[t * batch:(t + 1) * batch, h * HD:(h + 1) * HD] = (
                op_ref[h, pl.ds(t, batch, stride=chunk), :].astype(BF16))
    _project_and_norm(x_ref, ob_ref, wo_ref, g_ref, b_ref, y_ref, alpha, NORM_SLAB)


def _hgrn_sample_call(x, s0, so_prev, prep, layer, g, b, *, seq_len, alpha):
    lower, wih, ng, wo = prep
    T = x.shape[0]
    chunk = seq_len
    heads_rows = jax.ShapeDtypeStruct((HEADS, T, HD), F32)
    heads_cols = jax.ShapeDtypeStruct((HEADS, HD, T), F32)
    full = lambda s: pl.BlockSpec(s, lambda i: (0,) * len(s))
    head_buf = pltpu.VMEM((T, HD), F32)
    specs_args = [_layer_operand(w) for w in (lower, (wih, layer))]
    oi, qe, v, sg, kdt, et = pl.pallas_call(
        functools.partial(_hgrn_sample_front_kernel, tm=T, chunk=chunk, layer=layer),
        grid=(1,),
        in_specs=[full(x.shape)] + [s for s, _ in specs_args],
        out_specs=[full(heads_rows.shape)] * 4 + [full(heads_cols.shape)] * 2,
        out_shape=[heads_rows] * 4 + [heads_cols] * 2,
        scratch_shapes=[pltpu.VMEM((T, D_MODEL), BF16), pltpu.VMEM((2, 4, T, HD), F32),
                        head_buf, head_buf, head_buf,
                        pltpu.VMEM((T // BLK, BLK, BLK), BF16), head_buf],
        compiler_params=_params(),
        name="hgrn_sample_front",
    )(x, *[a for _, a in specs_args])
    seqs = BLK // chunk
    steps = T // BLK
    n_layers = s0.shape[0]
    passes = n_layers if so_prev is None else 1
    blk = lambda p, i: jnp.where(p == 0, i, steps - 1)
    by_rows = pl.BlockSpec((HEADS, BLK, HD), lambda p, i: (0, blk(p, i), 0))
    by_cols = pl.BlockSpec((HEADS, HD, BLK), lambda p, i: (0, 0, blk(p, i)))
    state_in = pl.BlockSpec((None, seqs, HEADS, HD, HD), lambda p, i: (layer, blk(p, i), 0, 0, 0))
    state_out = pl.BlockSpec((None, seqs, HEADS, HD, HD),
                             lambda p, i: ((layer + p) % n_layers, i, 0, 0, 0))
    ng_spec, ng_arr = _layer_operand((ng, layer))
    in_specs = [by_rows, by_rows, by_rows, by_rows, by_cols, by_cols, state_in, ng_spec]
    args = [qe, oi, v, sg, kdt, et, s0, ng_arr]
    aliases = {}
    if so_prev is not None:
        in_specs.append(pl.BlockSpec(memory_space=pl.ANY))
        args.append(so_prev)
        aliases = {len(args) - 1: 1}
    op, so = pl.pallas_call(
        functools.partial(_hgrn_sample_state_kernel, chunk=chunk),
        grid=(passes, steps),
        in_specs=in_specs,
        out_specs=[by_rows, state_out],
        out_shape=[heads_rows, jax.ShapeDtypeStruct(s0.shape, F32)],
        input_output_aliases=aliases,
        compiler_params=_params(2),
        name="hgrn_sample_state",
    )(*args)
    specs_args = [_layer_operand(w) for w in ((wo, layer), g, b)]
    y = pl.pallas_call(
        functools.partial(_hgrn_sample_out_kernel, chunk=chunk, alpha=alpha),
        grid=(1,),
        in_specs=[full(x.shape), full(op.shape)] + [s for s, _ in specs_args],
        out_specs=full(x.shape),
        out_shape=jax.ShapeDtypeStruct(x.shape, F32),
        scratch_shapes=[pltpu.VMEM((T, D_MODEL), BF16)],
        compiler_params=_params(),
        name="hgrn_sample_out",
    )(x, op, *[a for _, a in specs_args])
    return y, so


def _to_time_major(a):
    return jnp.transpose(a, (1, 0, 2)).reshape(-1, a.shape[-1])


def _from_time_major(a, batch):
    return jnp.transpose(a.reshape(-1, batch, a.shape[-1]), (1, 0, 2))


def kernel(x_prompt, x_sample, state_rglru_h, state_rglru_conv, state_hgrn_s, state_ffn_conv,
           ln_g, ln_b, rg_w_in, rg_conv_w, rg_conv_b, rg_gate_w, rg_gate_b, rg_lambda, rg_w_out,
           hg_lower, hg_w_in, hg_norm_g, hg_w_out, ffn_w_in, ffn_conv_w, ffn_conv_b, ffn_w_out):
    depth = ln_g.shape[0]
    alpha = (2.0 * depth) ** 0.25
    pb, pl_len, _ = x_prompt.shape
    sb, sl_len, _ = x_sample.shape

    n_rg, n_ffn = rg_w_in.shape[0], ffn_w_in.shape[0]
    lng = ln_g.reshape(depth * 2, 1, D_MODEL)
    lnb = ln_b.reshape(depth * 2, 1, D_MODEL)
    rg_wi = _rglru_in_proj(rg_w_in)
    rg_wg = jnp.concatenate([rg_gate_w[:, 0], rg_gate_w[:, 1]], axis=-1).astype(BF16)
    rg_wo = rg_w_out.astype(BF16)
    rg_cb = rg_conv_b.reshape(n_rg, 1, D_MODEL)
    rg_lam = rg_lambda.reshape(n_rg, 1, D_MODEL)
    hg = _hgrn_prep(hg_lower, hg_w_in, hg_norm_g, hg_w_out)
    ffn_wi = ffn_w_in.astype(BF16)
    ffn_wo = ffn_w_out.astype(BF16)
    ffn_cb = ffn_conv_b.reshape(n_ffn, 1, D_FF)
    rg_cst = jnp.transpose(state_rglru_conv, (0, 2, 1, 3)).reshape(n_rg, -1, D_MODEL)
    ffn_cst = jnp.transpose(state_ffn_conv, (0, 2, 1, 3)).reshape(n_ffn, -1, D_FF)

    xp = x_prompt.reshape(pb * pl_len, D_MODEL)
    xs = _to_time_major(x_sample)
    p_h, p_rc, p_s, p_fc = [], [], [], []
    s_h, s_rc, s_fc = [], [], []
    s_s = None
    for i in range(depth):
        j = i // 2
        g0, b0 = (lng, 2 * i), (lnb, 2 * i)
        if i % 2 == 0:
            w = ((rg_wi, j), (rg_conv_w, j), (rg_cb, j), (rg_wg, j), (rg_gate_b, j), (rg_lam, j),
                 (rg_wo, j), g0, b0)
            xp, ho, co = _rglru_call(xp, None, None, *w, seq_len=pl_len, alpha=alpha)
            p_h.append(ho[:, 0])
            p_rc.append(co)
            xs, ho, co = _rglru_call(xs, (state_rglru_h, j), (rg_cst, j), *w,
                                     seq_len=sl_len, alpha=alpha)
            s_h.append(ho)
            s_rc.append(co)
        else:
            xp, so = _hgrn_prompt_call(xp, hg, j, g0, b0, seq_len=pl_len, alpha=alpha)
            p_s.append(so)
            xs, s_s = _hgrn_sample_call(xs, state_hgrn_s, s_s, hg, j, g0, b0,
                                        seq_len=sl_len, alpha=alpha)
        w = ((ffn_wi, i), (ffn_conv_w, i), (ffn_cb, i), (ffn_wo, i), (lng, 2 * i + 1), (lnb, 2 * i + 1))
        xp, fo = _ffn_call(xp, None, *w, seq_len=pl_len, alpha=alpha)
        p_fc.append(fo)
        xs, fo = _ffn_call(xs, (ffn_cst, i), *w, seq_len=sl_len, alpha=alpha)
        s_fc.append(fo)

    def stacked_from_time_major(parts):
        a = jnp.stack(parts)
        a = a.reshape(a.shape[0], -1, sb, a.shape[-1])
        return jnp.transpose(a, (0, 2, 1, 3))

    return (xp.reshape(x_prompt.shape), _from_time_major(xs, sb),
            jnp.stack(p_h), jnp.stack(p_rc), jnp.stack(p_s), jnp.stack(p_fc),
            jnp.stack(s_h), stacked_from_time_major(s_rc), s_s, stacked_from_time_major(s_fc))
```
